```python
import jax, jax.numpy as jnp
from jax import lax
import numpy as np

D_MODEL = 1024
BATCH = 8
SEQ = 2048
DEPTH = 2
DEC_BATCH = 128
DEC_SEQ = 1
PAST_LEN = 16384
PAGE_SIZE = 128

R_HEADS = 4
R_DK = 128
R_DV = 128
G_HEADS = 4
G_EXP = 128
G_DV = 128
M_INNER = 2 * D_MODEL
M_HEADS = 4
M_DH = M_INNER // M_HEADS
M_CONV = 4
QKV_BLOCK = 4
D_FF = 2816
CHUNK = 64
EPS = 1e-6
ROPE_BASE = 10000.0
N_EVEN = (DEPTH + 1) // 2
N_ODD = DEPTH // 2
EVEN_IN = 2 * R_HEADS * (R_DK + R_DV) + 2 * G_HEADS * (G_EXP + G_DV)
EVEN_OUT = R_HEADS * R_DV + G_HEADS * G_DV

kernel_name = 'hybrid_retention_hgrn2_mlstm_macaron_step'


def rmsnorm(x, g):
    xf = x.astype(jnp.float32)
    y = xf * lax.rsqrt(jnp.mean(xf * xf, axis=-1, keepdims=True) + EPS)
    return (y * g.astype(jnp.float32)).astype(x.dtype)


def head_norm(x, g, center):
    if center:
        x = x - jnp.mean(x, axis=-1, keepdims=True)
    return x * lax.rsqrt(jnp.mean(x * x, axis=-1, keepdims=True) + EPS) * g.astype(jnp.float32)


def swiglu(x, wg, wu, wd):
    return (jax.nn.silu(x @ wg) * (x @ wu)) @ wd


def rotary(x, pos):
    half = x.shape[-1] // 2
    inv = ROPE_BASE ** (-jnp.arange(half, dtype=jnp.float32) / half)
    ang = pos.astype(jnp.float32)[:, None] * inv[None, :]
    cos = jnp.cos(ang)[None, :, None, :]
    sin = jnp.sin(ang)[None, :, None, :]
    x1, x2 = x[..., :half], x[..., half:]
    return jnp.concatenate([x1 * cos - x2 * sin, x1 * sin + x2 * cos], axis=-1)


def chunk_len(T):
    return CHUNK if T % CHUNK == 0 else T


def to_chunks(x, L):
    B, T = x.shape[:2]
    return jnp.moveaxis(x.reshape((B, T // L, L) + x.shape[2:]), 1, 0)


def from_chunks(y):
    y = jnp.moveaxis(y, 0, 1)
    return y.reshape((y.shape[0], y.shape[1] * y.shape[2]) + y.shape[3:])


def retention_scan(q, k, v, S0, log_gamma):
    T = q.shape[1]
    L = chunk_len(T)
    idx = jnp.arange(L, dtype=jnp.float32)
    dec_in = jnp.exp(log_gamma[None, :] * (idx + 1.0)[:, None])
    dec_out = jnp.exp(log_gamma[None, :] * (L - 1.0 - idx)[:, None])
    diff = idx[:, None] - idx[None, :]
    dmat = jnp.where(diff[None] >= 0, jnp.exp(log_gamma[:, None, None] * jnp.maximum(diff, 0.0)[None]), 0.0)
    g_l = jnp.exp(log_gamma * L)

    def step(S, inp):
        qc, kc, vc = inp
        inter = jnp.einsum('blhk,bhkv->blhv', qc, S) * dec_in[None, :, :, None]
        scores = jnp.einsum('bthk,bshk->bhts', qc, kc) * dmat[None]
        intra = jnp.einsum('bhts,bshv->bthv', scores, vc)
        S_new = g_l[None, :, None, None] * S + jnp.einsum('bshk,bshv->bhkv', kc * dec_out[None, :, :, None], vc)
        return S_new, inter + intra

    S, ys = lax.scan(step, S0, (to_chunks(q, L), to_chunks(k, L), to_chunks(v, L)))
    return from_chunks(ys), S


def hgrn2_scan(q, k, logf, v, S0):
    T = q.shape[1]
    L = chunk_len(T)
    mask = jnp.tril(jnp.ones((L, L), dtype=bool))

    def step(S, inp):
        qc, kc, gc, vc = inp
        bcum = jnp.cumsum(gc, axis=1)
        inter = jnp.einsum('blhe,bhev->blhv', qc * jnp.exp(bcum), S)
        rel = bcum[:, :, None] - bcum[:, None, :]
        rel = jnp.where(mask[None, :, :, None, None], rel, -jnp.inf)
        scores = jnp.einsum('btshe,bshe->bhts', qc[:, :, None] * jnp.exp(rel), kc)
        intra = jnp.einsum('bhts,bshv->bthv', scores, vc)
        btot = bcum[:, -1]
        S_new = jnp.exp(btot)[..., None] * S + jnp.einsum('bshe,bshv->bhev', kc * jnp.exp(btot[:, None] - bcum), vc)
        return S_new, inter + intra

    S, ys = lax.scan(step, S0, (to_chunks(q, L), to_chunks(k, L), to_chunks(logf, L), to_chunks(v, L)))
    return from_chunks(ys), S


def mlstm_scan(q, k, v, ig, lf, C0, n0, m0):
    T = q.shape[1]
    L = chunk_len(T)
    mask = jnp.tril(jnp.ones((L, L), dtype=bool))

    def step(carry, inp):
        C, n, m = carry
        qc, kc, vc, ic, fc = inp
        b = jnp.cumsum(fc, axis=1)
        inter_log = b + m[:, None, :]
        dlog = b[:, :, None, :] - b[:, None, :, :] + ic[:, None, :, :]
        dlog = jnp.where(mask[None, :, :, None], dlog, -jnp.inf)
        m_row = jnp.maximum(inter_log, jnp.max(dlog, axis=2))
        w_inter = jnp.exp(inter_log - m_row)
        qk = jnp.einsum('bthk,bshk->btsh', qc, kc) * jnp.exp(dlog - m_row[:, :, None, :])
        num = w_inter[..., None] * jnp.einsum('bthk,bhkv->bthv', qc, C) + jnp.einsum('btsh,bshv->bthv', qk, vc)
        den = w_inter * jnp.einsum('bthk,bhk->bth', qc, n) + jnp.sum(qk, axis=2)
        den = jnp.maximum(jnp.abs(den), jnp.exp(-m_row))
        h = num / den[..., None]
        b_end = b[:, -1]
        s_log = b_end[:, None] - b + ic
        m_new = jnp.maximum(b_end + m, jnp.max(s_log, axis=1))
        a = jnp.exp(b_end + m - m_new)
        ws = jnp.exp(s_log - m_new[:, None])
        C_new = a[..., None, None] * C + jnp.einsum('bshk,bshv->bhkv', kc * ws[..., None], vc)
        n_new = a[..., None] * n + jnp.einsum('bshk,bsh->bhk', kc, ws)
        return (C_new, n_new, m_new), h

    (C, n, m), hs = lax.scan(step, (C0, n0, m0), (to_chunks(q, L), to_chunks(k, L), to_chunks(v, L), to_chunks(ig, L), to_chunks(lf, L)))
    return from_chunks(hs), C, n, m


def causal_conv(x, buf, w, b):
    T = x.shape[1]
    xp = jnp.concatenate([buf, x], axis=1)
    y = b + w[0] * xp[:, 0:T]
    for j in range(1, M_CONV):
        y = y + w[j] * xp[:, j:j + T]
    return y, xp[:, -(M_CONV - 1):]


def headwise(x, w):
    B, T, _ = x.shape
    xb = x.reshape(B, T, w.shape[0], w.shape[1])
    return jnp.einsum('btni,nio->btno', xb, w).reshape(B, T, -1)


def even_mixer(h, pos, S_ret0, S_hg0, w_in, w_out, ret_g, hg_g, lb):
    f32 = jnp.float32
    B, T, _ = h.shape
    widths = [R_HEADS * R_DK, R_HEADS * R_DK, R_HEADS * R_DV, R_HEADS * R_DV,
              G_HEADS * G_EXP, G_HEADS * G_EXP, G_HEADS * G_DV, G_HEADS * G_DV]
    splits = [int(s) for s in np.cumsum(widths)[:-1]]
    rq, rk, rv, rg, gq, gf, gi, gg = jnp.split(h @ w_in, splits, axis=-1)
    log_gamma = jnp.log(1.0 - 2.0 ** (-5.0 - jnp.arange(R_HEADS, dtype=f32)))
    q = rotary(rq.astype(f32).reshape(B, T, R_HEADS, R_DK), pos)
    k = rotary(rk.astype(f32).reshape(B, T, R_HEADS, R_DK), pos) * (R_DK ** -0.5)
    v = rv.astype(f32).reshape(B, T, R_HEADS, R_DV)
    o_r, S_ret = retention_scan(q, k, v, S_ret0.astype(f32), log_gamma)
    o_r = head_norm(o_r, ret_g, False).reshape(B, T, -1) * jax.nn.silu(rg.astype(f32))
    lbh = lb.reshape(G_HEADS, G_EXP)
    gf_ = gf.astype(f32).reshape(B, T, G_HEADS, G_EXP)
    logf = jnp.log(lbh + (1.0 - lbh) * jax.nn.sigmoid(gf_))
    kk = (1.0 - lbh) * jax.nn.sigmoid(-gf_)
    qq = jax.nn.silu(gq.astype(f32)).reshape(B, T, G_HEADS, G_EXP)
    vv = gi.astype(f32).reshape(B, T, G_HEADS, G_DV)
    o_g, S_hg = hgrn2_scan(qq, kk, logf, vv, S_hg0.astype(f32))
    o_g = head_norm(o_g, hg_g, False).reshape(B, T, -1) * jax.nn.sigmoid(gg.astype(f32))
    y = jnp.concatenate([o_r, o_g], axis=-1).astype(h.dtype) @ w_out
    return y, S_ret, S_hg


def odd_mixer(h, C0, n0, m0, conv0, w_in, conv_w, conv_b, wq, wk, wv, w_ig, b_ig, w_fg, b_fg, norm_g, skip, w_down):
    f32 = jnp.float32
    B, T, _ = h.shape
    xm, z = jnp.split(h @ w_in, 2, axis=-1)
    xc, conv_new = causal_conv(xm, conv0.astype(xm.dtype), conv_w, conv_b)
    xc = jax.nn.silu(xc)
    q = headwise(xc, wq)
    k = headwise(xc, wk)
    v = headwise(xm, wv)
    gate_in = jnp.concatenate([q, k, v], axis=-1)
    ig = (gate_in @ w_ig + b_ig).astype(f32)
    lf = jax.nn.log_sigmoid((gate_in @ w_fg + b_fg).astype(f32))
    rs = lambda a: a.astype(f32).reshape(B, T, M_HEADS, M_DH)
    hc, C, n, m = mlstm_scan(rs(q), rs(k) * (M_DH ** -0.5), rs(v), ig, lf, C0.astype(f32), n0.astype(f32), m0.astype(f32))
    hc = head_norm(hc, norm_g, True).reshape(B, T, M_INNER) + skip.astype(f32) * xc.astype(f32)
    y = (hc * jax.nn.silu(z.astype(f32))).astype(h.dtype) @ w_down
    return y, C, n, m, conv_new


def trunk(x, pos, s_ret, s_hg, s_C, s_n, s_m, s_conv, prm):
    lb_all = jnp.cumsum(jax.nn.softmax(prm['hg_lb_logits'].astype(jnp.float32), axis=0), axis=0)
    o_ret, o_hg, o_C, o_n, o_m, o_conv = [], [], [], [], [], []
    for layer in range(DEPTH):
        ng = prm['norm_g'][layer]
        x = x + 0.5 * swiglu(rmsnorm(x, ng[0]), prm['ffn_w_gate'][layer, 0], prm['ffn_w_up'][layer, 0], prm['ffn_w_down'][layer, 0])
        hn = rmsnorm(x, ng[1])
        if layer % 2 == 0:
            e = layer // 2
            y, sr, sh = even_mixer(hn, pos, s_ret[:, e], s_hg[:, e], prm['ev_w_in'][e], prm['ev_w_out'][e],
                                   prm['ret_norm_g'][e], prm['hg_norm_g'][e], lb_all[layer])
            o_ret.append(sr)
            o_hg.append(sh)
        else:
            o = layer // 2
            y, c, n, m, cv = odd_mixer(hn, s_C[:, o], s_n[:, o], s_m[:, o], s_conv[:, o],
                                       prm['ml_w_in'][o], prm['ml_conv_w'][o], prm['ml_conv_b'][o],
                                       prm['ml_w_q'][o], prm['ml_w_k'][o], prm['ml_w_v'][o],
                                       prm['ml_w_ig'][o], prm['ml_b_ig'][o], prm['ml_w_fg'][o], prm['ml_b_fg'][o],
                                       prm['ml_norm_g'][o], prm['ml_skip'][o], prm['ml_w_down'][o])
            o_C.append(c)
            o_n.append(n)
            o_m.append(m)
            o_conv.append(cv)
        x = x + y
        x = x + 0.5 * swiglu(rmsnorm(x, ng[2]), prm['ffn_w_gate'][layer, 1], prm['ffn_w_up'][layer, 1], prm['ffn_w_down'][layer, 1])
    y = rmsnorm(x, prm['final_norm_g'])
    st = lambda lst: jnp.stack(lst, axis=1).astype(x.dtype)
    return y, st(o_ret), st(o_hg), st(o_C), st(o_n), st(o_m), st(o_conv)


def setup_inputs(seed: int = 0) -> dict:
    key = jax.random.key(seed)
    ks = iter(list(jax.random.split(key, 40)))
    f32 = jnp.float32
    nrm = lambda shape, s: jax.random.normal(next(ks), shape, f32) * s
    gain = lambda shape: 1.0 + nrm(shape, 0.02)
    return {
        'x_prompt': nrm((BATCH, SEQ, D_MODEL), 1.0),
        'x_sample': nrm((DEC_BATCH, DEC_SEQ, D_MODEL), 1.0),
        'state_ret': nrm((DEC_BATCH, N_EVEN, R_HEADS, R_DK, R_DV), 0.1),
        'state_hgrn': nrm((DEC_BATCH, N_EVEN, G_HEADS, G_EXP, G_DV), 0.5),
        'state_mlstm_C': nrm((DEC_BATCH, N_ODD, M_HEADS, M_DH, M_DH), 0.05),
        'state_mlstm_n': nrm((DEC_BATCH, N_ODD, M_HEADS, M_DH), 0.05),
        'state_mlstm_m': nrm((DEC_BATCH, N_ODD, M_HEADS), 1.0),
        'state_mlstm_conv': nrm((DEC_BATCH, N_ODD, M_CONV - 1, M_INNER), 1.0),
        'norm_g': gain((DEPTH, 3, D_MODEL)),
        'final_norm_g': gain((D_MODEL,)),
        'ffn_w_gate': nrm((DEPTH, 2, D_MODEL, D_FF), D_MODEL ** -0.5),
        'ffn_w_up': nrm((DEPTH, 2, D_MODEL, D_FF), D_MODEL ** -0.5),
        'ffn_w_down': nrm((DEPTH, 2, D_FF, D_MODEL), D_FF ** -0.5),
        'ev_w_in': nrm((N_EVEN, D_MODEL, EVEN_IN), D_MODEL ** -0.5),
        'ev_w_out': nrm((N_EVEN, EVEN_OUT, D_MODEL), EVEN_OUT ** -0.5),
        'ret_norm_g': gain((N_EVEN, R_HEADS, R_DV)),
        'hg_norm_g': gain((N_EVEN, G_HEADS, G_DV)),
        'hg_lb_logits': nrm((DEPTH + 1, G_HEADS * G_EXP), 0.5),
        'ml_w_in': nrm((N_ODD, D_MODEL, 2 * M_INNER), D_MODEL ** -0.5),
        'ml_conv_w': nrm((N_ODD, M_CONV, M_INNER), 0.5),
        'ml_conv_b': nrm((N_ODD, M_INNER), 0.01),
        'ml_w_q': nrm((N_ODD, M_INNER // QKV_BLOCK, QKV_BLOCK, QKV_BLOCK), QKV_BLOCK ** -0.5),
        'ml_w_k': nrm((N_ODD, M_INNER // QKV_BLOCK, QKV_BLOCK, QKV_BLOCK), QKV_BLOCK ** -0.5),
        'ml_w_v': nrm((N_ODD, M_INNER // QKV_BLOCK, QKV_BLOCK, QKV_BLOCK), QKV_BLOCK ** -0.5),
        'ml_w_ig': nrm((N_ODD, 3 * M_INNER, M_HEADS), 0.3 * (3 * M_INNER) ** -0.5),
        'ml_b_ig': nrm((N_ODD, M_HEADS), 0.1),
        'ml_w_fg': nrm((N_ODD, 3 * M_INNER, M_HEADS), 0.3 * (3 * M_INNER) ** -0.5),
        'ml_b_fg': jnp.linspace(3.0, 6.0, M_HEADS, dtype=f32)[None, :] + nrm((N_ODD, M_HEADS), 0.01),
        'ml_norm_g': gain((N_ODD, M_HEADS, M_DH)),
        'ml_skip': gain((N_ODD, M_INNER)),
        'ml_w_down': nrm((N_ODD, M_INNER, D_MODEL), M_INNER ** -0.5),
    }


def reference(x_prompt, x_sample, state_ret, state_hgrn, state_mlstm_C, state_mlstm_n, state_mlstm_m, state_mlstm_conv,
              norm_g, final_norm_g, ffn_w_gate, ffn_w_up, ffn_w_down, ev_w_in, ev_w_out, ret_norm_g, hg_norm_g,
              hg_lb_logits, ml_w_in, ml_conv_w, ml_conv_b, ml_w_q, ml_w_k, ml_w_v, ml_w_ig, ml_b_ig, ml_w_fg,
              ml_b_fg, ml_norm_g, ml_skip, ml_w_down):
    prm = {'norm_g': norm_g, 'final_norm_g': final_norm_g, 'ffn_w_gate': ffn_w_gate, 'ffn_w_up': ffn_w_up,
           'ffn_w_down': ffn_w_down, 'ev_w_in': ev_w_in, 'ev_w_out': ev_w_out, 'ret_norm_g': ret_norm_g,
           'hg_norm_g': hg_norm_g, 'hg_lb_logits': hg_lb_logits, 'ml_w_in': ml_w_in, 'ml_conv_w': ml_conv_w,
           'ml_conv_b': ml_conv_b, 'ml_w_q': ml_w_q, 'ml_w_k': ml_w_k, 'ml_w_v': ml_w_v, 'ml_w_ig': ml_w_ig,
           'ml_b_ig': ml_b_ig, 'ml_w_fg': ml_w_fg, 'ml_b_fg': ml_b_fg, 'ml_norm_g': ml_norm_g,
           'ml_skip': ml_skip, 'ml_w_down': ml_w_down}
    f32 = jnp.float32
    bp, tp = x_prompt.shape[0], x_prompt.shape[1]
    z_ret = jnp.zeros((bp, N_EVEN, R_HEADS, R_DK, R_DV), f32)
    z_hg = jnp.zeros((bp, N_EVEN, G_HEADS, G_EXP, G_DV), f32)
    z_C = jnp.zeros((bp, N_ODD, M_HEADS, M_DH, M_DH), f32)
    z_n = jnp.zeros((bp, N_ODD, M_HEADS, M_DH), f32)
    z_m = jnp.zeros((bp, N_ODD, M_HEADS), f32)
    z_conv = jnp.zeros((bp, N_ODD, M_CONV - 1, M_INNER), x_prompt.dtype)
    pos_p = jnp.arange(tp, dtype=jnp.int32)
    y_prompt, ret_p, hg_p, C_p, n_p, m_p, conv_p = trunk(x_prompt, pos_p, z_ret, z_hg, z_C, z_n, z_m, z_conv, prm)
    pos_s = PAST_LEN + jnp.arange(x_sample.shape[1], dtype=jnp.int32)
    y_sample, ret_s, hg_s, C_s, n_s, m_s, conv_s = trunk(x_sample, pos_s, state_ret, state_hgrn, state_mlstm_C,
                                                         state_mlstm_n, state_mlstm_m, state_mlstm_conv, prm)
    return (y_prompt, y_sample, ret_p, hg_p, C_p, n_p, m_p, conv_p, ret_s, hg_s, C_s, n_s, m_s, conv_s)
```

```python
import functools
import math

import numpy as np
import jax
import jax.numpy as jnp
from jax import lax
from jax.experimental import pallas as pl
from jax.experimental.pallas import tpu as pltpu

D_MODEL = 1024
PAST_LEN = 16384
R_HEADS = 4
R_DK = 128
R_DV = 128
G_HEADS = 4
G_EXP = 128
G_DV = 128
M_INNER = 2 * D_MODEL
M_HEADS = 4
M_DH = M_INNER // M_HEADS
M_CONV = 4
QKV_BLOCK = 4
D_FF = 2816
EPS = 1e-6
ROPE_BASE = 10000.0
EVEN_IN = 4096
EVEN_OUT = 1024

F32 = jnp.float32
BF16 = jnp.bfloat16

VMEM_LIMIT_BYTES = 56 * 1024 * 1024

HG_CHUNK = 64
HG_SUB = 16
FFN_SPLIT = 2
BD = 256


def _nt(a, b):
    return lax.dot_general(a, b, (((1,), (1,)), ((), ())), preferred_element_type=F32)


def _tn(a, b):
    return lax.dot_general(a, b, (((0,), (0,)), ((), ())), preferred_element_type=F32)


def _mm(a, b):
    return jnp.dot(a, b, preferred_element_type=F32)


def _sigmoid(x):
    return 1.0 / (1.0 + jnp.exp(-x))


def _silu(x):
    return x * _sigmoid(x)


def _log_sigmoid(x):
    return jnp.minimum(x, 0.0) - jnp.log(1.0 + jnp.exp(-jnp.abs(x)))


def _rms(x, g):
    return x * lax.rsqrt(jnp.mean(x * x, axis=-1, keepdims=True) + EPS) * g


def _head_norm(x, g, center):
    if center:
        x = x - jnp.mean(x, axis=-1, keepdims=True)
    return x * lax.rsqrt(jnp.mean(x * x, axis=-1, keepdims=True) + EPS) * g


def _rotary(x, cos, sin_signed):
    return x * cos + pltpu.roll(x, 64, 1) * sin_signed


def _const_spec(shape):
    n = len(shape)
    return pl.BlockSpec(shape, lambda *_: (0,) * n, pipeline_mode=pl.Buffered(1))


def _params(sem):
    return pltpu.CompilerParams(dimension_semantics=sem, vmem_limit_bytes=VMEM_LIMIT_BYTES)


def _ffn_kernel(x_ref, g_ref, wg_ref, wu_ref, wd_ref, *rest, final):
    if final:
        gfin_ref, o_ref = rest
    else:
        (o_ref,) = rest
    x = x_ref[...]
    h = _rms(x, g_ref[...]).astype(BF16)
    fc = D_FF // FFN_SPLIT
    y = jnp.zeros_like(x)
    for j in range(FFN_SPLIT):
        gt = _mm(h, wg_ref[:, j * fc:(j + 1) * fc])
        ut = _mm(h, wu_ref[:, j * fc:(j + 1) * fc])
        a = (_silu(gt) * ut).astype(BF16)
        y = y + _mm(a, wd_ref[j * fc:(j + 1) * fc, :])
    out = x + 0.5 * y
    if final:
        out = _rms(out, gfin_ref[...])
    o_ref[...] = out


def _ffn(x, g, wg, wu, wd, tm, gfin=None):
    n = x.shape[0]
    final = gfin is not None
    in_specs = [pl.BlockSpec((tm, D_MODEL), lambda i: (i, 0)),
                _const_spec((1, D_MODEL)),
                _const_spec((D_MODEL, D_FF)),
                _const_spec((D_MODEL, D_FF)),
                _const_spec((D_FF, D_MODEL))]
    args = [x, g.reshape(1, D_MODEL), wg, wu, wd]
    if final:
        in_specs.append(_const_spec((1, D_MODEL)))
        args.append(gfin.reshape(1, D_MODEL))
    return pl.pallas_call(
        functools.partial(_ffn_kernel, final=final),
        grid=(n // tm,),
        in_specs=in_specs,
        out_specs=pl.BlockSpec((tm, D_MODEL), lambda i: (i, 0)),
        out_shape=jax.ShapeDtypeStruct((n, D_MODEL), F32),
        compiler_params=_params(("arbitrary",)),
        name="ffn_final" if final else "ffn",
    )(*args)


def _ret_log_gamma(h):
    return math.log(1.0 - 2.0 ** (-5.0 - h))


def _chunk_cumsum(v, rin, chunk):
    d = 1
    while d < chunk:
        v = v + jnp.where(rin >= d, pltpu.roll(v, d, 0), 0.0)
        d *= 2
    return v


def _hgrn_gates(gq, gf, lbh):
    sg = _sigmoid(gf)
    logf = jnp.log(lbh + (1.0 - lbh) * sg)
    kk = (1.0 - lbh) * _sigmoid(-gf)
    qq = _silu(gq)
    return qq, kk, logf


def _even_prompt_kernel(x_ref, g_ref, win_ref, wout_ref, cos_ref, sin_ref, retg_ref, hgg_ref, lb_ref,
                        y_ref, sret_ref, shg_ref, dmat_ref, st_ref, *, tt):
    b_id = pl.program_id(0)
    t_id = pl.program_id(1)
    n_t = pl.num_programs(1)
    hd = 128

    @pl.when(jnp.logical_and(b_id == 0, t_id == 0))
    def _():
        ti = lax.broadcasted_iota(jnp.int32, (tt, tt), 0)
        si = lax.broadcasted_iota(jnp.int32, (tt, tt), 1)
        diff = (ti - si).astype(F32)
        for h in range(R_HEADS):
            dmat_ref[h] = jnp.where(diff >= 0.0, jnp.exp(_ret_log_gamma(h) * jnp.maximum(diff, 0.0)), 0.0)

    @pl.when(t_id == 0)
    def _():
        sret_ref[...] = jnp.zeros_like(sret_ref)
        st_ref[...] = jnp.zeros_like(st_ref)

    x = x_ref[0]
    hn = _rms(x, g_ref[...]).astype(BF16)
    p = _mm(hn, win_ref[...])
    cos = cos_ref[...]
    sin = sin_ref[...]

    row = lax.broadcasted_iota(jnp.int32, (tt, hd), 0)
    rowf = row.astype(F32)
    outs = []

    for h in range(R_HEADS):
        lg = _ret_log_gamma(h)
        q = _rotary(p[:, h * hd:(h + 1) * hd], cos, sin)
        k = _rotary(p[:, 512 + h * hd:512 + (h + 1) * hd], cos, sin) * (R_DK ** -0.5)
        v = p[:, 1024 + h * hd:1024 + (h + 1) * hd]
        rg = p[:, 1536 + h * hd:1536 + (h + 1) * hd]
        s0 = sret_ref[0, h]
        qb = q.astype(BF16)
        vb = v.astype(BF16)
        inter = _mm(qb, s0.astype(BF16)) * jnp.exp(lg * (rowf + 1.0))
        scores = _nt(qb, k.astype(BF16)) * dmat_ref[h]
        intra = _mm(scores.astype(BF16), vb)
        kd = (k * jnp.exp(lg * (tt - 1.0 - rowf))).astype(BF16)
        sret_ref[0, h] = math.exp(lg * tt) * s0 + _tn(kd, vb)
        o = _head_norm(inter + intra, retg_ref[h:h + 1, :], False) * _silu(rg)
        outs.append(o)

    rin = row % HG_CHUNK
    rsub = row % HG_SUB
    sub = (lax.broadcasted_iota(jnp.int32, (HG_CHUNK, hd), 0)) // HG_SUB
    n_sub = HG_CHUNK // HG_SUB
    for h in range(G_HEADS):
        lbh = lb_ref[h:h + 1, :]
        qq, kk, logf = _hgrn_gates(p[:, 2048 + h * hd:2048 + (h + 1) * hd],
                                   p[:, 2560 + h * hd:2560 + (h + 1) * hd], lbh)
        vv = p[:, 3072 + h * hd:3072 + (h + 1) * hd]
        gg = p[:, 3584 + h * hd:3584 + (h + 1) * hd]
        bc_all = _chunk_cumsum(logf, rin, HG_CHUNK)
        o_band = jnp.sum(qq * kk, axis=-1, keepdims=True) * vv
        for d in range(1, HG_SUB):
            kd = pltpu.roll(kk, d, 0)
            bd = pltpu.roll(bc_all, d, 0)
            vd = pltpu.roll(vv, d, 0)
            term = jnp.where(rsub >= d, qq * kd * jnp.exp(bc_all - bd), 0.0)
            o_band = o_band + jnp.sum(term, axis=-1, keepdims=True) * vd
        st = st_ref[h]
        o_chunks = []
        for c in range(tt // HG_CHUNK):
            r0 = c * HG_CHUNK
            bc = bc_all[r0:r0 + HG_CHUNK]
            qc = qq[r0:r0 + HG_CHUNK]
            kc = kk[r0:r0 + HG_CHUNK]
            vcb = vv[r0:r0 + HG_CHUNK].astype(BF16)
            refs = [bc[i * HG_SUB - 1:i * HG_SUB] for i in range(1, n_sub)]
            refrow = refs[-1]
            for i in range(n_sub - 2, 0, -1):
                refrow = jnp.where(sub == i, refs[i - 1], refrow)
            qp = qc * jnp.exp(bc - refrow)
            lhs = jnp.concatenate([jnp.where(sub == i, qp, 0.0) for i in range(1, n_sub)], axis=1)
            kcat = jnp.concatenate([jnp.where(sub < i, kc * jnp.exp(refs[i - 1] - bc), 0.0)
                                    for i in range(1, n_sub)], axis=1)
            a = _nt(lhs.astype(BF16), kcat.astype(BF16))
            qb = (qc * jnp.exp(bc)).astype(BF16)
            o_chunks.append(_mm(a.astype(BF16), vcb) + _nt(qb, st.astype(BF16)))
            btot = bc[HG_CHUNK - 1:HG_CHUNK]
            ke = (kc * jnp.exp(btot - bc)).astype(BF16)
            st = st * jnp.exp(btot) + _tn(vcb, ke)
        st_ref[h] = st
        o = o_band + jnp.concatenate(o_chunks, axis=0)
        o = _head_norm(o, hgg_ref[h:h + 1, :], False) * _sigmoid(gg)
        outs.append(o)

    ycat = jnp.concatenate(outs, axis=1).astype(BF16)
    y_ref[0] = x + _mm(ycat, wout_ref[...])

    @pl.when(t_id == n_t - 1)
    def _():
        for h in range(G_HEADS):
            shg_ref[0, h] = st_ref[h].T


def _rope_tables(pos):
    half = R_DK // 2
    inv = ROPE_BASE ** (-jnp.arange(half, dtype=F32) / half)
    ang = pos.astype(F32)[:, None] * inv[None, :]
    cos = jnp.cos(ang)
    sin = jnp.sin(ang)
    return jnp.concatenate([cos, cos], axis=-1), jnp.concatenate([-sin, sin], axis=-1)


def _even_prompt(x, g, win, wout, retg, hgg, lb, tt):
    bsz, seq, _ = x.shape
    cos, sin = _rope_tables(jnp.arange(seq, dtype=jnp.int32))
    state_spec = pl.BlockSpec((1, 4, 128, 128), lambda b, t: (b, 0, 0, 0))
    return pl.pallas_call(
        functools.partial(_even_prompt_kernel, tt=tt),
        grid=(bsz, seq // tt),
        in_specs=[pl.BlockSpec((1, tt, D_MODEL), lambda b, t: (b, t, 0)),
                  _const_spec((1, D_MODEL)),
                  _const_spec((D_MODEL, EVEN_IN)),
                  _const_spec((EVEN_OUT, D_MODEL)),
                  pl.BlockSpec((tt, 128), lambda b, t: (t, 0)),
                  pl.BlockSpec((tt, 128), lambda b, t: (t, 0)),
                  _const_spec((4, 128)), _const_spec((4, 128)), _const_spec((4, 128))],
        out_specs=[pl.BlockSpec((1, tt, D_MODEL), lambda b, t: (b, t, 0)), state_spec, state_spec],
        out_shape=[jax.ShapeDtypeStruct(x.shape, F32),
                   jax.ShapeDtypeStruct((bsz, 4, 128, 128), F32),
                   jax.ShapeDtypeStruct((bsz, 4, 128, 128), F32)],
        scratch_shapes=[pltpu.VMEM((4, tt, tt), F32), pltpu.VMEM((4, 128, 128), F32)],
        compiler_params=_params(("arbitrary", "arbitrary")),
        name="even_prompt",
    )(x, g.reshape(1, D_MODEL), win, wout, cos, sin, retg, hgg, lb)


SB = 8


def _even_sample_kernel(x_ref, g_ref, win_ref, wout_ref, cos_ref, sin_ref, retg_ref, hgg_ref, lb_ref,
                        sret_in, shg_in, y_ref, sret_out, shg_out, p_ref, o_ref):
    i = pl.program_id(0)
    n_i = pl.num_programs(0)
    hd = 128

    @pl.when(i == 0)
    def _():
        hn = _rms(x_ref[...], g_ref[...]).astype(BF16)
        p_ref[...] = _mm(hn, win_ref[...])

    r0 = pl.multiple_of(i * SB, SB)
    p = p_ref[pl.ds(r0, SB), :]
    cos = cos_ref[...]
    sin = sin_ref[...]
    row = lax.broadcasted_iota(jnp.int32, (SB, hd), 0)
    outs = []
    for h in range(R_HEADS):
        gamma = math.exp(_ret_log_gamma(h))
        q = _rotary(p[:, h * hd:(h + 1) * hd], cos, sin).astype(BF16)
        k = _rotary(p[:, 512 + h * hd:512 + (h + 1) * hd], cos, sin) * (R_DK ** -0.5)
        vb = p[:, 1024 + h * hd:1024 + (h + 1) * hd].astype(BF16)
        rg = p[:, 1536 + h * hd:1536 + (h + 1) * hd]
        o = jnp.zeros((SB, hd), F32)
        for j in range(SB):
            kj = jnp.where(row == j, k, 0.0).astype(BF16)
            s_new = gamma * sret_in[j, h] + _tn(kj, vb)
            sret_out[j, h] = s_new
            o = jnp.where(row == j, _mm(q, s_new.astype(BF16)), o)
        outs.append(_head_norm(o, retg_ref[h:h + 1, :], False) * _silu(rg))
    for h in range(G_HEADS):
        lbh = lb_ref[h:h + 1, :]
        qq, kk, logf = _hgrn_gates(p[:, 2048 + h * hd:2048 + (h + 1) * hd],
                                   p[:, 2560 + h * hd:2560 + (h + 1) * hd], lbh)
        vb = p[:, 3072 + h * hd:3072 + (h + 1) * hd].astype(BF16)
        gg = p[:, 3584 + h * hd:3584 + (h + 1) * hd]
        f_cols = jnp.concatenate([jnp.exp(logf), jnp.zeros((hd - SB, hd), F32)], axis=0).T
        qb = qq.astype(BF16)
        o = jnp.zeros((SB, hd), F32)
        for j in range(SB):
            kj = jnp.where(row == j, kk, 0.0).astype(BF16)
            s_new = f_cols[:, j:j + 1] * shg_in[j, h] + _tn(kj, vb)
            shg_out[j, h] = s_new
            o = jnp.where(row == j, _mm(qb, s_new.astype(BF16)), o)
        outs.append(_head_norm(o, hgg_ref[h:h + 1, :], False) * _sigmoid(gg))
    o_ref[pl.ds(r0, SB), :] = jnp.concatenate(outs, axis=1)

    @pl.when(i == n_i - 1)
    def _():
        y_ref[...] = x_ref[...] + _mm(o_ref[...].astype(BF16), wout_ref[...])


def _even_sample(x, g, win, wout, retg, hgg, lb, sret, shg):
    n = x.shape[0]
    cos, sin = _rope_tables(jnp.full((1,), PAST_LEN, dtype=jnp.int32))
    state_spec = pl.BlockSpec((SB, 4, 128, 128), lambda i: (i, 0, 0, 0))
    return pl.pallas_call(
        _even_sample_kernel,
        grid=(n // SB,),
        in_specs=[_const_spec((n, D_MODEL)),
                  _const_spec((1, D_MODEL)),
                  _const_spec((D_MODEL, EVEN_IN)),
                  _const_spec((EVEN_OUT, D_MODEL)),
                  _const_spec((1, 128)), _const_spec((1, 128)),
                  _const_spec((4, 128)), _const_spec((4, 128)), _const_spec((4, 128)),
                  state_spec, state_spec],
        out_specs=[pl.BlockSpec((n, D_MODEL), lambda i: (0, 0)), state_spec, state_spec],
        out_shape=[jax.ShapeDtypeStruct((n, D_MODEL), F32),
                   jax.ShapeDtypeStruct(sret.shape, F32),
                   jax.ShapeDtypeStruct(shg.shape, F32)],
        scratch_shapes=[pltpu.VMEM((n, EVEN_IN), F32), pltpu.VMEM((n, EVEN_OUT), F32)],
        compiler_params=_params(("arbitrary",)),
        name="even_sample",
    )(x, g.reshape(1, D_MODEL), win, wout, cos, sin, retg, hgg, lb, sret, shg)


def _block_diag(w):
    per = BD // QKV_BLOCK
    wg = w.reshape(M_INNER // BD, per, QKV_BLOCK, QKV_BLOCK)
    eye = jnp.eye(per, dtype=w.dtype)
    return jnp.einsum('gaio,ab->gaibo', wg, eye).reshape(M_INNER // BD, BD, BD)


def _headwise(xb, w_ref):
    return jnp.concatenate([_mm(xb[:, g * BD:(g + 1) * BD], w_ref[g]) for g in range(M_INNER // BD)], axis=1)


def _gate_weights(w_ig, w_fg):
    w = jnp.concatenate([w_ig, w_fg], axis=1)
    w = jnp.pad(w, ((0, 0), (0, 128 - 2 * M_HEADS)))
    return w.reshape(3, M_INNER, 128)


def _mlstm_out(hs, xc, z, normg_ref, skip_ref, wdown_ref):
    hc = jnp.concatenate([_head_norm(hs[h], normg_ref[:, h * M_DH:(h + 1) * M_DH], True) for h in range(M_HEADS)],
                         axis=1)
    hc = hc + skip_ref[...] * xc
    return _mm((hc * _silu(z)).astype(BF16), wdown_ref[...])


def _odd_prompt_kernel(x_ref, g_ref, win_ref, convw_ref, convb_ref, wq_ref, wk_ref, wv_ref,
                       wgate_ref, wgate_t_ref, bgate_ref, bgate_t_ref, normg_ref, skip_ref, wdown_ref,
                       y_ref, c_ref, n_ref, m_out_ref, conv_out_ref, carry_ref, m_ref, *, tt):
    t_id = pl.program_id(1)
    n_t = pl.num_programs(1)

    @pl.when(t_id == 0)
    def _():
        c_ref[...] = jnp.zeros_like(c_ref)
        n_ref[...] = jnp.zeros_like(n_ref)
        m_ref[...] = jnp.zeros_like(m_ref)
        carry_ref[...] = jnp.zeros_like(carry_ref)

    x = x_ref[0]
    hn = _rms(x, g_ref[...]).astype(BF16)
    p = _mm(hn, win_ref[...])
    xm = p[:, :M_INNER]
    z = p[:, M_INNER:]

    carry = carry_ref[...]
    row8 = lax.broadcasted_iota(jnp.int32, (8, M_INNER), 0)
    conv = convb_ref[...] + convw_ref[M_CONV - 1:M_CONV, :] * xm
    for j in range(1, M_CONV):
        rolled = pltpu.roll(xm, j, 0)
        head = jnp.where(row8 < j, pltpu.roll(carry, j, 0), rolled[:8])
        shifted = jnp.concatenate([head, rolled[8:]], axis=0)
        conv = conv + convw_ref[M_CONV - 1 - j:M_CONV - j, :] * shifted
    carry_ref[...] = xm[tt - 8:, :]
    xc = _silu(conv)

    xcb = xc.astype(BF16)
    q = _headwise(xcb, wq_ref)
    k = _headwise(xcb, wk_ref)
    v = _headwise(xm.astype(BF16), wv_ref)
    qb = q.astype(BF16)
    kb = k.astype(BF16)
    vb = v.astype(BF16)
    gates = (_mm(qb, wgate_ref[0]) + _mm(kb, wgate_ref[1]) + _mm(vb, wgate_ref[2])) + bgate_ref[...]
    gates_t = (_nt(wgate_t_ref[0], qb) + _nt(wgate_t_ref[1], kb) + _nt(wgate_t_ref[2], vb)) + bgate_t_ref[...]

    rowg = lax.broadcasted_iota(jnp.int32, (tt, 128), 0)
    lane8 = lax.broadcasted_iota(jnp.int32, (8, tt), 1)
    bcol_all = _chunk_cumsum(_log_sigmoid(gates), rowg, tt)
    brow_all = _log_sigmoid(gates_t)
    d = 1
    while d < tt:
        brow_all = brow_all + jnp.where(lane8 >= d, pltpu.roll(brow_all, d, 1), 0.0)
        d *= 2

    ti = lax.broadcasted_iota(jnp.int32, (tt, tt), 0)
    si = lax.broadcasted_iota(jnp.int32, (tt, tt), 1)
    causal = si <= ti
    hs = []
    for h in range(M_HEADS):
        sl = slice(h * M_DH, (h + 1) * M_DH)
        qh = q[:, sl]
        kh = k[:, sl] * (M_DH ** -0.5)
        qhb = qh.astype(BF16)
        khb = kh.astype(BF16)
        vhb = vb[:, sl]
        ig_col = gates[:, h:h + 1]
        b_col = bcol_all[:, M_HEADS + h:M_HEADS + h + 1]
        ig_row = gates_t[h:h + 1, :]
        b_row = brow_all[M_HEADS + h:M_HEADS + h + 1, :]
        m_prev = m_ref[h:h + 1, 0:1]
        c_prev = c_ref[0, h]
        n_prev = n_ref[0, h:h + 1, :]

        dlog = jnp.where(causal, b_col - b_row + ig_row, -jnp.inf)
        inter_log = b_col + m_prev
        m_row = jnp.maximum(inter_log, jnp.max(dlog, axis=-1, keepdims=True))
        w_inter = jnp.exp(inter_log - m_row)
        qk = _nt(qhb, khb) * jnp.exp(dlog - m_row)
        num = w_inter * _mm(qhb, c_prev.astype(BF16)) + _mm(qk.astype(BF16), vhb)
        den = w_inter * jnp.sum(qh * n_prev, axis=-1, keepdims=True) + jnp.sum(qk, axis=-1, keepdims=True)
        den = jnp.maximum(jnp.abs(den), jnp.exp(-m_row))
        hs.append(num / den)

        b_end = b_col[tt - 1:tt, :]
        s_log = b_end - b_col + ig_col
        m_new = jnp.maximum(b_end + m_prev, jnp.max(s_log, axis=0, keepdims=True))
        a = jnp.exp(b_end + m_prev - m_new)
        kw = kh * jnp.exp(s_log - m_new)
        c_ref[0, h] = a * c_prev + _tn(kw.astype(BF16), vhb)
        n_ref[0, h:h + 1, :] = a * n_prev + jnp.sum(kw, axis=0, keepdims=True)
        m_ref[h:h + 1, :] = jnp.broadcast_to(m_new, (1, 128))

    y_ref[0] = x + _mlstm_out(hs, xc, z, normg_ref, skip_ref, wdown_ref)

    @pl.when(t_id == n_t - 1)
    def _():
        m_out_ref[0] = m_ref[...]
        conv_out_ref[0] = xm[tt - (M_CONV - 1):, :]


def _odd_prompt(x, g, ml, tt):
    bsz, seq, _ = x.shape
    bt = jnp.broadcast_to(ml['bgate_col'], (8, tt))
    return pl.pallas_call(
        functools.partial(_odd_prompt_kernel, tt=tt),
        grid=(bsz, seq // tt),
        in_specs=[pl.BlockSpec((1, tt, D_MODEL), lambda b, t: (b, t, 0)),
                  _const_spec((1, D_MODEL)),
                  _const_spec((D_MODEL, 2 * M_INNER)),
                  _const_spec((M_CONV, M_INNER)),
                  _const_spec((1, M_INNER)),
                  _const_spec((M_INNER // BD, BD, BD)),
                  _const_spec((M_INNER // BD, BD, BD)),
                  _const_spec((M_INNER // BD, BD, BD)),
                  _const_spec((3, M_INNER, 128)),
                  _const_spec((3, 8, M_INNER)),
                  _const_spec((1, 128)),
                  _const_spec((8, tt)),
                  _const_spec((1, M_INNER)),
                  _const_spec((1, M_INNER)),
                  _const_spec((M_INNER, D_MODEL))],
        out_specs=[pl.BlockSpec((1, tt, D_MODEL), lambda b, t: (b, t, 0)),
                   pl.BlockSpec((1, M_HEADS, M_DH, M_DH), lambda b, t: (b, 0, 0, 0)),
                   pl.BlockSpec((1, M_HEADS, M_DH), lambda b, t: (b, 0, 0)),
                   pl.BlockSpec((1, 8, 128), lambda b, t: (b, 0, 0)),
                   pl.BlockSpec((1, M_CONV - 1, M_INNER), lambda b, t: (b, 0, 0))],
        out_shape=[jax.ShapeDtypeStruct(x.shape, F32),
                   jax.ShapeDtypeStruct((bsz, M_HEADS, M_DH, M_DH), F32),
                   jax.ShapeDtypeStruct((bsz, M_HEADS, M_DH), F32),
                   jax.ShapeDtypeStruct((bsz, 8, 128), F32),
                   jax.ShapeDtypeStruct((bsz, M_CONV - 1, M_INNER), F32)],
        scratch_shapes=[pltpu.VMEM((8, M_INNER), F32), pltpu.VMEM((8, 128), F32)],
        compiler_params=_params(("arbitrary", "arbitrary")),
        name="odd_prompt",
    )(x, g.reshape(1, D_MODEL), ml['win'], ml['convw'], ml['convb'], ml['wq'], ml['wk'], ml['wv'],
      ml['wgate'], ml['wgate_t'], ml['bgate'], bt, ml['normg'], ml['skip'], ml['wdown'])


def _pick_row(ref, b):
    r0 = pl.multiple_of((b // 8) * 8, 8)
    blk = ref[pl.ds(r0, 8), :]
    row = lax.broadcasted_iota(jnp.int32, blk.shape, 0)
    return jnp.sum(jnp.where(row == b % 8, blk, 0.0), axis=0, keepdims=True)


def _put_row(ref, b, val):
    r0 = pl.multiple_of((b // 8) * 8, 8)
    blk = ref[pl.ds(r0, 8), :]
    row = lax.broadcasted_iota(jnp.int32, blk.shape, 0)
    ref[pl.ds(r0, 8), :] = jnp.where(row == b % 8, jnp.broadcast_to(val, blk.shape), blk)


def _odd_sample_kernel(x_ref, g_ref, win_ref, convw_ref, convb_ref, cv0_ref, cv1_ref, cv2_ref,
                       wq_ref, wk_ref, wv_ref, wgate_ref, bgate_ref, normg_ref, skip_ref, wdown_ref,
                       m_in_ref, c_in, n_in,
                       y_ref, c_out, n_out, m_out_ref, xm_out_ref,
                       q_ref, k_ref, v_ref, gate_ref, xc_ref, z_ref, h_ref):
    b = pl.program_id(0)
    n_b = pl.num_programs(0)

    @pl.when(b == 0)
    def _():
        hn = _rms(x_ref[...], g_ref[...]).astype(BF16)
        p = _mm(hn, win_ref[...])
        xm = p[:, :M_INNER]
        z_ref[...] = p[:, M_INNER:]
        xm_out_ref[...] = xm
        conv = (convb_ref[...] + convw_ref[3:4, :] * xm + convw_ref[2:3, :] * cv2_ref[...]
                + convw_ref[1:2, :] * cv1_ref[...] + convw_ref[0:1, :] * cv0_ref[...])
        xc = _silu(conv)
        xc_ref[...] = xc
        xcb = xc.astype(BF16)
        q = _headwise(xcb, wq_ref)
        k = _headwise(xcb, wk_ref)
        v = _headwise(xm.astype(BF16), wv_ref)
        q_ref[...] = q
        k_ref[...] = k * (M_DH ** -0.5)
        v_ref[...] = v
        gate_ref[...] = (_mm(q.astype(BF16), wgate_ref[0]) + _mm(k.astype(BF16), wgate_ref[1])
                         + _mm(v.astype(BF16), wgate_ref[2])) + bgate_ref[...]
        m_out_ref[...] = jnp.zeros_like(m_out_ref)
        h_ref[...] = jnp.zeros_like(h_ref)

    r0 = pl.multiple_of((b // 8) * 8, 8)
    row8 = lax.broadcasted_iota(jnp.int32, (8, M_INNER), 0)
    sel = row8 == b % 8
    q8 = jnp.where(sel, q_ref[pl.ds(r0, 8), :], 0.0)
    k8 = jnp.where(sel, k_ref[pl.ds(r0, 8), :], 0.0)
    v8 = jnp.where(sel, v_ref[pl.ds(r0, 8), :], 0.0)
    q8b = q8.astype(BF16)
    k8b = k8.astype(BF16)
    v8b = v8.astype(BF16)
    k_row = jnp.sum(k8, axis=0, keepdims=True)
    q_row = jnp.sum(q8, axis=0, keepdims=True)
    gate = _pick_row(gate_ref, b)
    m_all = _pick_row(m_in_ref, b)
    lane = lax.broadcasted_iota(jnp.int32, (1, 128), 1)
    m_new_all = jnp.zeros((1, 128), F32)
    h_parts = []
    for h in range(M_HEADS):
        sl = slice(h * M_DH, (h + 1) * M_DH)
        ig = gate[:, h:h + 1]
        lf = _log_sigmoid(gate[:, M_HEADS + h:M_HEADS + h + 1])
        m_prev = m_all[:, h:h + 1]
        m_new = jnp.maximum(lf + m_prev, ig)
        a = jnp.exp(lf + m_prev - m_new)
        ws = jnp.exp(ig - m_new)
        c_new = a * c_in[0, h] + ws * _tn(k8b[:, sl], v8b[:, sl])
        c_out[0, h] = c_new
        n_new = a * n_in[0, h:h + 1, :] + ws * k_row[:, sl]
        n_out[0, h:h + 1, :] = n_new
        num = jnp.sum(_mm(q8b[:, sl], c_new.astype(BF16)), axis=0, keepdims=True)
        den = jnp.sum(q_row[:, sl] * n_new, axis=-1, keepdims=True)
        den = jnp.maximum(jnp.abs(den), jnp.exp(-m_new))
        h_parts.append(num / den)
        m_new_all = jnp.where(lane == h, m_new, m_new_all)
    _put_row(h_ref, b, jnp.concatenate(h_parts, axis=1))
    _put_row(m_out_ref, b, m_new_all)

    @pl.when(b == n_b - 1)
    def _():
        hfull = h_ref[...]
        hs = [hfull[:, h * M_DH:(h + 1) * M_DH] for h in range(M_HEADS)]
        y_ref[...] = x_ref[...] + _mlstm_out(hs, xc_ref[...], z_ref[...], normg_ref, skip_ref, wdown_ref)


def _odd_sample(x, g, ml, c0, n0, m0, conv0):
    n = x.shape[0]
    m_pad = jnp.pad(m0, ((0, 0), (0, 128 - M_HEADS)))
    full = lambda shape: pl.BlockSpec(shape, lambda b: (0,) * len(shape))
    outs = pl.pallas_call(
        _odd_sample_kernel,
        grid=(n,),
        in_specs=[_const_spec((n, D_MODEL)),
                  _const_spec((1, D_MODEL)),
                  _const_spec((D_MODEL, 2 * M_INNER)),
                  _const_spec((M_CONV, M_INNER)),
                  _const_spec((1, M_INNER)),
                  _const_spec((n, M_INNER)), _const_spec((n, M_INNER)), _const_spec((n, M_INNER)),
                  _const_spec((M_INNER // BD, BD, BD)),
                  _const_spec((M_INNER // BD, BD, BD)),
                  _const_spec((M_INNER // BD, BD, BD)),
                  _const_spec((3, M_INNER, 128)),
                  _const_spec((1, 128)),
                  _const_spec((1, M_INNER)),
                  _const_spec((1, M_INNER)),
                  _const_spec((M_INNER, D_MODEL)),
                  _const_spec((n, 128)),
                  pl.BlockSpec((1, M_HEADS, M_DH, M_DH), lambda b: (b, 0, 0, 0)),
                  pl.BlockSpec((1, M_HEADS, M_DH), lambda b: (b, 0, 0))],
        out_specs=[full((n, D_MODEL)),
                   pl.BlockSpec((1, M_HEADS, M_DH, M_DH), lambda b: (b, 0, 0, 0)),
                   pl.BlockSpec((1, M_HEADS, M_DH), lambda b: (b, 0, 0)),
                   full((n, 128)),
                   full((n, M_INNER))],
        out_shape=[jax.ShapeDtypeStruct((n, D_MODEL), F32),
                   jax.ShapeDtypeStruct(c0.shape, F32),
                   jax.ShapeDtypeStruct(n0.shape, F32),
                   jax.ShapeDtypeStruct((n, 128), F32),
                   jax.ShapeDtypeStruct((n, M_INNER), F32)],
        scratch_shapes=[pltpu.VMEM((n, M_INNER), F32), pltpu.VMEM((n, M_INNER), F32), pltpu.VMEM((n, M_INNER), F32),
                        pltpu.VMEM((n, 128), F32), pltpu.VMEM((n, M_INNER), F32), pltpu.VMEM((n, M_INNER), F32),
                        pltpu.VMEM((n, M_INNER), F32)],
        compiler_params=_params(("arbitrary",)),
        name="odd_sample",
    )(x, g.reshape(1, D_MODEL), ml['win'], ml['convw'], ml['convb'],
      conv0[:, 0], conv0[:, 1], conv0[:, 2],
      ml['wq'], ml['wk'], ml['wv'], ml['wgate'], ml['bgate'], ml['normg'], ml['skip'], ml['wdown'],
      m_pad, c0, n0)
    y, c, nn, m_new, xm = outs
    conv_new = jnp.stack([conv0[:, 1], conv0[:, 2], xm], axis=1)
    return y, c, nn, m_new[:, :M_HEADS], conv_new


TM_FFN = 512
TT_EVEN = 256
TT_ODD = 256


def kernel(x_prompt, x_sample, state_ret, state_hgrn, state_mlstm_C, state_mlstm_n, state_mlstm_m, state_mlstm_conv,
           norm_g, final_norm_g, ffn_w_gate, ffn_w_up, ffn_w_down, ev_w_in, ev_w_out, ret_norm_g, hg_norm_g,
           hg_lb_logits, ml_w_in, ml_conv_w, ml_conv_b, ml_w_q, ml_w_k, ml_w_v, ml_w_ig, ml_b_ig, ml_w_fg,
           ml_b_fg, ml_norm_g, ml_skip, ml_w_down):
    bp, tp, _ = x_prompt.shape
    ns = x_sample.shape[0]

    wg = ffn_w_gate.astype(BF16)
    wu = ffn_w_up.astype(BF16)
    wd = ffn_w_down.astype(BF16)
    ev_in = ev_w_in[0].astype(BF16)
    ev_out = ev_w_out[0].astype(BF16)
    lb_all = jnp.cumsum(jax.nn.softmax(hg_lb_logits.astype(F32), axis=0), axis=0)
    lb = lb_all[0].reshape(G_HEADS, G_EXP)
    retg = ret_norm_g[0]
    hgg = hg_norm_g[0]
    wgate = _gate_weights(ml_w_ig[0], ml_w_fg[0])
    bgate = jnp.pad(jnp.concatenate([ml_b_ig[0], ml_b_fg[0]]), (0, 128 - 2 * M_HEADS)).reshape(1, 128)
    ml = {
        'win': ml_w_in[0].astype(BF16),
        'convw': ml_conv_w[0],
        'convb': ml_conv_b[0].reshape(1, M_INNER),
        'wq': _block_diag(ml_w_q[0]).astype(BF16),
        'wk': _block_diag(ml_w_k[0]).astype(BF16),
        'wv': _block_diag(ml_w_v[0]).astype(BF16),
        'wgate': wgate.astype(BF16),
        'wgate_t': jnp.swapaxes(wgate[:, :, :8], 1, 2).astype(BF16),
        'bgate': bgate,
        'bgate_col': bgate[0, :8].reshape(8, 1),
        'normg': ml_norm_g[0].reshape(1, M_INNER),
        'skip': ml_skip[0].reshape(1, M_INNER),
        'wdown': ml_w_down[0].astype(BF16),
    }

    xp = x_prompt.reshape(bp * tp, D_MODEL)
    xp = _ffn(xp, norm_g[0, 0], wg[0, 0], wu[0, 0], wd[0, 0], TM_FFN)
    xp, ret_p, hg_p = _even_prompt(xp.reshape(bp, tp, D_MODEL), norm_g[0, 1], ev_in, ev_out, retg, hgg, lb, TT_EVEN)
    xp = _ffn(xp.reshape(bp * tp, D_MODEL), norm_g[0, 2], wg[0, 1], wu[0, 1], wd[0, 1], TM_FFN)
    xp = _ffn(xp, norm_g[1, 0], wg[1, 0], wu[1, 0], wd[1, 0], TM_FFN)
    xp, c_p, n_p, m_p, conv_p = _odd_prompt(xp.reshape(bp, tp, D_MODEL), norm_g[1, 1], ml, TT_ODD)
    y_p = _ffn(xp.reshape(bp * tp, D_MODEL), norm_g[1, 2], wg[1, 1], wu[1, 1], wd[1, 1], TM_FFN, gfin=final_norm_g)

    xs = x_sample.reshape(ns, D_MODEL)
    xs = _ffn(xs, norm_g[0, 0], wg[0, 0], wu[0, 0], wd[0, 0], ns)
    xs, ret_s, hg_s = _even_sample(xs, norm_g[0, 1], ev_in, ev_out, retg, hgg, lb,
                                   state_ret[:, 0], state_hgrn[:, 0])
    xs = _ffn(xs, norm_g[0, 2], wg[0, 1], wu[0, 1], wd[0, 1], ns)
    xs = _ffn(xs, norm_g[1, 0], wg[1, 0], wu[1, 0], wd[1, 0], ns)
    xs, c_s, n_s, m_s, conv_s = _odd_sample(xs, norm_g[1, 1], ml, state_mlstm_C[:, 0], state_mlstm_n[:, 0],
                                            state_mlstm_m[:, 0], state_mlstm_conv[:, 0])
    y_s = _ffn(xs, norm_g[1, 2], wg[1, 1], wu[1, 1], wd[1, 1], ns, gfin=final_norm_g)

    return (y_p.reshape(bp, tp, D_MODEL), y_s.reshape(ns, 1, D_MODEL),
            ret_p[:, None], hg_p[:, None], c_p[:, None], n_p[:, None], m_p[:, None, :M_HEADS, 0], conv_p[:, None],
            ret_s[:, None], hg_s[:, None], c_s[:, None], n_s[:, None], m_s[:, None], conv_s[:, None])
```

```python
import functools
import math

import jax
import jax.numpy as jnp
from jax import lax
from jax.experimental import pallas as pl
from jax.experimental.pallas import tpu as pltpu

D_MODEL = 1024
PAST_LEN = 16384
R_HEADS = 4
R_DK = 128
R_DV = 128
G_HEADS = 4
G_EXP = 128
G_DV = 128
M_INNER = 2 * D_MODEL
M_HEADS = 4
M_DH = M_INNER // M_HEADS
M_CONV = 4
QKV_BLOCK = 4
D_FF = 2816
EPS = 1e-6
ROPE_BASE = 10000.0
EVEN_IN = 4096
EVEN_OUT = 1024

F32 = jnp.float32
BF16 = jnp.bfloat16

VMEM_LIMIT_BYTES = 56 * 1024 * 1024

HG_CHUNK = 64
HG_SUB = 16
HG_SAFE_LOG_DECAY = -60.0
FFN_SPLIT = 2
BD = 256
SUBLANES = 8


def _nt(a, b):
    return lax.dot_general(a, b, (((1,), (1,)), ((), ())), preferred_element_type=F32)


def _tn(a, b):
    return lax.dot_general(a, b, (((0,), (0,)), ((), ())), preferred_element_type=F32)


def _mm(a, b):
    return jnp.dot(a, b, preferred_element_type=F32)


def _sigmoid(x):
    return 1.0 / (1.0 + jnp.exp(-x))


def _silu(x):
    return x * _sigmoid(x)


def _log_sigmoid(x):
    return jnp.minimum(x, 0.0) - jnp.log(1.0 + jnp.exp(-jnp.abs(x)))


def _rms(x, g):
    return x * lax.rsqrt(jnp.mean(x * x, axis=-1, keepdims=True) + EPS) * g


def _head_norm(x, g, center):
    if center:
        x = x - jnp.mean(x, axis=-1, keepdims=True)
    return x * lax.rsqrt(jnp.mean(x * x, axis=-1, keepdims=True) + EPS) * g


def _rotary(x, cos, sin_signed):
    return x * cos + pltpu.roll(x, 64, 1) * sin_signed


def _const_spec(shape):
    n = len(shape)
    return pl.BlockSpec(shape, lambda *_: (0,) * n, pipeline_mode=pl.Buffered(1))


def _params(sem):
    return pltpu.CompilerParams(dimension_semantics=sem, vmem_limit_bytes=VMEM_LIMIT_BYTES)


def _ffn_rows(x, g, wg_ref, wu_ref, wd_ref, gfin):
    h = _rms(x, g).astype(BF16)
    fc = D_FF // FFN_SPLIT
    y = jnp.zeros_like(x)
    for j in range(FFN_SPLIT):
        gt = _mm(h, wg_ref[:, j * fc:(j + 1) * fc])
        ut = _mm(h, wu_ref[:, j * fc:(j + 1) * fc])
        a = (_silu(gt) * ut).astype(BF16)
        y = y + _mm(a, wd_ref[j * fc:(j + 1) * fc, :])
    out = x + 0.5 * y
    if gfin is not None:
        out = _rms(out, gfin)
    return out


def _ffn_kernel(xp_ref, xs_ref, g_ref, gfin_ref, wg_ref, wu_ref, wd_ref, op_ref, os_ref, *, gi, final):
    g = g_ref[gi:gi + 1, :]
    gfin = gfin_ref[...] if final else None
    op_ref[...] = _ffn_rows(xp_ref[...], g, wg_ref, wu_ref, wd_ref, gfin)

    @pl.when(pl.program_id(0) == pl.num_programs(0) - 1)
    def _():
        os_ref[...] = _ffn_rows(xs_ref[...], g, wg_ref, wu_ref, wd_ref, gfin)


def _ffn(xp, xs, g_all, gfin, wg, wu, wd, layer, idx, tm, final=False):
    n = xp.shape[0]
    ns = xs.shape[0]
    w_in_spec = pl.BlockSpec((None, None, D_MODEL, D_FF), lambda i: (layer, idx, 0, 0), pipeline_mode=pl.Buffered(1))
    w_out_spec = pl.BlockSpec((None, None, D_FF, D_MODEL), lambda i: (layer, idx, 0, 0), pipeline_mode=pl.Buffered(1))
    return pl.pallas_call(
        functools.partial(_ffn_kernel, gi=3 * layer + 2 * idx, final=final),
        grid=(n // tm,),
        in_specs=[pl.BlockSpec((tm, D_MODEL), lambda i: (i, 0)),
                  _const_spec((ns, D_MODEL)),
                  _const_spec(g_all.shape),
                  _const_spec((1, D_MODEL)),
                  w_in_spec, w_in_spec, w_out_spec],
        out_specs=[pl.BlockSpec((tm, D_MODEL), lambda i: (i, 0)),
                   pl.BlockSpec((ns, D_MODEL), lambda i: (0, 0))],
        out_shape=[jax.ShapeDtypeStruct((n, D_MODEL), F32), jax.ShapeDtypeStruct((ns, D_MODEL), F32)],
        compiler_params=_params(("arbitrary",)),
        name="ffn_final" if final else "ffn",
    )(xp, xs, g_all, gfin, wg, wu, wd)


def _ret_log_gamma(h):
    return math.log(1.0 - 2.0 ** (-5.0 - h))


def _hgrn_gates(gq, gf, lb):
    f = lb + (1.0 - lb) * _sigmoid(gf)
    kk = (1.0 - lb) * _sigmoid(-gf)
    qq = _silu(gq)
    return qq, kk, f


def _split3(x):
    hi = x.astype(BF16)
    r1 = x - hi.astype(F32)
    mid = r1.astype(BF16)
    lo = (r1 - mid.astype(F32)).astype(BF16)
    return hi, mid, lo


def _shift_rows(bases, d):
    base = bases[d % SUBLANES]
    full = (d // SUBLANES) * SUBLANES
    return pltpu.roll(base, full, 0) if full else base


def _hgrn_tile_factorised(qq_all, kk_all, f_all, vv_all, b_all, st_all):
    del f_all
    tt = qq_all.shape[0]
    hd = G_EXP
    qx_all = qq_all * jnp.exp(b_all)
    kx_all = kk_all * jnp.exp(-b_all)
    ti = lax.broadcasted_iota(jnp.int32, (HG_CHUNK, HG_CHUNK), 0)
    si = lax.broadcasted_iota(jnp.int32, (HG_CHUNK, HG_CHUNK), 1)
    causal = si <= ti
    outs, states = [], []
    for h in range(G_HEADS):
        hs = slice(h * hd, (h + 1) * hd)
        st = st_all[h]
        o_chunks = []
        for c in range(tt // HG_CHUNK):
            rs = slice(c * HG_CHUNK, (c + 1) * HG_CHUNK)
            qb = qx_all[rs, hs].astype(BF16)
            kx = kx_all[rs, hs]
            vcb = vv_all[rs, hs].astype(BF16)
            a = jnp.where(causal, _nt(qb, kx.astype(BF16)), 0.0)
            o_chunks.append(_mm(a.astype(BF16), vcb) + _nt(qb, st.astype(BF16)))
            etot = jnp.exp(b_all[(c + 1) * HG_CHUNK - 1:(c + 1) * HG_CHUNK, hs])
            st = st * etot + _tn(vcb, (kx * etot).astype(BF16))
        outs.append(jnp.concatenate(o_chunks, axis=0))
        states.append(st)
    return jnp.concatenate(outs, axis=1), jnp.stack(states)


def _hgrn_tile_guarded(qq_all, kk_all, f_all, vv_all, b_all, st_all):
    tt = qq_all.shape[0]
    hd = G_EXP
    row = lax.broadcasted_iota(jnp.int32, (tt, hd), 0)
    rsub = row % HG_SUB
    sub = (lax.broadcasted_iota(jnp.int32, (HG_CHUNK, hd), 0)) // HG_SUB
    n_sub = HG_CHUNK // HG_SUB
    outs, states = [], []
    for h in range(G_HEADS):
        hs = slice(h * hd, (h + 1) * hd)
        qq = qq_all[:, hs]
        kk = kk_all[:, hs]
        ff = f_all[:, hs]
        bc_all = b_all[:, hs]
        vv = vv_all[:, hs]
        f_sh = [ff] + [pltpu.roll(ff, r, 0) for r in range(1, SUBLANES)]
        k_sh = [kk] + [pltpu.roll(kk, r, 0) for r in range(1, SUBLANES)]
        v_sh = [vv] + [pltpu.roll(vv, r, 0) for r in range(1, SUBLANES)]
        o_band = jnp.sum(qq * kk, axis=-1, keepdims=True) * vv
        decay = None
        for d in range(1, HG_SUB):
            fd = _shift_rows(f_sh, d - 1)
            decay = jnp.where(rsub >= d, fd if decay is None else decay * fd, 0.0)
            term = qq * _shift_rows(k_sh, d) * decay
            o_band = o_band + jnp.sum(term, axis=-1, keepdims=True) * _shift_rows(v_sh, d)
        st = st_all[h]
        o_chunks = []
        for c in range(tt // HG_CHUNK):
            r0 = c * HG_CHUNK
            bc = bc_all[r0:r0 + HG_CHUNK]
            qc = qq[r0:r0 + HG_CHUNK]
            kc = kk[r0:r0 + HG_CHUNK]
            vcb = vv[r0:r0 + HG_CHUNK].astype(BF16)
            refs = [bc[i * HG_SUB - 1:i * HG_SUB] for i in range(1, n_sub)]
            refrow = refs[-1]
            for i in range(n_sub - 2, 0, -1):
                refrow = jnp.where(sub == i, refs[i - 1], refrow)
            qp = qc * jnp.exp(bc - refrow)
            lhs = jnp.concatenate([jnp.where(sub == i, qp, 0.0) for i in range(1, n_sub)], axis=1)
            kcat = jnp.concatenate([jnp.where(sub < i, kc * jnp.exp(refs[i - 1] - bc), 0.0)
                                    for i in range(1, n_sub)], axis=1)
            a = _nt(lhs.astype(BF16), kcat.astype(BF16))
            qb = (qc * jnp.exp(bc)).astype(BF16)
            o_chunks.append(_mm(a.astype(BF16), vcb) + _nt(qb, st.astype(BF16)))
            btot = bc[HG_CHUNK - 1:HG_CHUNK]
            ke = (kc * jnp.exp(btot - bc)).astype(BF16)
            st = st * jnp.exp(btot) + _tn(vcb, ke)
        outs.append(o_band + jnp.concatenate(o_chunks, axis=0))
        states.append(st)
    return jnp.concatenate(outs, axis=1), jnp.stack(states)


def _even_prompt_kernel(x_ref, g_ref, win_ref, wout_ref, cos_ref, sin_ref, retg_ref, hgg_ref, lb_ref,
                        y_ref, sret_ref, shg_ref, dmat_ref, tri_ref, st_ref, *, tt, gi):
    b_id = pl.program_id(0)
    t_id = pl.program_id(1)
    n_t = pl.num_programs(1)
    hd = 128

    @pl.when(jnp.logical_and(b_id == 0, t_id == 0))
    def _():
        ti = lax.broadcasted_iota(jnp.int32, (tt, tt), 0)
        si = lax.broadcasted_iota(jnp.int32, (tt, tt), 1)
        diff = (ti - si).astype(F32)
        for h in range(R_HEADS):
            dmat_ref[h] = jnp.where(diff >= 0.0, jnp.exp(_ret_log_gamma(h) * jnp.maximum(diff, 0.0)), 0.0)
        same_chunk = (ti // HG_CHUNK) == (si // HG_CHUNK)
        tri_ref[...] = jnp.where(jnp.logical_and(same_chunk, si <= ti), 1.0, 0.0).astype(BF16)

    @pl.when(t_id == 0)
    def _():
        sret_ref[...] = jnp.zeros_like(sret_ref)
        st_ref[...] = jnp.zeros_like(st_ref)

    x = x_ref[0]
    hn = _rms(x, g_ref[gi:gi + 1, :]).astype(BF16)
    p = _mm(hn, win_ref[...])
    cos = cos_ref[...]
    sin = sin_ref[...]

    row = lax.broadcasted_iota(jnp.int32, (tt, hd), 0)
    rowf = row.astype(F32)
    outs = []

    for h in range(R_HEADS):
        lg = _ret_log_gamma(h)
        q = _rotary(p[:, h * hd:(h + 1) * hd], cos, sin)
        k = _rotary(p[:, 512 + h * hd:512 + (h + 1) * hd], cos, sin) * (R_DK ** -0.5)
        v = p[:, 1024 + h * hd:1024 + (h + 1) * hd]
        rg = p[:, 1536 + h * hd:1536 + (h + 1) * hd]
        s0 = sret_ref[0, h]
        qb = q.astype(BF16)
        vb = v.astype(BF16)
        inter = _mm(qb, s0.astype(BF16)) * jnp.exp(lg * (rowf + 1.0))
        scores = _nt(qb, k.astype(BF16)) * dmat_ref[h]
        intra = _mm(scores.astype(BF16), vb)
        kd = (k * jnp.exp(lg * (tt - 1.0 - rowf))).astype(BF16)
        sret_ref[0, h] = math.exp(lg * tt) * s0 + _tn(kd, vb)
        o = _head_norm(inter + intra, retg_ref[h:h + 1, :], False) * _silu(rg)
        outs.append(o)

    qq_all, kk_all, f_all = _hgrn_gates(p[:, 2048:2560], p[:, 2560:3072], lb_ref[...])
    vv_all = p[:, 3072:3584]
    cs = _mm(tri_ref[...], jnp.concatenate(_split3(jnp.log(f_all)), axis=1))
    b_all = cs[:, :512] + cs[:, 512:1024] + cs[:, 1024:]
    mild = jnp.min(b_all) >= HG_SAFE_LOG_DECAY
    o_hg, st_new = lax.cond(mild, _hgrn_tile_factorised, _hgrn_tile_guarded,
                            qq_all, kk_all, f_all, vv_all, b_all, st_ref[...])
    st_ref[...] = st_new
    for h in range(G_HEADS):
        gg = p[:, 3584 + h * hd:3584 + (h + 1) * hd]
        outs.append(_head_norm(o_hg[:, h * hd:(h + 1) * hd], hgg_ref[h:h + 1, :], False) * _sigmoid(gg))

    ycat = jnp.concatenate(outs, axis=1).astype(BF16)
    y_ref[0] = x + _mm(ycat, wout_ref[...])

    @pl.when(t_id == n_t - 1)
    def _():
        for h in range(G_HEADS):
            shg_ref[0, h] = st_ref[h].T


def _rope_tables(pos):
    half = R_DK // 2
    inv = ROPE_BASE ** (-jnp.arange(half, dtype=F32) / half)
    ang = pos.astype(F32)[:, None] * inv[None, :]
    cos = jnp.cos(ang)
    sin = jnp.sin(ang)
    return jnp.concatenate([cos, cos], axis=-1), jnp.concatenate([-sin, sin], axis=-1)


def _even_prompt(x, g_all, gi, win, wout, retg, hgg, lb, tt):
    bsz, seq, _ = x.shape
    cos, sin = _rope_tables(jnp.arange(seq, dtype=jnp.int32))
    state_spec = pl.BlockSpec((1, 4, 128, 128), lambda b, t: (b, 0, 0, 0))
    return pl.pallas_call(
        functools.partial(_even_prompt_kernel, tt=tt, gi=gi),
        grid=(bsz, seq // tt),
        in_specs=[pl.BlockSpec((1, tt, D_MODEL), lambda b, t: (b, t, 0)),
                  _const_spec(g_all.shape),
                  _const_spec((D_MODEL, EVEN_IN)),
                  _const_spec((EVEN_OUT, D_MODEL)),
                  pl.BlockSpec((tt, 128), lambda b, t: (t, 0)),
                  pl.BlockSpec((tt, 128), lambda b, t: (t, 0)),
                  _const_spec((4, 128)), _const_spec((4, 128)), _const_spec((1, 512))],
        out_specs=[pl.BlockSpec((1, tt, D_MODEL), lambda b, t: (b, t, 0)), state_spec, state_spec],
        out_shape=[jax.ShapeDtypeStruct(x.shape, F32),
                   jax.ShapeDtypeStruct((bsz, 4, 128, 128), F32),
                   jax.ShapeDtypeStruct((bsz, 4, 128, 128), F32)],
        scratch_shapes=[pltpu.VMEM((4, tt, tt), F32), pltpu.VMEM((tt, tt), BF16), pltpu.VMEM((4, 128, 128), F32)],
        compiler_params=_params(("arbitrary", "arbitrary")),
        name="even_prompt",
    )(x, g_all, win, wout, cos, sin, retg, hgg, lb)


SB = 8


def _even_sample_kernel(x_ref, g_ref, win_ref, wout_ref, cos_ref, sin_ref, retg_ref, hgg_ref, lb_ref,
                        sret_in, shg_in, y_ref, sret_out, shg_out, p_ref, o_ref, *, gi):
    i = pl.program_id(0)
    n_i = pl.num_programs(0)
    hd = 128

    @pl.when(i == 0)
    def _():
        hn = _rms(x_ref[...], g_ref[gi:gi + 1, :]).astype(BF16)
        p_ref[...] = _mm(hn, win_ref[...])

    r0 = pl.multiple_of(i * SB, SB)
    p = p_ref[pl.ds(r0, SB), :]
    cos = cos_ref[...]
    sin = sin_ref[...]
    row = lax.broadcasted_iota(jnp.int32, (SB, hd), 0)
    outs = []
    for h in range(R_HEADS):
        gamma = math.exp(_ret_log_gamma(h))
        q = _rotary(p[:, h * hd:(h + 1) * hd], cos, sin).astype(BF16)
        k = _rotary(p[:, 512 + h * hd:512 + (h + 1) * hd], cos, sin) * (R_DK ** -0.5)
        vb = p[:, 1024 + h * hd:1024 + (h + 1) * hd].astype(BF16)
        rg = p[:, 1536 + h * hd:1536 + (h + 1) * hd]
        o = jnp.zeros((SB, hd), F32)
        for j in range(SB):
            kj = jnp.where(row == j, k, 0.0).astype(BF16)
            s_new = gamma * sret_in[j, h] + _tn(kj, vb)
            sret_out[j, h] = s_new
            o = jnp.where(row == j, _mm(q, s_new.astype(BF16)), o)
        outs.append(_head_norm(o, retg_ref[h:h + 1, :], False) * _silu(rg))
    qq_all, kk_all, f_all = _hgrn_gates(p[:, 2048:2560], p[:, 2560:3072], lb_ref[...])
    for h in range(G_HEADS):
        hs = slice(h * hd, (h + 1) * hd)
        kk = kk_all[:, hs]
        vb = p[:, 3072 + h * hd:3072 + (h + 1) * hd].astype(BF16)
        gg = p[:, 3584 + h * hd:3584 + (h + 1) * hd]
        f_cols = jnp.concatenate([f_all[:, hs], jnp.zeros((hd - SB, hd), F32)], axis=0).T
        qb = qq_all[:, hs].astype(BF16)
        o = jnp.zeros((SB, hd), F32)
        for j in range(SB):
            kj = jnp.where(row == j, kk, 0.0).astype(BF16)
            s_new = f_cols[:, j:j + 1] * shg_in[j, h] + _tn(kj, vb)
            shg_out[j, h] = s_new
            o = jnp.where(row == j, _mm(qb, s_new.astype(BF16)), o)
        outs.append(_head_norm(o, hgg_ref[h:h + 1, :], False) * _sigmoid(gg))
    o_ref[pl.ds(r0, SB), :] = jnp.concatenate(outs, axis=1)

    @pl.when(i == n_i - 1)
    def _():
        y_ref[...] = x_ref[...] + _mm(o_ref[...].astype(BF16), wout_ref[...])


def _even_sample(x, g_all, gi, win, wout, retg, hgg, lb, sret, shg):
    n = x.shape[0]
    cos, sin = _rope_tables(jnp.full((1,), PAST_LEN, dtype=jnp.int32))
    state_spec = pl.BlockSpec((SB, 4, 128, 128), lambda i: (i, 0, 0, 0))
    return pl.pallas_call(
        functools.partial(_even_sample_kernel, gi=gi),
        grid=(n // SB,),
        in_specs=[_const_spec((n, D_MODEL)),
                  _const_spec(g_all.shape),
                  _const_spec((D_MODEL, EVEN_IN)),
                  _const_spec((EVEN_OUT, D_MODEL)),
                  _const_spec((1, 128)), _const_spec((1, 128)),
                  _const_spec((4, 128)), _const_spec((4, 128)), _const_spec((1, 512)),
                  state_spec, state_spec],
        out_specs=[pl.BlockSpec((n, D_MODEL), lambda i: (0, 0)), state_spec, state_spec],
        out_shape=[jax.ShapeDtypeStruct((n, D_MODEL), F32),
                   jax.ShapeDtypeStruct(sret.shape, F32),
                   jax.ShapeDtypeStruct(shg.shape, F32)],
        scratch_shapes=[pltpu.VMEM((n, EVEN_IN), F32), pltpu.VMEM((n, EVEN_OUT), F32)],
        compiler_params=_params(("arbitrary",)),
        name="even_sample",
    )(x, g_all, win, wout, cos, sin, retg, hgg, lb, sret, shg)


def _block_diag(w):
    per = BD // QKV_BLOCK
    wg = w.reshape(M_INNER // BD, per, QKV_BLOCK, QKV_BLOCK)
    eye = jnp.eye(per, dtype=w.dtype)
    return jnp.einsum('gaio,ab->gaibo', wg, eye).reshape(M_INNER // BD, BD, BD)


def _headwise(xb, w_ref, g0=0):
    return jnp.concatenate([_mm(xb[:, g * BD:(g + 1) * BD], w_ref[g0 + g]) for g in range(xb.shape[1] // BD)], axis=1)


def _gate_weights(w_ig, w_fg):
    w = jnp.concatenate([w_ig, w_fg], axis=1)
    w = jnp.pad(w, ((0, 0), (0, 128 - 2 * M_HEADS)))
    return w.reshape(3, M_INNER, 128)


def _mlstm_out(hs, xc, z, normg_ref, skip_ref, wdown_ref):
    hc = jnp.concatenate([_head_norm(hs[h], normg_ref[:, h * M_DH:(h + 1) * M_DH], True) for h in range(M_HEADS)],
                         axis=1)
    hc = hc + skip_ref[...] * xc
    return _mm((hc * _silu(z)).astype(BF16), wdown_ref[...])


def _odd_prompt_kernel(x_ref, g_ref, win_ref, convw_ref, convb_ref, wq_ref, wk_ref, wv_ref,
                       wgate_ref, wgate_t_ref, bgate_ref, bgate_t_ref, normg_ref, skip_ref, wdown_ref,
                       y_ref, c_ref, n_ref, m_out_ref, conv_out_ref,
                       carry_ref, m_ref, xc_ref, q_ref, k_ref, v_ref, *, tt, gi):
    t_id = pl.program_id(1)
    k_scale = M_DH ** -0.5

    @pl.when(t_id == 0)
    def _():
        c_ref[...] = jnp.zeros_like(c_ref)
        n_ref[...] = jnp.zeros_like(n_ref)
        m_ref[...] = jnp.zeros_like(m_ref)
        carry_ref[...] = jnp.zeros_like(carry_ref)

    x = x_ref[0]
    hn = _rms(x, g_ref[gi:gi + 1, :]).astype(BF16)

    gates = jnp.broadcast_to(bgate_ref[...], (tt, 128))
    gates_t = bgate_t_ref[...]
    row8 = lax.broadcasted_iota(jnp.int32, (SUBLANES, M_DH), 0)
    tiles = M_DH // BD
    for h in range(M_HEADS):
        sl = slice(h * M_DH, (h + 1) * M_DH)
        xm = _mm(hn, win_ref[:, sl])
        carry = carry_ref[:, sl]
        conv = convb_ref[:, sl] + convw_ref[M_CONV - 1:M_CONV, sl] * xm
        for j in range(1, M_CONV):
            rolled = pltpu.roll(xm, j, 0)
            head = jnp.where(row8 < j, pltpu.roll(carry, j, 0), rolled[:SUBLANES])
            shifted = jnp.concatenate([head, rolled[SUBLANES:]], axis=0)
            conv = conv + convw_ref[M_CONV - 1 - j:M_CONV - j, sl] * shifted
        carry_ref[:, sl] = xm[tt - SUBLANES:, :]
        conv_out_ref[0, :, sl] = xm[tt - (M_CONV - 1):, :]
        xc = _silu(conv)
        xc_ref[:, sl] = xc
        xcb = xc.astype(BF16)
        qb = _headwise(xcb, wq_ref, h * tiles).astype(BF16)
        kb = _headwise(xcb, wk_ref, h * tiles).astype(BF16)
        vb = _headwise(xm.astype(BF16), wv_ref, h * tiles).astype(BF16)
        q_ref[:, sl] = qb
        k_ref[:, sl] = kb
        v_ref[:, sl] = vb
        gates = gates + (_mm(qb, wgate_ref[0, sl, :]) + _mm(kb, wgate_ref[1, sl, :]) + _mm(vb, wgate_ref[2, sl, :]))
        gates_t = gates_t + (_nt(wgate_t_ref[0, :, sl], qb) + _nt(wgate_t_ref[1, :, sl], kb)
                             + _nt(wgate_t_ref[2, :, sl], vb))

    rowg = lax.broadcasted_iota(jnp.int32, (tt, 128), 0)
    lane8 = lax.broadcasted_iota(jnp.int32, (SUBLANES, tt), 1)
    bcol_all = _log_sigmoid(gates)
    brow_all = _log_sigmoid(gates_t)
    d = 1
    while d < tt:
        bcol_all = bcol_all + jnp.where(rowg >= d, pltpu.roll(bcol_all, d, 0), 0.0)
        brow_all = brow_all + jnp.where(lane8 >= d, pltpu.roll(brow_all, d, 1), 0.0)
        d *= 2

    ti = lax.broadcasted_iota(jnp.int32, (tt, tt), 0)
    si = lax.broadcasted_iota(jnp.int32, (tt, tt), 1)
    causal = si <= ti
    y = jnp.zeros((tt, D_MODEL), F32)
    for h in range(M_HEADS):
        sl = slice(h * M_DH, (h + 1) * M_DH)
        qhb = q_ref[:, sl]
        khb = k_ref[:, sl]
        vhb = v_ref[:, sl]
        ig_col = gates[:, h:h + 1]
        b_col = bcol_all[:, M_HEADS + h:M_HEADS + h + 1]
        ig_row = gates_t[h:h + 1, :]
        b_row = brow_all[M_HEADS + h:M_HEADS + h + 1, :]
        m_prev = m_ref[h:h + 1, 0:1]
        c_prev = c_ref[0, h]
        n_prev = n_ref[0, h:h + 1, :]

        dlog = jnp.where(causal, b_col + (ig_row - b_row), -jnp.inf)
        inter_log = b_col + m_prev
        m_row = jnp.maximum(inter_log, jnp.max(dlog, axis=-1, keepdims=True))
        w_inter = jnp.exp(inter_log - m_row)
        qk = _nt(qhb, khb) * (jnp.exp(dlog - m_row) * k_scale)
        num = w_inter * _mm(qhb, c_prev.astype(BF16)) + _mm(qk.astype(BF16), vhb)
        qn = _nt(qhb, jnp.broadcast_to(n_prev, (SUBLANES, M_DH)).astype(BF16))[:, 0:1]
        den = w_inter * qn + jnp.sum(qk, axis=-1, keepdims=True)
        den = jnp.maximum(jnp.abs(den), jnp.exp(-m_row))
        hh = num * (1.0 / den)

        b_end = b_col[tt - 1:tt, :]
        s_log = b_end - b_col + ig_col
        m_new = jnp.maximum(b_end + m_prev, jnp.max(s_log, axis=0, keepdims=True))
        a = jnp.exp(b_end + m_prev - m_new)
        kw = khb.astype(F32) * (jnp.exp(s_log - m_new) * k_scale)
        c_ref[0, h] = a * c_prev + _tn(kw.astype(BF16), vhb)
        n_ref[0, h:h + 1, :] = a * n_prev + jnp.sum(kw, axis=0, keepdims=True)
        m_ref[h:h + 1, :] = jnp.broadcast_to(m_new, (1, 128))

        hc = _head_norm(hh, normg_ref[:, sl], True) + skip_ref[:, sl] * xc_ref[:, sl]
        z = _mm(hn, win_ref[:, M_INNER + h * M_DH:M_INNER + (h + 1) * M_DH])
        y = y + _mm((hc * _silu(z)).astype(BF16), wdown_ref[sl, :])

    y_ref[0] = x + y
    m_out_ref[0] = m_ref[...]


def _odd_prompt(x, g_all, gi, ml, tt):
    bsz, seq, _ = x.shape
    bt = jnp.broadcast_to(ml['bgate_col'], (SUBLANES, tt))
    return pl.pallas_call(
        functools.partial(_odd_prompt_kernel, tt=tt, gi=gi),
        grid=(bsz, seq // tt),
        in_specs=[pl.BlockSpec((1, tt, D_MODEL), lambda b, t: (b, t, 0)),
                  _const_spec(g_all.shape),
                  _const_spec((D_MODEL, 2 * M_INNER)),
                  _const_spec((M_CONV, M_INNER)),
                  _const_spec((1, M_INNER)),
                  _const_spec((M_INNER // BD, BD, BD)),
                  _const_spec((M_INNER // BD, BD, BD)),
                  _const_spec((M_INNER // BD, BD, BD)),
                  _const_spec((3, M_INNER, 128)),
                  _const_spec((3, SUBLANES, M_INNER)),
                  _const_spec((1, 128)),
                  _const_spec((SUBLANES, tt)),
                  _const_spec((1, M_INNER)),
                  _const_spec((1, M_INNER)),
                  _const_spec((M_INNER, D_MODEL))],
        out_specs=[pl.BlockSpec((1, tt, D_MODEL), lambda b, t: (b, t, 0)),
                   pl.BlockSpec((1, M_HEADS, M_DH, M_DH), lambda b, t: (b, 0, 0, 0)),
                   pl.BlockSpec((1, M_HEADS, M_DH), lambda b, t: (b, 0, 0)),
                   pl.BlockSpec((1, SUBLANES, 128), lambda b, t: (b, 0, 0)),
                   pl.BlockSpec((1, M_CONV - 1, M_INNER), lambda b, t: (b, 0, 0))],
        out_shape=[jax.ShapeDtypeStruct(x.shape, F32),
                   jax.ShapeDtypeStruct((bsz, M_HEADS, M_DH, M_DH), F32),
                   jax.ShapeDtypeStruct((bsz, M_HEADS, M_DH), F32),
                   jax.ShapeDtypeStruct((bsz, SUBLANES, 128), F32),
                   jax.ShapeDtypeStruct((bsz, M_CONV - 1, M_INNER), F32)],
        scratch_shapes=[pltpu.VMEM((SUBLANES, M_INNER), F32), pltpu.VMEM((SUBLANES, 128), F32),
                        pltpu.VMEM((tt, M_INNER), F32), pltpu.VMEM((tt, M_INNER), BF16),
                        pltpu.VMEM((tt, M_INNER), BF16), pltpu.VMEM((tt, M_INNER), BF16)],
        compiler_params=_params(("arbitrary", "arbitrary")),
        name="odd_prompt",
    )(x, g_all, ml['win'], ml['convw'], ml['convb'], ml['wq'], ml['wk'], ml['wv'],
      ml['wgate'], ml['wgate_t'], ml['bgate'], bt, ml['normg'], ml['skip'], ml['wdown'])


def _pick_row(ref, b):
    r0 = pl.multiple_of((b // SUBLANES) * SUBLANES, SUBLANES)
    blk = ref[pl.ds(r0, SUBLANES), :]
    row = lax.broadcasted_iota(jnp.int32, blk.shape, 0)
    return jnp.sum(jnp.where(row == b % SUBLANES, blk, 0.0), axis=0, keepdims=True)


def _put_row(ref, b, val):
    r0 = pl.multiple_of((b // SUBLANES) * SUBLANES, SUBLANES)
    blk = ref[pl.ds(r0, SUBLANES), :]
    row = lax.broadcasted_iota(jnp.int32, blk.shape, 0)
    ref[pl.ds(r0, SUBLANES), :] = jnp.where(row == b % SUBLANES, jnp.broadcast_to(val, blk.shape), blk)


def _odd_sample_kernel(x_ref, g_ref, win_ref, convw_ref, convb_ref, cv_ref,
                       wq_ref, wk_ref, wv_ref, wgate_ref, bgate_ref, normg_ref, skip_ref, wdown_ref,
                       m_in_ref, c_in, n_in,
                       y_ref, c_out, n_out, m_out_ref, cv_out_ref,
                       q_ref, k_ref, v_ref, gate_ref, xc_ref, z_ref, h_ref, *, gi):
    b = pl.program_id(0)
    n_b = pl.num_programs(0)

    @pl.when(b == 0)
    def _():
        hn = _rms(x_ref[...], g_ref[gi:gi + 1, :]).astype(BF16)
        p = _mm(hn, win_ref[...])
        xm = p[:, :M_INNER]
        z_ref[...] = p[:, M_INNER:]
        cv0 = cv_ref[:, :M_INNER]
        cv1 = cv_ref[:, M_INNER:2 * M_INNER]
        cv2 = cv_ref[:, 2 * M_INNER:]
        cv_out_ref[:, :M_INNER] = cv1
        cv_out_ref[:, M_INNER:2 * M_INNER] = cv2
        cv_out_ref[:, 2 * M_INNER:] = xm
        conv = (convb_ref[...] + convw_ref[3:4, :] * xm + convw_ref[2:3, :] * cv2
                + convw_ref[1:2, :] * cv1 + convw_ref[0:1, :] * cv0)
        xc = _silu(conv)
        xc_ref[...] = xc
        xcb = xc.astype(BF16)
        q = _headwise(xcb, wq_ref)
        k = _headwise(xcb, wk_ref)
        v = _headwise(xm.astype(BF16), wv_ref)
        q_ref[...] = q
        k_ref[...] = k * (M_DH ** -0.5)
        v_ref[...] = v
        gate_ref[...] = (_mm(q.astype(BF16), wgate_ref[0]) + _mm(k.astype(BF16), wgate_ref[1])
                         + _mm(v.astype(BF16), wgate_ref[2])) + bgate_ref[...]
        m_out_ref[...] = jnp.zeros_like(m_out_ref)
        h_ref[...] = jnp.zeros_like(h_ref)

    r0 = pl.multiple_of((b // SUBLANES) * SUBLANES, SUBLANES)
    row8 = lax.broadcasted_iota(jnp.int32, (SUBLANES, M_INNER), 0)
    sel = row8 == b % SUBLANES
    q8 = jnp.where(sel, q_ref[pl.ds(r0, SUBLANES), :], 0.0)
    k8 = jnp.where(sel, k_ref[pl.ds(r0, SUBLANES), :], 0.0)
    v8 = jnp.where(sel, v_ref[pl.ds(r0, SUBLANES), :], 0.0)
    q8b = q8.astype(BF16)
    k8b = k8.astype(BF16)
    v8b = v8.astype(BF16)
    k_row = jnp.sum(k8, axis=0, keepdims=True)
    q_row = jnp.sum(q8, axis=0, keepdims=True)
    gate = _pick_row(gate_ref, b)
    m_all = _pick_row(m_in_ref, b)
    lane = lax.broadcasted_iota(jnp.int32, (1, 128), 1)
    m_new_all = jnp.zeros((1, 128), F32)
    h_parts = []
    for h in range(M_HEADS):
        sl = slice(h * M_DH, (h + 1) * M_DH)
        ig = gate[:, h:h + 1]
        lf = _log_sigmoid(gate[:, M_HEADS + h:M_HEADS + h + 1])
        m_prev = m_all[:, h:h + 1]
        m_new = jnp.maximum(lf + m_prev, ig)
        a = jnp.exp(lf + m_prev - m_new)
        ws = jnp.exp(ig - m_new)
        c_new = a * c_in[0, h] + ws * _tn(k8b[:, sl], v8b[:, sl])
        c_out[0, h] = c_new
        n_new = a * n_in[0, h:h + 1, :] + ws * k_row[:, sl]
        n_out[0, h:h + 1, :] = n_new
        num = jnp.sum(_mm(q8b[:, sl], c_new.astype(BF16)), axis=0, keepdims=True)
        den = jnp.sum(q_row[:, sl] * n_new, axis=-1, keepdims=True)
        den = jnp.maximum(jnp.abs(den), jnp.exp(-m_new))
        h_parts.append(num / den)
        m_new_all = jnp.where(lane == h, m_new, m_new_all)
    _put_row(h_ref, b, jnp.concatenate(h_parts, axis=1))
    _put_row(m_out_ref, b, m_new_all)

    @pl.when(b == n_b - 1)
    def _():
        hfull = h_ref[...]
        hs = [hfull[:, h * M_DH:(h + 1) * M_DH] for h in range(M_HEADS)]
        y_ref[...] = x_ref[...] + _mlstm_out(hs, xc_ref[...], z_ref[...], normg_ref, skip_ref, wdown_ref)


def _odd_sample(x, g_all, gi, ml, c0, n0, m0, conv0):
    n = x.shape[0]
    m_pad = jnp.pad(m0, ((0, 0), (0, 128 - M_HEADS)))
    cw = (M_CONV - 1) * M_INNER
    full = lambda shape: pl.BlockSpec(shape, lambda b: (0,) * len(shape))
    outs = pl.pallas_call(
        functools.partial(_odd_sample_kernel, gi=gi),
        grid=(n,),
        in_specs=[_const_spec((n, D_MODEL)),
                  _const_spec(g_all.shape),
                  _const_spec((D_MODEL, 2 * M_INNER)),
                  _const_spec((M_CONV, M_INNER)),
                  _const_spec((1, M_INNER)),
                  _const_spec((n, cw)),
                  _const_spec((M_INNER // BD, BD, BD)),
                  _const_spec((M_INNER // BD, BD, BD)),
                  _const_spec((M_INNER // BD, BD, BD)),
                  _const_spec((3, M_INNER, 128)),
                  _const_spec((1, 128)),
                  _const_spec((1, M_INNER)),
                  _const_spec((1, M_INNER)),
                  _const_spec((M_INNER, D_MODEL)),
                  _const_spec((n, 128)),
                  pl.BlockSpec((1, M_HEADS, M_DH, M_DH), lambda b: (b, 0, 0, 0)),
                  pl.BlockSpec((1, M_HEADS, M_DH), lambda b: (b, 0, 0))],
        out_specs=[full((n, D_MODEL)),
                   pl.BlockSpec((1, M_HEADS, M_DH, M_DH), lambda b: (b, 0, 0, 0)),
                   pl.BlockSpec((1, M_HEADS, M_DH), lambda b: (b, 0, 0)),
                   full((n, 128)),
                   full((n, cw))],
        out_shape=[jax.ShapeDtypeStruct((n, D_MODEL), F32),
                   jax.ShapeDtypeStruct(c0.shape, F32),
                   jax.ShapeDtypeStruct(n0.shape, F32),
                   jax.ShapeDtypeStruct((n, 128), F32),
                   jax.ShapeDtypeStruct((n, cw), F32)],
        scratch_shapes=[pltpu.VMEM((n, M_INNER), F32), pltpu.VMEM((n, M_INNER), F32), pltpu.VMEM((n, M_INNER), F32),
                        pltpu.VMEM((n, 128), F32), pltpu.VMEM((n, M_INNER), F32), pltpu.VMEM((n, M_INNER), F32),
                        pltpu.VMEM((n, M_INNER), F32)],
        compiler_params=_params(("arbitrary",)),
        name="odd_sample",
    )(x, g_all, ml['win'], ml['convw'], ml['convb'], conv0.reshape(n, cw),
      ml['wq'], ml['wk'], ml['wv'], ml['wgate'], ml['bgate'], ml['normg'], ml['skip'], ml['wdown'],
      m_pad, c0, n0)
    y, c, nn, m_new, conv_new = outs
    return y, c, nn, m_new[:, :M_HEADS], conv_new.reshape(n, M_CONV - 1, M_INNER)


TM_FFN = 512
TT_EVEN = 256
TT_ODD = 512


def kernel(x_prompt, x_sample, state_ret, state_hgrn, state_mlstm_C, state_mlstm_n, state_mlstm_m, state_mlstm_conv,
           norm_g, final_norm_g, ffn_w_gate, ffn_w_up, ffn_w_down, ev_w_in, ev_w_out, ret_norm_g, hg_norm_g,
           hg_lb_logits, ml_w_in, ml_conv_w, ml_conv_b, ml_w_q, ml_w_k, ml_w_v, ml_w_ig, ml_b_ig, ml_w_fg,
           ml_b_fg, ml_norm_g, ml_skip, ml_w_down):
    bp, tp, _ = x_prompt.shape
    ns = x_sample.shape[0]

    wg = ffn_w_gate.astype(BF16)
    wu = ffn_w_up.astype(BF16)
    wd = ffn_w_down.astype(BF16)
    ev_in = ev_w_in[0].astype(BF16)
    ev_out = ev_w_out[0].astype(BF16)
    g_all = norm_g.reshape(-1, D_MODEL)
    gfin = final_norm_g.reshape(1, D_MODEL)
    lb_all = jnp.cumsum(jax.nn.softmax(hg_lb_logits.astype(F32), axis=0), axis=0)
    lb = lb_all[0].reshape(1, G_HEADS * G_EXP)
    retg = ret_norm_g[0]
    hgg = hg_norm_g[0]
    wgate = _gate_weights(ml_w_ig[0], ml_w_fg[0])
    bgate = jnp.pad(jnp.concatenate([ml_b_ig[0], ml_b_fg[0]]), (0, 128 - 2 * M_HEADS)).reshape(1, 128)
    ml = {
        'win': ml_w_in[0].astype(BF16),
        'convw': ml_conv_w[0],
        'convb': ml_conv_b[0].reshape(1, M_INNER),
        'wq': _block_diag(ml_w_q[0]).astype(BF16),
        'wk': _block_diag(ml_w_k[0]).astype(BF16),
        'wv': _block_diag(ml_w_v[0]).astype(BF16),
        'wgate': wgate.astype(BF16),
        'wgate_t': jnp.swapaxes(wgate[:, :, :SUBLANES], 1, 2).astype(BF16),
        'bgate': bgate,
        'bgate_col': bgate[0, :SUBLANES].reshape(SUBLANES, 1),
        'normg': ml_norm_g[0].reshape(1, M_INNER),
        'skip': ml_skip[0].reshape(1, M_INNER),
        'wdown': ml_w_down[0].astype(BF16),
    }

    xp = x_prompt.reshape(bp * tp, D_MODEL)
    xs = x_sample.reshape(ns, D_MODEL)

    xp, xs = _ffn(xp, xs, g_all, gfin, wg, wu, wd, 0, 0, TM_FFN)
    xp, ret_p, hg_p = _even_prompt(xp.reshape(bp, tp, D_MODEL), g_all, 1, ev_in, ev_out, retg, hgg, lb, TT_EVEN)
    xs, ret_s, hg_s = _even_sample(xs, g_all, 1, ev_in, ev_out, retg, hgg, lb, state_ret[:, 0], state_hgrn[:, 0])
    xp, xs = _ffn(xp.reshape(bp * tp, D_MODEL), xs, g_all, gfin, wg, wu, wd, 0, 1, TM_FFN)
    xp, xs = _ffn(xp, xs, g_all, gfin, wg, wu, wd, 1, 0, TM_FFN)
    xp, c_p, n_p, m_p, conv_p = _odd_prompt(xp.reshape(bp, tp, D_MODEL), g_all, 4, ml, TT_ODD)
    xs, c_s, n_s, m_s, conv_s = _odd_sample(xs, g_all, 4, ml, state_mlstm_C[:, 0], state_mlstm_n[:, 0],
                                            state_mlstm_m[:, 0], state_mlstm_conv[:, 0])
    y_p, y_s = _ffn(xp.reshape(bp * tp, D_MODEL), xs, g_all, gfin, wg, wu, wd, 1, 1, TM_FFN, final=True)

    return (y_p.reshape(bp, tp, D_MODEL), y_s.reshape(ns, 1, D_MODEL),
            ret_p[:, None], hg_p[:, None], c_p[:, None], n_p[:, None], m_p[:, None, :M_HEADS, 0], conv_p[:, None],
            ret_s[:, None], hg_s[:, None], c_s[:, None], n_s[:, None], m_s[:, None], conv_s[:, None])
```

```python
import functools
import math

import jax
import jax.numpy as jnp
from jax import lax
from jax.experimental import pallas as pl
from jax.experimental.pallas import tpu as pltpu

D_MODEL = 1024
PAST_LEN = 16384
R_HEADS = 4
R_DK = 128
R_DV = 128
G_HEADS = 4
G_EXP = 128
G_DV = 128
M_INNER = 2 * D_MODEL
M_HEADS = 4
M_DH = M_INNER // M_HEADS
M_CONV = 4
QKV_BLOCK = 4
D_FF = 2816
EPS = 1e-6
ROPE_BASE = 10000.0
EVEN_IN = 4096
EVEN_OUT = 1024

F32 = jnp.float32
BF16 = jnp.bfloat16

VMEM_LIMIT_BYTES = 56 * 1024 * 1024

HG_CHUNK = 64
HG_SUB = 16
HG_SAFE_LOG_DECAY = -60.0
MXU_TILE = 256
FFN_BOUNDS = (0, 6 * MXU_TILE, D_FF)
BD = MXU_TILE
SUBLANES = 8


def _nt(a, b):
    return lax.dot_general(a, b, (((1,), (1,)), ((), ())), preferred_element_type=F32)


def _tn(a, b):
    return lax.dot_general(a, b, (((0,), (0,)), ((), ())), preferred_element_type=F32)


def _mm(a, b):
    return jnp.dot(a, b, preferred_element_type=F32)


def _sigmoid(x):
    return 1.0 / (1.0 + jnp.exp(-x))


def _silu(x):
    return x * _sigmoid(x)


def _log_sigmoid(x):
    return jnp.minimum(x, 0.0) - jnp.log(1.0 + jnp.exp(-jnp.abs(x)))


def _rms(x, g):
    return x * lax.rsqrt(jnp.mean(x * x, axis=-1, keepdims=True) + EPS) * g


def _head_norm(x, g, center):
    if center:
        x = x - jnp.mean(x, axis=-1, keepdims=True)
    return x * lax.rsqrt(jnp.mean(x * x, axis=-1, keepdims=True) + EPS) * g


def _rotary(x, cos, sin_signed):
    return x * cos + pltpu.roll(x, 64, 1) * sin_signed


def _const_spec(shape):
    n = len(shape)
    return pl.BlockSpec(shape, lambda *_: (0,) * n, pipeline_mode=pl.Buffered(1))


def _params(sem):
    return pltpu.CompilerParams(dimension_semantics=sem, vmem_limit_bytes=VMEM_LIMIT_BYTES)


def _ffn_rows(x, g, wg_ref, wu_ref, wd_ref, gfin):
    h = _rms(x, g).astype(BF16)
    y = jnp.zeros_like(x)
    for lo, hi in zip(FFN_BOUNDS[:-1], FFN_BOUNDS[1:]):
        gt = _mm(h, wg_ref[:, lo:hi])
        ut = _mm(h, wu_ref[:, lo:hi])
        a = (_silu(gt) * ut).astype(BF16)
        y = y + _mm(a, wd_ref[lo:hi, :])
    out = x + 0.5 * y
    if gfin is not None:
        out = _rms(out, gfin)
    return out


def _ffn_kernel(xp_ref, xs_ref, g_ref, gfin_ref, wg_ref, wu_ref, wd_ref, op_ref, os_ref, *, gi, final):
    g = g_ref[gi:gi + 1, :]
    gfin = gfin_ref[...] if final else None
    op_ref[...] = _ffn_rows(xp_ref[...], g, wg_ref, wu_ref, wd_ref, gfin)

    @pl.when(pl.program_id(0) == pl.num_programs(0) - 1)
    def _():
        os_ref[...] = _ffn_rows(xs_ref[...], g, wg_ref, wu_ref, wd_ref, gfin)


def _ffn(xp, xs, g_all, gfin, wg, wu, wd, layer, idx, tm, final=False):
    n = xp.shape[0]
    ns = xs.shape[0]
    w_in_spec = pl.BlockSpec((None, None, D_MODEL, D_FF), lambda i: (layer, idx, 0, 0), pipeline_mode=pl.Buffered(1))
    w_out_spec = pl.BlockSpec((None, None, D_FF, D_MODEL), lambda i: (layer, idx, 0, 0), pipeline_mode=pl.Buffered(1))
    return pl.pallas_call(
        functools.partial(_ffn_kernel, gi=3 * layer + 2 * idx, final=final),
        grid=(n // tm,),
        in_specs=[pl.BlockSpec((tm, D_MODEL), lambda i: (i, 0)),
                  _const_spec((ns, D_MODEL)),
                  _const_spec(g_all.shape),
                  _const_spec((1, D_MODEL)),
                  w_in_spec, w_in_spec, w_out_spec],
        out_specs=[pl.BlockSpec((tm, D_MODEL), lambda i: (i, 0)),
                   pl.BlockSpec((ns, D_MODEL), lambda i: (0, 0))],
        out_shape=[jax.ShapeDtypeStruct((n, D_MODEL), F32), jax.ShapeDtypeStruct((ns, D_MODEL), F32)],
        compiler_params=_params(("arbitrary",)),
        name="ffn_final" if final else "ffn",
    )(xp, xs, g_all, gfin, wg, wu, wd)


def _ret_log_gamma(h):
    return math.log(1.0 - 2.0 ** (-5.0 - h))


def _hgrn_gates(gq, gf, lb):
    f = lb + (1.0 - lb) * _sigmoid(gf)
    kk = (1.0 - lb) * _sigmoid(-gf)
    qq = _silu(gq)
    return qq, kk, f


def _split3(x):
    hi = x.astype(BF16)
    r1 = x - hi.astype(F32)
    mid = r1.astype(BF16)
    lo = (r1 - mid.astype(F32)).astype(BF16)
    return hi, mid, lo


def _shift_rows(bases, d):
    base = bases[d % SUBLANES]
    full = (d // SUBLANES) * SUBLANES
    return pltpu.roll(base, full, 0) if full else base


def _hgrn_tile_factorised(qq, kk, vv, b, st):
    tt = qq[0].shape[0]
    heads = range(len(qq))
    qx = [(qq[h] * jnp.exp(b[h])).astype(BF16) for h in heads]
    kx = [kk[h] * jnp.exp(-b[h]) for h in heads]
    kxb = [kx[h].astype(BF16) for h in heads]
    vb = [vv[h].astype(BF16) for h in heads]
    ti = lax.broadcasted_iota(jnp.int32, (HG_CHUNK, HG_CHUNK), 0)
    si = lax.broadcasted_iota(jnp.int32, (HG_CHUNK, HG_CHUNK), 1)
    causal = si <= ti
    st = list(st)
    o_chunks = [[] for _ in heads]
    for c in range(tt // HG_CHUNK):
        rs = slice(c * HG_CHUNK, (c + 1) * HG_CHUNK)
        for h in heads:
            a = jnp.where(causal, _nt(qx[h][rs], kxb[h][rs]), 0.0)
            o_chunks[h].append(_mm(a.astype(BF16), vb[h][rs]) + _nt(qx[h][rs], st[h].astype(BF16)))
            etot = jnp.exp(b[h][(c + 1) * HG_CHUNK - 1:(c + 1) * HG_CHUNK])
            st[h] = st[h] * etot + _tn(vb[h][rs], (kx[h][rs] * etot).astype(BF16))
    return [jnp.concatenate(o_chunks[h], axis=0) for h in heads], st


def _hgrn_tile_guarded(qq_all, kk_all, f_all, vv_all, b_all, st_all):
    tt = qq_all.shape[0]
    hd = G_EXP
    row = lax.broadcasted_iota(jnp.int32, (tt, hd), 0)
    rsub = row % HG_SUB
    sub = (lax.broadcasted_iota(jnp.int32, (HG_CHUNK, hd), 0)) // HG_SUB
    n_sub = HG_CHUNK // HG_SUB
    outs, states = [], []
    for h in range(G_HEADS):
        hs = slice(h * hd, (h + 1) * hd)
        qq = qq_all[:, hs]
        kk = kk_all[:, hs]
        ff = f_all[:, hs]
        bc_all = b_all[:, hs]
        vv = vv_all[:, hs]
        f_sh = [ff] + [pltpu.roll(ff, r, 0) for r in range(1, SUBLANES)]
        k_sh = [kk] + [pltpu.roll(kk, r, 0) for r in range(1, SUBLANES)]
        v_sh = [vv] + [pltpu.roll(vv, r, 0) for r in range(1, SUBLANES)]
        o_band = jnp.sum(qq * kk, axis=-1, keepdims=True) * vv
        decay = None
        for d in range(1, HG_SUB):
            fd = _shift_rows(f_sh, d - 1)
            decay = jnp.where(rsub >= d, fd if decay is None else decay * fd, 0.0)
            term = qq * _shift_rows(k_sh, d) * decay
            o_band = o_band + jnp.sum(term, axis=-1, keepdims=True) * _shift_rows(v_sh, d)
        st = st_all[h]
        o_chunks = []
        for c in range(tt // HG_CHUNK):
            r0 = c * HG_CHUNK
            bc = bc_all[r0:r0 + HG_CHUNK]
            qc = qq[r0:r0 + HG_CHUNK]
            kc = kk[r0:r0 + HG_CHUNK]
            vcb = vv[r0:r0 + HG_CHUNK].astype(BF16)
            refs = [bc[i * HG_SUB - 1:i * HG_SUB] for i in range(1, n_sub)]
            refrow = refs[-1]
            for i in range(n_sub - 2, 0, -1):
                refrow = jnp.where(sub == i, refs[i - 1], refrow)
            qp = qc * jnp.exp(bc - refrow)
            lhs = jnp.concatenate([jnp.where(sub == i, qp, 0.0) for i in range(1, n_sub)], axis=1)
            kcat = jnp.concatenate([jnp.where(sub < i, kc * jnp.exp(refs[i - 1] - bc), 0.0)
                                    for i in range(1, n_sub)], axis=1)
            a = _nt(lhs.astype(BF16), kcat.astype(BF16))
            qb = (qc * jnp.exp(bc)).astype(BF16)
            o_chunks.append(_mm(a.astype(BF16), vcb) + _nt(qb, st.astype(BF16)))
            btot = bc[HG_CHUNK - 1:HG_CHUNK]
            ke = (kc * jnp.exp(btot - bc)).astype(BF16)
            st = st * jnp.exp(btot) + _tn(vcb, ke)
        outs.append(o_band + jnp.concatenate(o_chunks, axis=0))
        states.append(st)
    return jnp.concatenate(outs, axis=1), jnp.stack(states)


def _even_prompt_kernel(x_ref, g_ref, win_ref, wout_ref, cos_ref, sin_ref, retg_ref, hgg_ref, lb_ref,
                        y_ref, sret_ref, shg_ref, dmat_ref, tri_ref, st_ref, ohg_ref, *, tt, gi):
    b_id = pl.program_id(0)
    t_id = pl.program_id(1)
    n_t = pl.num_programs(1)
    hd = 128

    @pl.when(jnp.logical_and(b_id == 0, t_id == 0))
    def _():
        ti = lax.broadcasted_iota(jnp.int32, (tt, tt), 0)
        si = lax.broadcasted_iota(jnp.int32, (tt, tt), 1)
        diff = (ti - si).astype(F32)
        for h in range(R_HEADS):
            dmat_ref[h] = jnp.where(diff >= 0.0, jnp.exp(_ret_log_gamma(h) * jnp.maximum(diff, 0.0)), 0.0)
        same_chunk = (ti // HG_CHUNK) == (si // HG_CHUNK)
        tri_ref[...] = jnp.where(jnp.logical_and(same_chunk, si <= ti), 1.0, 0.0).astype(BF16)

    @pl.when(t_id == 0)
    def _():
        sret_ref[...] = jnp.zeros_like(sret_ref)
        st_ref[...] = jnp.zeros_like(st_ref)

    x = x_ref[0]
    hn = _rms(x, g_ref[gi:gi + 1, :]).astype(BF16)
    cos = cos_ref[...]
    sin = sin_ref[...]
    rowf = lax.broadcasted_iota(jnp.int32, (tt, hd), 0).astype(F32)
    bw = 4 * hd
    outs = []
    hg = []
    logf_parts = []
    st_old = st_ref[...]

    p = _mm(hn, win_ref[...])
    for i in range(R_HEADS + G_HEADS):
        pb = p[:, i * bw:(i + 1) * bw]
        h = i // 2
        if i % 2 == 0:
            lg = _ret_log_gamma(h)
            q = _rotary(pb[:, :hd], cos, sin)
            k = _rotary(pb[:, hd:2 * hd], cos, sin) * (R_DK ** -0.5)
            v = pb[:, 2 * hd:3 * hd]
            rg = pb[:, 3 * hd:]
            s0 = sret_ref[0, h]
            qb = q.astype(BF16)
            vb = v.astype(BF16)
            inter = _mm(qb, s0.astype(BF16)) * jnp.exp(lg * (rowf + 1.0))
            scores = _nt(qb, k.astype(BF16)) * dmat_ref[h]
            intra = _mm(scores.astype(BF16), vb)
            kd = (k * jnp.exp(lg * (tt - 1.0 - rowf))).astype(BF16)
            sret_ref[0, h] = math.exp(lg * tt) * s0 + _tn(kd, vb)
            outs.append(_head_norm(inter + intra, retg_ref[h:h + 1, :], False) * _silu(rg))
        else:
            qq, kk, ff = _hgrn_gates(pb[:, :hd], pb[:, hd:2 * hd], lb_ref[:, h * hd:(h + 1) * hd])
            hg.append((qq, kk, ff, pb[:, 2 * hd:3 * hd], _sigmoid(pb[:, 3 * hd:])))
            logf_parts.extend(_split3(jnp.log(ff)))

    cs = _mm(tri_ref[...], jnp.concatenate(logf_parts, axis=1))
    b = [cs[:, (3 * h) * hd:(3 * h + 1) * hd] + cs[:, (3 * h + 1) * hd:(3 * h + 2) * hd]
         + cs[:, (3 * h + 2) * hd:(3 * h + 3) * hd] for h in range(G_HEADS)]
    b_all = jnp.concatenate(b, axis=1)
    o_fast, st_fast = _hgrn_tile_factorised([t[0] for t in hg], [t[1] for t in hg], [t[3] for t in hg], b,
                                            [st_old[h] for h in range(G_HEADS)])
    for h in range(G_HEADS):
        ohg_ref[:, h * hd:(h + 1) * hd] = o_fast[h]
        st_ref[h] = st_fast[h]

    @pl.when(jnp.min(b_all) < HG_SAFE_LOG_DECAY)
    def _():
        cat = lambda j: jnp.concatenate([t[j] for t in hg], axis=1)
        o_safe, st_safe = _hgrn_tile_guarded(cat(0), cat(1), cat(2), cat(3), b_all, st_old)
        ohg_ref[...] = o_safe
        st_ref[...] = st_safe

    for h in range(G_HEADS):
        outs.append(_head_norm(ohg_ref[:, h * hd:(h + 1) * hd], hgg_ref[h:h + 1, :], False) * hg[h][4])


    ycat = jnp.concatenate(outs, axis=1).astype(BF16)
    y_ref[0] = x + _mm(ycat, wout_ref[...])

    @pl.when(t_id == n_t - 1)
    def _():
        for h in range(G_HEADS):
            shg_ref[0, h] = st_ref[h].T


def _rope_tables(pos):
    half = R_DK // 2
    inv = ROPE_BASE ** (-jnp.arange(half, dtype=F32) / half)
    ang = pos.astype(F32)[:, None] * inv[None, :]
    cos = jnp.cos(ang)
    sin = jnp.sin(ang)
    return jnp.concatenate([cos, cos], axis=-1), jnp.concatenate([-sin, sin], axis=-1)


def _even_prompt(x, g_all, gi, win, wout, retg, hgg, lb, tt):
    bsz, seq, _ = x.shape
    cos, sin = _rope_tables(jnp.arange(seq, dtype=jnp.int32))
    state_spec = pl.BlockSpec((1, 4, 128, 128), lambda b, t: (b, 0, 0, 0))
    return pl.pallas_call(
        functools.partial(_even_prompt_kernel, tt=tt, gi=gi),
        grid=(bsz, seq // tt),
        in_specs=[pl.BlockSpec((1, tt, D_MODEL), lambda b, t: (b, t, 0)),
                  _const_spec(g_all.shape),
                  _const_spec((D_MODEL, EVEN_IN)),
                  _const_spec((EVEN_OUT, D_MODEL)),
                  pl.BlockSpec((tt, 128), lambda b, t: (t, 0)),
                  pl.BlockSpec((tt, 128), lambda b, t: (t, 0)),
                  _const_spec((4, 128)), _const_spec((4, 128)), _const_spec((1, 512))],
        out_specs=[pl.BlockSpec((1, tt, D_MODEL), lambda b, t: (b, t, 0)), state_spec, state_spec],
        out_shape=[jax.ShapeDtypeStruct(x.shape, F32),
                   jax.ShapeDtypeStruct((bsz, 4, 128, 128), F32),
                   jax.ShapeDtypeStruct((bsz, 4, 128, 128), F32)],
        scratch_shapes=[pltpu.VMEM((4, tt, tt), F32), pltpu.VMEM((tt, tt), BF16), pltpu.VMEM((4, 128, 128), F32),
                        pltpu.VMEM((tt, G_HEADS * G_DV), F32)],
        compiler_params=_params(("arbitrary", "arbitrary")),
        name="even_prompt",
    )(x, g_all, win, wout, cos, sin, retg, hgg, lb)


SB = 8


def _even_sample_kernel(x_ref, g_ref, win_ref, wout_ref, cos_ref, sin_ref, retg_ref, hgg_ref, lb_ref,
                        sret_in, shg_in, y_ref, sret_out, shg_out, p_ref, o_ref, *, gi):
    i = pl.program_id(0)
    n_i = pl.num_programs(0)
    hd = 128

    @pl.when(i == 0)
    def _():
        hn = _rms(x_ref[...], g_ref[gi:gi + 1, :]).astype(BF16)
        p_ref[...] = _mm(hn, win_ref[...])

    r0 = pl.multiple_of(i * SB, SB)
    p = p_ref[pl.ds(r0, SB), :]
    cos = cos_ref[...]
    sin = sin_ref[...]
    row = lax.broadcasted_iota(jnp.int32, (SB, hd), 0)
    outs = []
    for h in range(R_HEADS):
        gamma = math.exp(_ret_log_gamma(h))
        pb = p[:, 2 * h * 4 * hd:(2 * h + 1) * 4 * hd]
        q = _rotary(pb[:, :hd], cos, sin).astype(BF16)
        k = _rotary(pb[:, hd:2 * hd], cos, sin) * (R_DK ** -0.5)
        vb = pb[:, 2 * hd:3 * hd].astype(BF16)
        rg = pb[:, 3 * hd:]
        o = jnp.zeros((SB, hd), F32)
        for j in range(SB):
            kj = jnp.where(row == j, k, 0.0).astype(BF16)
            s_new = gamma * sret_in[j, h] + _tn(kj, vb)
            sret_out[j, h] = s_new
            o = jnp.where(row == j, _mm(q, s_new.astype(BF16)), o)
        outs.append(_head_norm(o, retg_ref[h:h + 1, :], False) * _silu(rg))
    for h in range(G_HEADS):
        pb = p[:, (2 * h + 1) * 4 * hd:(2 * h + 2) * 4 * hd]
        qq, kk, ff = _hgrn_gates(pb[:, :hd], pb[:, hd:2 * hd], lb_ref[:, h * hd:(h + 1) * hd])
        vb = pb[:, 2 * hd:3 * hd].astype(BF16)
        gg = pb[:, 3 * hd:]
        f_cols = jnp.concatenate([ff, jnp.zeros((hd - SB, hd), F32)], axis=0).T
        qb = qq.astype(BF16)
        o = jnp.zeros((SB, hd), F32)
        for j in range(SB):
            kj = jnp.where(row == j, kk, 0.0).astype(BF16)
            s_new = f_cols[:, j:j + 1] * shg_in[j, h] + _tn(kj, vb)
            shg_out[j, h] = s_new
            o = jnp.where(row == j, _mm(qb, s_new.astype(BF16)), o)
        outs.append(_head_norm(o, hgg_ref[h:h + 1, :], False) * _sigmoid(gg))
    o_ref[pl.ds(r0, SB), :] = jnp.concatenate(outs, axis=1)

    @pl.when(i == n_i - 1)
    def _():
        y_ref[...] = x_ref[...] + _mm(o_ref[...].astype(BF16), wout_ref[...])


def _even_sample(x, g_all, gi, win, wout, retg, hgg, lb, sret, shg):
    n = x.shape[0]
    cos, sin = _rope_tables(jnp.full((1,), PAST_LEN, dtype=jnp.int32))
    state_spec = pl.BlockSpec((SB, 4, 128, 128), lambda i: (i, 0, 0, 0))
    return pl.pallas_call(
        functools.partial(_even_sample_kernel, gi=gi),
        grid=(n // SB,),
        in_specs=[_const_spec((n, D_MODEL)),
                  _const_spec(g_all.shape),
                  _const_spec((D_MODEL, EVEN_IN)),
                  _const_spec((EVEN_OUT, D_MODEL)),
                  _const_spec((1, 128)), _const_spec((1, 128)),
                  _const_spec((4, 128)), _const_spec((4, 128)), _const_spec((1, 512)),
                  state_spec, state_spec],
        out_specs=[pl.BlockSpec((n, D_MODEL), lambda i: (0, 0)), state_spec, state_spec],
        out_shape=[jax.ShapeDtypeStruct((n, D_MODEL), F32),
                   jax.ShapeDtypeStruct(sret.shape, F32),
                   jax.ShapeDtypeStruct(shg.shape, F32)],
        scratch_shapes=[pltpu.VMEM((n, EVEN_IN), F32), pltpu.VMEM((n, EVEN_OUT), F32)],
        compiler_params=_params(("arbitrary",)),
        name="even_sample",
    )(x, g_all, win, wout, cos, sin, retg, hgg, lb, sret, shg)


def _block_diag(w):
    wr = w.reshape(M_INNER // BD, BD, QKV_BLOCK)
    tiled = jnp.tile(wr, (1, 1, BD // QKV_BLOCK))
    rb = lax.broadcasted_iota(jnp.int32, (BD, BD), 0) // QKV_BLOCK
    cb = lax.broadcasted_iota(jnp.int32, (BD, BD), 1) // QKV_BLOCK
    return jnp.where((rb == cb)[None], tiled, 0.0)


def _headwise(xb, w_ref, g0=0):
    return jnp.concatenate([_mm(xb[:, g * BD:(g + 1) * BD], w_ref[g0 + g]) for g in range(xb.shape[1] // BD)], axis=1)


def _gate_weights(w_ig, w_fg):
    w = jnp.concatenate([w_ig, w_fg], axis=1)
    w = jnp.pad(w, ((0, 0), (0, 128 - 2 * M_HEADS)))
    return w.reshape(3, M_INNER, 128)


def _mlstm_out(hs, xc, z, normg_ref, skip_ref, wdown_ref):
    hc = jnp.concatenate([_head_norm(hs[h], normg_ref[:, h * M_DH:(h + 1) * M_DH], True) for h in range(M_HEADS)],
                         axis=1)
    hc = hc + skip_ref[...] * xc
    return _mm((hc * _silu(z)).astype(BF16), wdown_ref[...])


def _odd_prompt_kernel(x_ref, g_ref, win_ref, convw_ref, convb_ref, wq_ref, wk_ref, wv_ref,
                       wgate_t_ref, bgate_t_ref, normg_ref, skip_ref, wdown_ref,
                       y_ref, c_ref, n_ref, m_out_ref, conv_out_ref,
                       carry_ref, m_ref, xc_ref, q_ref, k_ref, v_ref, *, tt, gi):
    t_id = pl.program_id(1)
    k_scale = M_DH ** -0.5

    @pl.when(t_id == 0)
    def _():
        c_ref[...] = jnp.zeros_like(c_ref)
        n_ref[...] = jnp.zeros_like(n_ref)
        m_ref[...] = jnp.zeros_like(m_ref)
        carry_ref[...] = jnp.zeros_like(carry_ref)

    x = x_ref[0]
    hn = _rms(x, g_ref[gi:gi + 1, :]).astype(BF16)

    gates_t = bgate_t_ref[...]
    row8 = lax.broadcasted_iota(jnp.int32, (SUBLANES, M_DH), 0)
    tiles = M_DH // BD
    xm_next = _mm(hn, win_ref[:, :M_DH])
    for h in range(M_HEADS):
        sl = slice(h * M_DH, (h + 1) * M_DH)
        xm = xm_next
        if h + 1 < M_HEADS:
            xm_next = _mm(hn, win_ref[:, (h + 1) * M_DH:(h + 2) * M_DH])
        carry = carry_ref[:, sl]
        conv = convb_ref[:, sl] + convw_ref[M_CONV - 1:M_CONV, sl] * xm
        for j in range(1, M_CONV):
            rolled = pltpu.roll(xm, j, 0)
            head = jnp.where(row8 < j, pltpu.roll(carry, j, 0), rolled[:SUBLANES])
            shifted = jnp.concatenate([head, rolled[SUBLANES:]], axis=0)
            conv = conv + convw_ref[M_CONV - 1 - j:M_CONV - j, sl] * shifted
        carry_ref[:, sl] = xm[tt - SUBLANES:, :]
        conv_out_ref[0, :, sl] = xm[tt - (M_CONV - 1):, :]
        xc = _silu(conv)
        xc_ref[:, sl] = xc
        xcb = xc.astype(BF16)
        qb = _headwise(xcb, wq_ref, h * tiles).astype(BF16)
        kb = _headwise(xcb, wk_ref, h * tiles).astype(BF16)
        vb = _headwise(xm.astype(BF16), wv_ref, h * tiles).astype(BF16)
        q_ref[:, sl] = qb
        k_ref[:, sl] = kb
        v_ref[:, sl] = vb
        gates_t = gates_t + (_nt(wgate_t_ref[0, :, sl], qb) + _nt(wgate_t_ref[1, :, sl], kb)
                             + _nt(wgate_t_ref[2, :, sl], vb))

    lane8 = lax.broadcasted_iota(jnp.int32, (SUBLANES, tt), 1)
    row8t = lax.broadcasted_iota(jnp.int32, (SUBLANES, tt), 0)
    brow_all = _log_sigmoid(gates_t)
    d = 1
    while d < tt:
        brow_all = brow_all + jnp.where(lane8 >= d, pltpu.roll(brow_all, d, 1), 0.0)
        d *= 2
    rows = jnp.where(row8t < M_HEADS, gates_t, brow_all)
    pad = jnp.zeros((128 - SUBLANES, 128), F32)
    cols = jnp.concatenate([jnp.concatenate([rows[:, j * 128:(j + 1) * 128], pad], axis=0).T
                            for j in range(tt // 128)], axis=0)

    ti = lax.broadcasted_iota(jnp.int32, (tt, tt), 0)
    si = lax.broadcasted_iota(jnp.int32, (tt, tt), 1)
    causal = si <= ti
    y = jnp.zeros((tt, D_MODEL), F32)
    for h in range(M_HEADS):
        sl = slice(h * M_DH, (h + 1) * M_DH)
        qhb = q_ref[:, sl]
        khb = k_ref[:, sl]
        vhb = v_ref[:, sl]
        ig_col = cols[:, h:h + 1]
        b_col = cols[:, M_HEADS + h:M_HEADS + h + 1]
        ig_row = gates_t[h:h + 1, :]
        b_row = brow_all[M_HEADS + h:M_HEADS + h + 1, :]
        m_prev = m_ref[h:h + 1, 0:1]
        c_prev = c_ref[0, h]
        n_prev = n_ref[0, h:h + 1, :]

        dlog = jnp.where(causal, b_col + (ig_row - b_row), -jnp.inf)
        inter_log = b_col + m_prev
        m_row = jnp.maximum(inter_log, jnp.max(dlog, axis=-1, keepdims=True))
        w_inter = jnp.exp(inter_log - m_row)
        qk = _nt(qhb, khb) * (jnp.exp(dlog - m_row) * k_scale)
        num = w_inter * _mm(qhb, c_prev.astype(BF16)) + _mm(qk.astype(BF16), vhb)
        qn = _nt(qhb, jnp.broadcast_to(n_prev, (SUBLANES, M_DH)).astype(BF16))[:, 0:1]
        den = w_inter * qn + jnp.sum(qk, axis=-1, keepdims=True)
        den = jnp.maximum(jnp.abs(den), jnp.exp(-m_row))
        hh = num * (1.0 / den)

        b_end = b_col[tt - 1:tt, :]
        s_log = b_end - b_col + ig_col
        m_new = jnp.maximum(b_end + m_prev, jnp.max(s_log, axis=0, keepdims=True))
        a = jnp.exp(b_end + m_prev - m_new)
        kw = khb.astype(F32) * (jnp.exp(s_log - m_new) * k_scale)
        c_ref[0, h] = a * c_prev + _tn(kw.astype(BF16), vhb)
        n_ref[0, h:h + 1, :] = a * n_prev + jnp.sum(kw, axis=0, keepdims=True)
        m_ref[h:h + 1, :] = jnp.broadcast_to(m_new, (1, 128))

        hc = _head_norm(hh, normg_ref[:, sl], True) + skip_ref[:, sl] * xc_ref[:, sl]
        z = _mm(hn, win_ref[:, M_INNER + h * M_DH:M_INNER + (h + 1) * M_DH])
        y = y + _mm((hc * _silu(z)).astype(BF16), wdown_ref[sl, :])

    y_ref[0] = x + y
    m_out_ref[0] = m_ref[...]


def _odd_prompt(x, g_all, gi, ml, tt):
    bsz, seq, _ = x.shape
    bt = jnp.broadcast_to(ml['bgate_col'], (SUBLANES, tt))
    return pl.pallas_call(
        functools.partial(_odd_prompt_kernel, tt=tt, gi=gi),
        grid=(bsz, seq // tt),
        in_specs=[pl.BlockSpec((1, tt, D_MODEL), lambda b, t: (b, t, 0)),
                  _const_spec(g_all.shape),
                  _const_spec((D_MODEL, 2 * M_INNER)),
                  _const_spec((M_CONV, M_INNER)),
                  _const_spec((1, M_INNER)),
                  _const_spec((M_INNER // BD, BD, BD)),
                  _const_spec((M_INNER // BD, BD, BD)),
                  _const_spec((M_INNER // BD, BD, BD)),
                  _const_spec((3, SUBLANES, M_INNER)),
                  _const_spec((SUBLANES, tt)),
                  _const_spec((1, M_INNER)),
                  _const_spec((1, M_INNER)),
                  _const_spec((M_INNER, D_MODEL))],
        out_specs=[pl.BlockSpec((1, tt, D_MODEL), lambda b, t: (b, t, 0)),
                   pl.BlockSpec((1, M_HEADS, M_DH, M_DH), lambda b, t: (b, 0, 0, 0)),
                   pl.BlockSpec((1, M_HEADS, M_DH), lambda b, t: (b, 0, 0)),
                   pl.BlockSpec((1, SUBLANES, 128), lambda b, t: (b, 0, 0)),
                   pl.BlockSpec((1, M_CONV - 1, M_INNER), lambda b, t: (b, 0, 0))],
        out_shape=[jax.ShapeDtypeStruct(x.shape, F32),
                   jax.ShapeDtypeStruct((bsz, M_HEADS, M_DH, M_DH), F32),
                   jax.ShapeDtypeStruct((bsz, M_HEADS, M_DH), F32),
                   jax.ShapeDtypeStruct((bsz, SUBLANES, 128), F32),
                   jax.ShapeDtypeStruct((bsz, M_CONV - 1, M_INNER), F32)],
        scratch_shapes=[pltpu.VMEM((SUBLANES, M_INNER), F32), pltpu.VMEM((SUBLANES, 128), F32),
                        pltpu.VMEM((tt, M_INNER), F32), pltpu.VMEM((tt, M_INNER), BF16),
                        pltpu.VMEM((tt, M_INNER), BF16), pltpu.VMEM((tt, M_INNER), BF16)],
        compiler_params=_params(("arbitrary", "arbitrary")),
        name="odd_prompt",
    )(x, g_all, ml['win'], ml['convw'], ml['convb'], ml['wq'], ml['wk'], ml['wv'],
      ml['wgate_t'], bt, ml['normg'], ml['skip'], ml['wdown'])


def _pick_row(ref, b):
    r0 = pl.multiple_of((b // SUBLANES) * SUBLANES, SUBLANES)
    blk = ref[pl.ds(r0, SUBLANES), :]
    row = lax.broadcasted_iota(jnp.int32, blk.shape, 0)
    return jnp.sum(jnp.where(row == b % SUBLANES, blk, 0.0), axis=0, keepdims=True)


def _put_row(ref, b, val):
    r0 = pl.multiple_of((b // SUBLANES) * SUBLANES, SUBLANES)
    blk = ref[pl.ds(r0, SUBLANES), :]
    row = lax.broadcasted_iota(jnp.int32, blk.shape, 0)
    ref[pl.ds(r0, SUBLANES), :] = jnp.where(row == b % SUBLANES, jnp.broadcast_to(val, blk.shape), blk)


def _odd_sample_kernel(x_ref, g_ref, win_ref, convw_ref, convb_ref, cv_ref,
                       wq_ref, wk_ref, wv_ref, wgate_ref, bgate_ref, normg_ref, skip_ref, wdown_ref,
                       m_in_ref, c_in, n_in,
                       y_ref, c_out, n_out, m_out_ref, cv_out_ref,
                       q_ref, k_ref, v_ref, gate_ref, xc_ref, z_ref, h_ref, *, gi):
    b = pl.program_id(0)
    n_b = pl.num_programs(0)

    @pl.when(b == 0)
    def _():
        hn = _rms(x_ref[...], g_ref[gi:gi + 1, :]).astype(BF16)
        p = _mm(hn, win_ref[...])
        xm = p[:, :M_INNER]
        z_ref[...] = p[:, M_INNER:]
        cv0 = cv_ref[:, :M_INNER]
        cv1 = cv_ref[:, M_INNER:2 * M_INNER]
        cv2 = cv_ref[:, 2 * M_INNER:]
        cv_out_ref[:, :M_INNER] = cv1
        cv_out_ref[:, M_INNER:2 * M_INNER] = cv2
        cv_out_ref[:, 2 * M_INNER:] = xm
        conv = (convb_ref[...] + convw_ref[3:4, :] * xm + convw_ref[2:3, :] * cv2
                + convw_ref[1:2, :] * cv1 + convw_ref[0:1, :] * cv0)
        xc = _silu(conv)
        xc_ref[...] = xc
        xcb = xc.astype(BF16)
        q = _headwise(xcb, wq_ref)
        k = _headwise(xcb, wk_ref)
        v = _headwise(xm.astype(BF16), wv_ref)
        q_ref[...] = q
        k_ref[...] = k * (M_DH ** -0.5)
        v_ref[...] = v
        gate_ref[...] = (_mm(q.astype(BF16), wgate_ref[0]) + _mm(k.astype(BF16), wgate_ref[1])
                         + _mm(v.astype(BF16), wgate_ref[2])) + bgate_ref[...]
        m_out_ref[...] = jnp.zeros_like(m_out_ref)
        h_ref[...] = jnp.zeros_like(h_ref)

    r0 = pl.multiple_of((b // SUBLANES) * SUBLANES, SUBLANES)
    row8 = lax.broadcasted_iota(jnp.int32, (SUBLANES, M_INNER), 0)
    sel = row8 == b % SUBLANES
    q8 = jnp.where(sel, q_ref[pl.ds(r0, SUBLANES), :], 0.0)
    k8 = jnp.where(sel, k_ref[pl.ds(r0, SUBLANES), :], 0.0)
    v8 = jnp.where(sel, v_ref[pl.ds(r0, SUBLANES), :], 0.0)
    q8b = q8.astype(BF16)
    k8b = k8.astype(BF16)
    v8b = v8.astype(BF16)
    k_row = jnp.sum(k8, axis=0, keepdims=True)
    q_row = jnp.sum(q8, axis=0, keepdims=True)
    gate = _pick_row(gate_ref, b)
    m_all = _pick_row(m_in_ref, b)
    lane = lax.broadcasted_iota(jnp.int32, (1, 128), 1)
    m_new_all = jnp.zeros((1, 128), F32)
    h_parts = []
    for h in range(M_HEADS):
        sl = slice(h * M_DH, (h + 1) * M_DH)
        ig = gate[:, h:h + 1]
        lf = _log_sigmoid(gate[:, M_HEADS + h:M_HEADS + h + 1])
        m_prev = m_all[:, h:h + 1]
        m_new = jnp.maximum(lf + m_prev, ig)
        a = jnp.exp(lf + m_prev - m_new)
        ws = jnp.exp(ig - m_new)
        c_new = a * c_in[0, h] + ws * _tn(k8b[:, sl], v8b[:, sl])
        c_out[0, h] = c_new
        n_new = a * n_in[0, h:h + 1, :] + ws * k_row[:, sl]
        n_out[0, h:h + 1, :] = n_new
        num = jnp.sum(_mm(q8b[:, sl], c_new.astype(BF16)), axis=0, keepdims=True)
        den = jnp.sum(q_row[:, sl] * n_new, axis=-1, keepdims=True)
        den = jnp.maximum(jnp.abs(den), jnp.exp(-m_new))
        h_parts.append(num / den)
        m_new_all = jnp.where(lane == h, m_new, m_new_all)
    _put_row(h_ref, b, jnp.concatenate(h_parts, axis=1))
    _put_row(m_out_ref, b, m_new_all)

    @pl.when(b == n_b - 1)
    def _():
        hfull = h_ref[...]
        hs = [hfull[:, h * M_DH:(h + 1) * M_DH] for h in range(M_HEADS)]
        y_ref[...] = x_ref[...] + _mlstm_out(hs, xc_ref[...], z_ref[...], normg_ref, skip_ref, wdown_ref)


def _odd_sample(x, g_all, gi, ml, c0, n0, m0, conv0):
    n = x.shape[0]
    m_pad = jnp.pad(m0, ((0, 0), (0, 128 - M_HEADS)))
    cw = (M_CONV - 1) * M_INNER
    full = lambda shape: pl.BlockSpec(shape, lambda b: (0,) * len(shape))
    outs = pl.pallas_call(
        functools.partial(_odd_sample_kernel, gi=gi),
        grid=(n,),
        in_specs=[_const_spec((n, D_MODEL)),
                  _const_spec(g_all.shape),
                  _const_spec((D_MODEL, 2 * M_INNER)),
                  _const_spec((M_CONV, M_INNER)),
                  _const_spec((1, M_INNER)),
                  _const_spec((n, cw)),
                  _const_spec((M_INNER // BD, BD, BD)),
                  _const_spec((M_INNER // BD, BD, BD)),
                  _const_spec((M_INNER // BD, BD, BD)),
                  _const_spec((3, M_INNER, 128)),
                  _const_spec((1, 128)),
                  _const_spec((1, M_INNER)),
                  _const_spec((1, M_INNER)),
                  _const_spec((M_INNER, D_MODEL)),
                  _const_spec((n, 128)),
                  pl.BlockSpec((1, M_HEADS, M_DH, M_DH), lambda b: (b, 0, 0, 0)),
                  pl.BlockSpec((1, M_HEADS, M_DH), lambda b: (b, 0, 0))],
        out_specs=[full((n, D_MODEL)),
                   pl.BlockSpec((1, M_HEADS, M_DH, M_DH), lambda b: (b, 0, 0, 0)),
                   pl.BlockSpec((1, M_HEADS, M_DH), lambda b: (b, 0, 0)),
                   full((n, 128)),
                   full((n, cw))],
        out_shape=[jax.ShapeDtypeStruct((n, D_MODEL), F32),
                   jax.ShapeDtypeStruct(c0.shape, F32),
                   jax.ShapeDtypeStruct(n0.shape, F32),
                   jax.ShapeDtypeStruct((n, 128), F32),
                   jax.ShapeDtypeStruct((n, cw), F32)],
        scratch_shapes=[pltpu.VMEM((n, M_INNER), F32), pltpu.VMEM((n, M_INNER), F32), pltpu.VMEM((n, M_INNER), F32),
                        pltpu.VMEM((n, 128), F32), pltpu.VMEM((n, M_INNER), F32), pltpu.VMEM((n, M_INNER), F32),
                        pltpu.VMEM((n, M_INNER), F32)],
        compiler_params=_params(("arbitrary",)),
        name="odd_sample",
    )(x, g_all, ml['win'], ml['convw'], ml['convb'], conv0.reshape(n, cw),
      ml['wq'], ml['wk'], ml['wv'], ml['wgate'], ml['bgate'], ml['normg'], ml['skip'], ml['wdown'],
      m_pad, c0, n0)
    y, c, nn, m_new, conv_new = outs
    return y, c, nn, m_new[:, :M_HEADS], conv_new.reshape(n, M_CONV - 1, M_INNER)


TM_FFN = 512
TT_EVEN = 256
TT_ODD = 512


def kernel(x_prompt, x_sample, state_ret, state_hgrn, state_mlstm_C, state_mlstm_n, state_mlstm_m, state_mlstm_conv,
           norm_g, final_norm_g, ffn_w_gate, ffn_w_up, ffn_w_down, ev_w_in, ev_w_out, ret_norm_g, hg_norm_g,
           hg_lb_logits, ml_w_in, ml_conv_w, ml_conv_b, ml_w_q, ml_w_k, ml_w_v, ml_w_ig, ml_b_ig, ml_w_fg,
           ml_b_fg, ml_norm_g, ml_skip, ml_w_down):
    bp, tp, _ = x_prompt.shape
    ns = x_sample.shape[0]

    wg = ffn_w_gate.astype(BF16)
    wu = ffn_w_up.astype(BF16)
    wd = ffn_w_down.astype(BF16)
    ev_in = (ev_w_in[0].reshape(D_MODEL, 2, 4, R_HEADS, R_DK).transpose(0, 3, 1, 2, 4)
             .reshape(D_MODEL, EVEN_IN).astype(BF16))
    ev_out = ev_w_out[0].astype(BF16)
    g_all = norm_g.reshape(-1, D_MODEL)
    gfin = final_norm_g.reshape(1, D_MODEL)
    lb_all = jnp.cumsum(jax.nn.softmax(hg_lb_logits.astype(F32), axis=0), axis=0)
    lb = lb_all[0].reshape(1, G_HEADS * G_EXP)
    retg = ret_norm_g[0]
    hgg = hg_norm_g[0]
    wgate = _gate_weights(ml_w_ig[0], ml_w_fg[0])
    bgate = jnp.pad(jnp.concatenate([ml_b_ig[0], ml_b_fg[0]]), (0, 128 - 2 * M_HEADS)).reshape(1, 128)
    ml = {
        'win': ml_w_in[0].astype(BF16),
        'convw': ml_conv_w[0],
        'convb': ml_conv_b[0].reshape(1, M_INNER),
        'wq': _block_diag(ml_w_q[0]).astype(BF16),
        'wk': _block_diag(ml_w_k[0]).astype(BF16),
        'wv': _block_diag(ml_w_v[0]).astype(BF16),
        'wgate': wgate.astype(BF16),
        'wgate_t': jnp.swapaxes(wgate[:, :, :SUBLANES], 1, 2).astype(BF16),
        'bgate': bgate,
        'bgate_col': bgate[0, :SUBLANES].reshape(SUBLANES, 1),
        'normg': ml_norm_g[0].reshape(1, M_INNER),
        'skip': ml_skip[0].reshape(1, M_INNER),
        'wdown': ml_w_down[0].astype(BF16),
    }

    xp = x_prompt.reshape(bp * tp, D_MODEL)
    xs = x_sample.reshape(ns, D_MODEL)

    xp, xs = _ffn(xp, xs, g_all, gfin, wg, wu, wd, 0, 0, TM_FFN)
    xp, ret_p, hg_p = _even_prompt(xp.reshape(bp, tp, D_MODEL), g_all, 1, ev_in, ev_out, retg, hgg, lb, TT_EVEN)
    xs, ret_s, hg_s = _even_sample(xs, g_all, 1, ev_in, ev_out, retg, hgg, lb, state_ret[:, 0], state_hgrn[:, 0])
    xp, xs = _ffn(xp.reshape(bp * tp, D_MODEL), xs, g_all, gfin, wg, wu, wd, 0, 1, TM_FFN)
    xp, xs = _ffn(xp, xs, g_all, gfin, wg, wu, wd, 1, 0, TM_FFN)
    xp, c_p, n_p, m_p, conv_p = _odd_prompt(xp.reshape(bp, tp, D_MODEL), g_all, 4, ml, TT_ODD)
    xs, c_s, n_s, m_s, conv_s = _odd_sample(xs, g_all, 4, ml, state_mlstm_C[:, 0], state_mlstm_n[:, 0],
                                            state_mlstm_m[:, 0], state_mlstm_conv[:, 0])
    y_p, y_s = _ffn(xp.reshape(bp * tp, D_MODEL), xs, g_all, gfin, wg, wu, wd, 1, 1, TM_FFN, final=True)

    return (y_p.reshape(bp, tp, D_MODEL), y_s.reshape(ns, 1, D_MODEL),
            ret_p[:, None], hg_p[:, None], c_p[:, None], n_p[:, None], m_p[:, None, :M_HEADS, 0], conv_p[:, None],
            ret_s[:, None], hg_s[:, None], c_s[:, None], n_s[:, None], m_s[:, None], conv_s[:, None])
```

```python
import functools
import math

import jax
import jax.numpy as jnp
from jax import lax
from jax.experimental import pallas as pl
from jax.experimental.pallas import tpu as pltpu

D_MODEL = 1024
PAST_LEN = 16384
R_HEADS = 4
R_DK = 128
R_DV = 128
G_HEADS = 4
G_EXP = 128
G_DV = 128
M_INNER = 2 * D_MODEL
M_HEADS = 4
M_DH = M_INNER // M_HEADS
M_CONV = 4
QKV_BLOCK = 4
D_FF = 2816
EPS = 1e-6
ROPE_BASE = 10000.0
EVEN_IN = 4096
EVEN_OUT = 1024

F32 = jnp.float32
BF16 = jnp.bfloat16

VMEM_LIMIT_BYTES = 56 * 1024 * 1024

HG_CHUNK = 64
HG_SUB = 16
HG_SAFE_LOG_DECAY = -60.0
MXU_TILE = 256
FFN_BOUNDS = (0, 6 * MXU_TILE, D_FF)
BD = MXU_TILE
SUBLANES = 8


def _nt(a, b):
    return lax.dot_general(a, b, (((1,), (1,)), ((), ())), preferred_element_type=F32)


def _tn(a, b):
    return lax.dot_general(a, b, (((0,), (0,)), ((), ())), preferred_element_type=F32)


def _mm(a, b):
    return jnp.dot(a, b, preferred_element_type=F32)


def _sigmoid(x):
    return 1.0 / (1.0 + jnp.exp(-x))


def _silu(x):
    return x * _sigmoid(x)


def _log_sigmoid(x):
    return jnp.minimum(x, 0.0) - jnp.log(1.0 + jnp.exp(-jnp.abs(x)))


def _rms(x, g):
    return x * lax.rsqrt(jnp.mean(x * x, axis=-1, keepdims=True) + EPS) * g


def _head_norm(x, g, center):
    if center:
        x = x - jnp.mean(x, axis=-1, keepdims=True)
    return x * lax.rsqrt(jnp.mean(x * x, axis=-1, keepdims=True) + EPS) * g


def _rotary(x, cos, sin_signed):
    return x * cos + pltpu.roll(x, 64, 1) * sin_signed


def _const_spec(shape):
    n = len(shape)
    return pl.BlockSpec(shape, lambda *_: (0,) * n, pipeline_mode=pl.Buffered(1))


def _params(sem):
    return pltpu.CompilerParams(dimension_semantics=sem, vmem_limit_bytes=VMEM_LIMIT_BYTES)


def _ffn_rows(x, g, wg_ref, wu_ref, wd_ref, gfin):
    h = _rms(x, g).astype(BF16)
    y = jnp.zeros_like(x)
    for lo, hi in zip(FFN_BOUNDS[:-1], FFN_BOUNDS[1:]):
        gt = _mm(h, wg_ref[:, lo:hi])
        ut = _mm(h, wu_ref[:, lo:hi])
        a = (_silu(gt) * ut).astype(BF16)
        y = y + _mm(a, wd_ref[lo:hi, :])
    out = x + 0.5 * y
    if gfin is not None:
        out = _rms(out, gfin)
    return out


def _ffn_kernel(xp_ref, xs_ref, g_ref, gfin_ref, wg_ref, wu_ref, wd_ref, *rest, gi, final, n_cast):
    cast_in = rest[:n_cast]
    op_ref, os_ref = rest[n_cast:n_cast + 2]
    cast_out = rest[n_cast + 2:]
    g = g_ref[gi:gi + 1, :]
    gfin = gfin_ref[...] if final else None
    op_ref[...] = _ffn_rows(xp_ref[...], g, wg_ref, wu_ref, wd_ref, gfin)
    for src, dst in zip(cast_in, cast_out):
        dst[...] = src[...].astype(BF16)

    @pl.when(pl.program_id(0) == pl.num_programs(0) - 1)
    def _():
        os_ref[...] = _ffn_rows(xs_ref[...], g, wg_ref, wu_ref, wd_ref, gfin)


BF16_ROWS = 16


def _cast_specs(arr, lead, steps):
    rows, cols = arr.shape[-2:]
    per = 1 if (rows // steps) % BF16_ROWS == 0 else 2
    br = rows * per // steps
    in_spec = pl.BlockSpec((None,) * len(lead) + (br, cols), lambda i: tuple(lead) + (i // per, 0))
    out_spec = pl.BlockSpec((br, cols), lambda i: (i // per, 0))
    return in_spec, out_spec, jax.ShapeDtypeStruct((rows, cols), BF16)


def _ffn(xp, xs, g_all, gfin, wg, wu, wd, gi, tm, final=False, casts=()):
    n = xp.shape[0]
    ns = xs.shape[0]
    steps = n // tm
    cast_specs = [_cast_specs(arr, lead, steps) for arr, lead in casts]
    outs = pl.pallas_call(
        functools.partial(_ffn_kernel, gi=gi, final=final, n_cast=len(casts)),
        grid=(steps,),
        in_specs=[pl.BlockSpec((tm, D_MODEL), lambda i: (i, 0)),
                  _const_spec((ns, D_MODEL)),
                  _const_spec(g_all.shape),
                  _const_spec((1, D_MODEL)),
                  _const_spec((D_MODEL, D_FF)), _const_spec((D_MODEL, D_FF)), _const_spec((D_FF, D_MODEL))]
                 + [c[0] for c in cast_specs],
        out_specs=[pl.BlockSpec((tm, D_MODEL), lambda i: (i, 0)),
                   pl.BlockSpec((ns, D_MODEL), lambda i: (0, 0))] + [c[1] for c in cast_specs],
        out_shape=[jax.ShapeDtypeStruct((n, D_MODEL), F32), jax.ShapeDtypeStruct((ns, D_MODEL), F32)]
                  + [c[2] for c in cast_specs],
        compiler_params=_params(("arbitrary",)),
        name="ffn_final" if final else "ffn",
    )(xp, xs, g_all, gfin, wg, wu, wd, *[arr for arr, _ in casts])
    return outs[0], outs[1], outs[2:]


def _ret_log_gamma(h):
    return math.log(1.0 - 2.0 ** (-5.0 - h))


def _head_parts(p, mixer, h):
    base = mixer * 4 * R_HEADS * R_DK
    return [p[:, base + (j * R_HEADS + h) * R_DK:base + (j * R_HEADS + h + 1) * R_DK] for j in range(4)]


def _hgrn_gates(gq, gf, lb):
    f = lb + (1.0 - lb) * _sigmoid(gf)
    kk = (1.0 - lb) * _sigmoid(-gf)
    qq = _silu(gq)
    return qq, kk, f


def _split3(x):
    hi = x.astype(BF16)
    r1 = x - hi.astype(F32)
    mid = r1.astype(BF16)
    lo = (r1 - mid.astype(F32)).astype(BF16)
    return hi, mid, lo


def _shift_rows(bases, d):
    base = bases[d % SUBLANES]
    full = (d // SUBLANES) * SUBLANES
    return pltpu.roll(base, full, 0) if full else base


def _hgrn_tile_factorised(qq, kk, vv, b, st):
    tt = qq[0].shape[0]
    heads = range(len(qq))
    qx = [(qq[h] * jnp.exp(b[h])).astype(BF16) for h in heads]
    kx = [kk[h] * jnp.exp(-b[h]) for h in heads]
    kxb = [kx[h].astype(BF16) for h in heads]
    vb = [vv[h].astype(BF16) for h in heads]
    ti = lax.broadcasted_iota(jnp.int32, (HG_CHUNK, HG_CHUNK), 0)
    si = lax.broadcasted_iota(jnp.int32, (HG_CHUNK, HG_CHUNK), 1)
    causal = si <= ti
    st = list(st)
    o_chunks = [[] for _ in heads]
    for c in range(tt // HG_CHUNK):
        rs = slice(c * HG_CHUNK, (c + 1) * HG_CHUNK)
        for h in heads:
            a = jnp.where(causal, _nt(qx[h][rs], kxb[h][rs]), 0.0)
            o_chunks[h].append(_mm(a.astype(BF16), vb[h][rs]) + _nt(qx[h][rs], st[h].astype(BF16)))
            etot = jnp.exp(b[h][(c + 1) * HG_CHUNK - 1:(c + 1) * HG_CHUNK])
            st[h] = st[h] * etot + _tn(vb[h][rs], (kx[h][rs] * etot).astype(BF16))
    return [jnp.concatenate(o_chunks[h], axis=0) for h in heads], st


def _hgrn_tile_guarded(qq_all, kk_all, f_all, vv_all, b_all, st_all):
    tt = qq_all.shape[0]
    hd = G_EXP
    row = lax.broadcasted_iota(jnp.int32, (tt, hd), 0)
    rsub = row % HG_SUB
    sub = (lax.broadcasted_iota(jnp.int32, (HG_CHUNK, hd), 0)) // HG_SUB
    n_sub = HG_CHUNK // HG_SUB
    outs, states = [], []
    for h in range(G_HEADS):
        hs = slice(h * hd, (h + 1) * hd)
        qq = qq_all[:, hs]
        kk = kk_all[:, hs]
        ff = f_all[:, hs]
        bc_all = b_all[:, hs]
        vv = vv_all[:, hs]
        f_sh = [ff] + [pltpu.roll(ff, r, 0) for r in range(1, SUBLANES)]
        k_sh = [kk] + [pltpu.roll(kk, r, 0) for r in range(1, SUBLANES)]
        v_sh = [vv] + [pltpu.roll(vv, r, 0) for r in range(1, SUBLANES)]
        o_band = jnp.sum(qq * kk, axis=-1, keepdims=True) * vv
        decay = None
        for d in range(1, HG_SUB):
            fd = _shift_rows(f_sh, d - 1)
            decay = jnp.where(rsub >= d, fd if decay is None else decay * fd, 0.0)
            term = qq * _shift_rows(k_sh, d) * decay
            o_band = o_band + jnp.sum(term, axis=-1, keepdims=True) * _shift_rows(v_sh, d)
        st = st_all[h]
        o_chunks = []
        for c in range(tt // HG_CHUNK):
            r0 = c * HG_CHUNK
            bc = bc_all[r0:r0 + HG_CHUNK]
            qc = qq[r0:r0 + HG_CHUNK]
            kc = kk[r0:r0 + HG_CHUNK]
            vcb = vv[r0:r0 + HG_CHUNK].astype(BF16)
            refs = [bc[i * HG_SUB - 1:i * HG_SUB] for i in range(1, n_sub)]
            refrow = refs[-1]
            for i in range(n_sub - 2, 0, -1):
                refrow = jnp.where(sub == i, refs[i - 1], refrow)
            qp = qc * jnp.exp(bc - refrow)
            lhs = jnp.concatenate([jnp.where(sub == i, qp, 0.0) for i in range(1, n_sub)], axis=1)
            kcat = jnp.concatenate([jnp.where(sub < i, kc * jnp.exp(refs[i - 1] - bc), 0.0)
                                    for i in range(1, n_sub)], axis=1)
            a = _nt(lhs.astype(BF16), kcat.astype(BF16))
            qb = (qc * jnp.exp(bc)).astype(BF16)
            o_chunks.append(_mm(a.astype(BF16), vcb) + _nt(qb, st.astype(BF16)))
            btot = bc[HG_CHUNK - 1:HG_CHUNK]
            ke = (kc * jnp.exp(btot - bc)).astype(BF16)
            st = st * jnp.exp(btot) + _tn(vcb, ke)
        outs.append(o_band + jnp.concatenate(o_chunks, axis=0))
        states.append(st)
    return jnp.concatenate(outs, axis=1), jnp.stack(states)


def _even_prompt_kernel(x_ref, g_ref, win_ref, wout_ref, cos_ref, sin_ref, retg_ref, hgg_ref, lb_ref,
                        y_ref, sret_ref, shg_ref, dmat_ref, tri_ref, st_ref, ohg_ref, *, tt, gi):
    b_id = pl.program_id(0)
    t_id = pl.program_id(1)
    n_t = pl.num_programs(1)
    hd = 128

    @pl.when(jnp.logical_and(b_id == 0, t_id == 0))
    def _():
        ti = lax.broadcasted_iota(jnp.int32, (tt, tt), 0)
        si = lax.broadcasted_iota(jnp.int32, (tt, tt), 1)
        diff = (ti - si).astype(F32)
        for h in range(R_HEADS):
            dmat_ref[h] = jnp.where(diff >= 0.0, jnp.exp(_ret_log_gamma(h) * jnp.maximum(diff, 0.0)), 0.0)
        same_chunk = (ti // HG_CHUNK) == (si // HG_CHUNK)
        tri_ref[...] = jnp.where(jnp.logical_and(same_chunk, si <= ti), 1.0, 0.0).astype(BF16)

    @pl.when(t_id == 0)
    def _():
        sret_ref[...] = jnp.zeros_like(sret_ref)
        st_ref[...] = jnp.zeros_like(st_ref)

    x = x_ref[0]
    hn = _rms(x, g_ref[gi:gi + 1, :]).astype(BF16)
    cos = cos_ref[...]
    sin = sin_ref[...]
    rowf = lax.broadcasted_iota(jnp.int32, (tt, hd), 0).astype(F32)
    outs = []
    hg = []
    logf_parts = []
    st_old = st_ref[...]

    p = _mm(hn, win_ref[...])
    for i in range(R_HEADS + G_HEADS):
        h = i // 2
        pa, pb, pc, pd = _head_parts(p, i % 2, h)
        if i % 2 == 0:
            lg = _ret_log_gamma(h)
            q = _rotary(pa, cos, sin)
            k = _rotary(pb, cos, sin) * (R_DK ** -0.5)
            v = pc
            rg = pd
            s0 = sret_ref[0, h]
            qb = q.astype(BF16)
            vb = v.astype(BF16)
            inter = _mm(qb, s0.astype(BF16)) * jnp.exp(lg * (rowf + 1.0))
            scores = _nt(qb, k.astype(BF16)) * dmat_ref[h]
            intra = _mm(scores.astype(BF16), vb)
            kd = (k * jnp.exp(lg * (tt - 1.0 - rowf))).astype(BF16)
            sret_ref[0, h] = math.exp(lg * tt) * s0 + _tn(kd, vb)
            outs.append(_head_norm(inter + intra, retg_ref[h:h + 1, :], False) * _silu(rg))
        else:
            qq, kk, ff = _hgrn_gates(pa, pb, lb_ref[:, h * hd:(h + 1) * hd])
            hg.append((qq, kk, ff, pc, _sigmoid(pd)))
            logf_parts.extend(_split3(jnp.log(ff)))

    cs = _mm(tri_ref[...], jnp.concatenate(logf_parts, axis=1))
    b = [cs[:, (3 * h) * hd:(3 * h + 1) * hd] + cs[:, (3 * h + 1) * hd:(3 * h + 2) * hd]
         + cs[:, (3 * h + 2) * hd:(3 * h + 3) * hd] for h in range(G_HEADS)]
    b_all = jnp.concatenate(b, axis=1)
    o_fast, st_fast = _hgrn_tile_factorised([t[0] for t in hg], [t[1] for t in hg], [t[3] for t in hg], b,
                                            [st_old[h] for h in range(G_HEADS)])
    for h in range(G_HEADS):
        ohg_ref[:, h * hd:(h + 1) * hd] = o_fast[h]
        st_ref[h] = st_fast[h]

    @pl.when(jnp.min(b_all) < HG_SAFE_LOG_DECAY)
    def _():
        cat = lambda j: jnp.concatenate([t[j] for t in hg], axis=1)
        o_safe, st_safe = _hgrn_tile_guarded(cat(0), cat(1), cat(2), cat(3), b_all, st_old)
        ohg_ref[...] = o_safe
        st_ref[...] = st_safe

    for h in range(G_HEADS):
        outs.append(_head_norm(ohg_ref[:, h * hd:(h + 1) * hd], hgg_ref[h:h + 1, :], False) * hg[h][4])


    ycat = jnp.concatenate(outs, axis=1).astype(BF16)
    y_ref[0] = x + _mm(ycat, wout_ref[...])

    @pl.when(t_id == n_t - 1)
    def _():
        for h in range(G_HEADS):
            shg_ref[0, h] = st_ref[h].T


def _rope_tables(pos):
    half = R_DK // 2
    inv = ROPE_BASE ** (-jnp.arange(half, dtype=F32) / half)
    ang = pos.astype(F32)[:, None] * inv[None, :]
    cos = jnp.cos(ang)
    sin = jnp.sin(ang)
    return jnp.concatenate([cos, cos], axis=-1), jnp.concatenate([-sin, sin], axis=-1)


def _even_prompt(x, g_all, gi, win, wout, retg, hgg, lb, tt):
    bsz, seq, _ = x.shape
    cos, sin = _rope_tables(jnp.arange(seq, dtype=jnp.int32))
    state_spec = pl.BlockSpec((1, 4, 128, 128), lambda b, t: (b, 0, 0, 0))
    return pl.pallas_call(
        functools.partial(_even_prompt_kernel, tt=tt, gi=gi),
        grid=(bsz, seq // tt),
        in_specs=[pl.BlockSpec((1, tt, D_MODEL), lambda b, t: (b, t, 0)),
                  _const_spec(g_all.shape),
                  _const_spec((D_MODEL, EVEN_IN)),
                  _const_spec((EVEN_OUT, D_MODEL)),
                  pl.BlockSpec((tt, 128), lambda b, t: (t, 0)),
                  pl.BlockSpec((tt, 128), lambda b, t: (t, 0)),
                  _const_spec((4, 128)), _const_spec((4, 128)), _const_spec((1, 512))],
        out_specs=[pl.BlockSpec((1, tt, D_MODEL), lambda b, t: (b, t, 0)), state_spec, state_spec],
        out_shape=[jax.ShapeDtypeStruct(x.shape, F32),
                   jax.ShapeDtypeStruct((bsz, 4, 128, 128), F32),
                   jax.ShapeDtypeStruct((bsz, 4, 128, 128), F32)],
        scratch_shapes=[pltpu.VMEM((4, tt, tt), F32), pltpu.VMEM((tt, tt), BF16), pltpu.VMEM((4, 128, 128), F32),
                        pltpu.VMEM((tt, G_HEADS * G_DV), F32)],
        compiler_params=_params(("arbitrary", "arbitrary")),
        name="even_prompt",
    )(x, g_all, win, wout, cos, sin, retg, hgg, lb)


SB = 8


def _even_sample_kernel(x_ref, g_ref, win_ref, wout_ref, cos_ref, sin_ref, retg_ref, hgg_ref, lb_ref,
                        sret_in, shg_in, y_ref, sret_out, shg_out, p_ref, o_ref, *, gi):
    i = pl.program_id(0)
    n_i = pl.num_programs(0)
    hd = 128

    @pl.when(i == 0)
    def _():
        hn = _rms(x_ref[...], g_ref[gi:gi + 1, :]).astype(BF16)
        p_ref[...] = _mm(hn, win_ref[...])

    r0 = pl.multiple_of(i * SB, SB)
    p = p_ref[pl.ds(r0, SB), :]
    cos = cos_ref[...]
    sin = sin_ref[...]
    row = lax.broadcasted_iota(jnp.int32, (SB, hd), 0)
    outs = []
    for h in range(R_HEADS):
        gamma = math.exp(_ret_log_gamma(h))
        pa, pb, pc, rg = _head_parts(p, 0, h)
        q = _rotary(pa, cos, sin).astype(BF16)
        k = _rotary(pb, cos, sin) * (R_DK ** -0.5)
        vb = pc.astype(BF16)
        o = jnp.zeros((SB, hd), F32)
        for j in range(SB):
            kj = jnp.where(row == j, k, 0.0).astype(BF16)
            s_new = gamma * sret_in[j, h] + _tn(kj, vb)
            sret_out[j, h] = s_new
            o = jnp.where(row == j, _mm(q, s_new.astype(BF16)), o)
        outs.append(_head_norm(o, retg_ref[h:h + 1, :], False) * _silu(rg))
    for h in range(G_HEADS):
        pa, pb, pc, gg = _head_parts(p, 1, h)
        qq, kk, ff = _hgrn_gates(pa, pb, lb_ref[:, h * hd:(h + 1) * hd])
        vb = pc.astype(BF16)
        f_cols = jnp.concatenate([ff, jnp.zeros((hd - SB, hd), F32)], axis=0).T
        qb = qq.astype(BF16)
        o = jnp.zeros((SB, hd), F32)
        for j in range(SB):
            kj = jnp.where(row == j, kk, 0.0).astype(BF16)
            s_new = f_cols[:, j:j + 1] * shg_in[j, h] + _tn(kj, vb)
            shg_out[j, h] = s_new
            o = jnp.where(row == j, _mm(qb, s_new.astype(BF16)), o)
        outs.append(_head_norm(o, hgg_ref[h:h + 1, :], False) * _sigmoid(gg))
    o_ref[pl.ds(r0, SB), :] = jnp.concatenate(outs, axis=1)

    @pl.when(i == n_i - 1)
    def _():
        y_ref[...] = x_ref[...] + _mm(o_ref[...].astype(BF16), wout_ref[...])


def _even_sample(x, g_all, gi, win, wout, retg, hgg, lb, sret, shg):
    n = x.shape[0]
    cos, sin = _rope_tables(jnp.full((1,), PAST_LEN, dtype=jnp.int32))
    state_spec = pl.BlockSpec((SB, 4, 128, 128), lambda i: (i, 0, 0, 0))
    return pl.pallas_call(
        functools.partial(_even_sample_kernel, gi=gi),
        grid=(n // SB,),
        in_specs=[_const_spec((n, D_MODEL)),
                  _const_spec(g_all.shape),
                  _const_spec((D_MODEL, EVEN_IN)),
                  _const_spec((EVEN_OUT, D_MODEL)),
                  _const_spec((1, 128)), _const_spec((1, 128)),
                  _const_spec((4, 128)), _const_spec((4, 128)), _const_spec((1, 512)),
                  state_spec, state_spec],
        out_specs=[pl.BlockSpec((n, D_MODEL), lambda i: (0, 0)), state_spec, state_spec],
        out_shape=[jax.ShapeDtypeStruct((n, D_MODEL), F32),
                   jax.ShapeDtypeStruct(sret.shape, F32),
                   jax.ShapeDtypeStruct(shg.shape, F32)],
        scratch_shapes=[pltpu.VMEM((n, EVEN_IN), F32), pltpu.VMEM((n, EVEN_OUT), F32)],
        compiler_params=_params(("arbitrary",)),
        name="even_sample",
    )(x, g_all, win, wout, cos, sin, retg, hgg, lb, sret, shg)


def _block_diag(w):
    wr = w.reshape(M_INNER // BD, BD, QKV_BLOCK)
    tiled = jnp.tile(wr, (1, 1, BD // QKV_BLOCK))
    rb = lax.broadcasted_iota(jnp.int32, (BD, BD), 0) // QKV_BLOCK
    cb = lax.broadcasted_iota(jnp.int32, (BD, BD), 1) // QKV_BLOCK
    return jnp.where((rb == cb)[None], tiled, 0.0)


def _headwise(xb, w_ref, g0=0):
    return jnp.concatenate([_mm(xb[:, g * BD:(g + 1) * BD], w_ref[g0 + g]) for g in range(xb.shape[1] // BD)], axis=1)


def _gate_weights(w_ig, w_fg):
    w = jnp.concatenate([w_ig, w_fg], axis=1)
    w = jnp.pad(w, ((0, 0), (0, 128 - 2 * M_HEADS)))
    return w.reshape(3, M_INNER, 128)


def _mlstm_out(hs, xc, z, normg_ref, skip_ref, wdown_ref):
    hc = jnp.concatenate([_head_norm(hs[h], normg_ref[:, h * M_DH:(h + 1) * M_DH], True) for h in range(M_HEADS)],
                         axis=1)
    hc = hc + skip_ref[...] * xc
    return _mm((hc * _silu(z)).astype(BF16), wdown_ref[...])


def _odd_prompt_kernel(x_ref, g_ref, win_ref, convw_ref, convb_ref, wq_ref, wk_ref, wv_ref,
                       wgate_t_ref, bgate_t_ref, normg_ref, skip_ref, wdown_ref,
                       y_ref, c_ref, n_ref, m_out_ref, conv_out_ref,
                       carry_ref, m_ref, xc_ref, q_ref, k_ref, v_ref, *, tt, gi):
    t_id = pl.program_id(1)
    k_scale = M_DH ** -0.5

    @pl.when(t_id == 0)
    def _():
        c_ref[...] = jnp.zeros_like(c_ref)
        n_ref[...] = jnp.zeros_like(n_ref)
        m_ref[...] = jnp.zeros_like(m_ref)
        carry_ref[...] = jnp.zeros_like(carry_ref)

    x = x_ref[0]
    hn = _rms(x, g_ref[gi:gi + 1, :]).astype(BF16)

    gates_t = bgate_t_ref[...]
    row8 = lax.broadcasted_iota(jnp.int32, (SUBLANES, M_DH), 0)
    tiles = M_DH // BD
    xm_next = _mm(hn, win_ref[:, :M_DH])
    for h in range(M_HEADS):
        sl = slice(h * M_DH, (h + 1) * M_DH)
        xm = xm_next
        if h + 1 < M_HEADS:
            xm_next = _mm(hn, win_ref[:, (h + 1) * M_DH:(h + 2) * M_DH])
        carry = carry_ref[:, sl]
        conv = convb_ref[:, sl] + convw_ref[M_CONV - 1:M_CONV, sl] * xm
        for j in range(1, M_CONV):
            rolled = pltpu.roll(xm, j, 0)
            head = jnp.where(row8 < j, pltpu.roll(carry, j, 0), rolled[:SUBLANES])
            shifted = jnp.concatenate([head, rolled[SUBLANES:]], axis=0)
            conv = conv + convw_ref[M_CONV - 1 - j:M_CONV - j, sl] * shifted
        carry_ref[:, sl] = xm[tt - SUBLANES:, :]
        conv_out_ref[0, :, sl] = xm[tt - (M_CONV - 1):, :]
        xc = _silu(conv)
        xc_ref[:, sl] = xc
        xcb = xc.astype(BF16)
        qb = _headwise(xcb, wq_ref, h * tiles).astype(BF16)
        kb = _headwise(xcb, wk_ref, h * tiles).astype(BF16)
        vb = _headwise(xm.astype(BF16), wv_ref, h * tiles).astype(BF16)
        q_ref[:, sl] = qb
        k_ref[:, sl] = kb
        v_ref[:, sl] = vb
        gates_t = gates_t + (_nt(wgate_t_ref[0, :, sl], qb) + _nt(wgate_t_ref[1, :, sl], kb)
                             + _nt(wgate_t_ref[2, :, sl], vb))

    lane8 = lax.broadcasted_iota(jnp.int32, (SUBLANES, tt), 1)
    row8t = lax.broadcasted_iota(jnp.int32, (SUBLANES, tt), 0)
    brow_all = _log_sigmoid(gates_t)
    d = 1
    while d < tt:
        brow_all = brow_all + jnp.where(lane8 >= d, pltpu.roll(brow_all, d, 1), 0.0)
        d *= 2
    rows = jnp.where(row8t < M_HEADS, gates_t, brow_all)
    pad = jnp.zeros((128 - SUBLANES, 128), F32)
    cols = jnp.concatenate([jnp.concatenate([rows[:, j * 128:(j + 1) * 128], pad], axis=0).T
                            for j in range(tt // 128)], axis=0)

    ti = lax.broadcasted_iota(jnp.int32, (tt, tt), 0)
    si = lax.broadcasted_iota(jnp.int32, (tt, tt), 1)
    causal = si <= ti
    y = x
    for h in range(M_HEADS):
        sl = slice(h * M_DH, (h + 1) * M_DH)
        qhb = q_ref[:, sl]
        khb = k_ref[:, sl]
        vhb = v_ref[:, sl]
        ig_col = cols[:, h:h + 1]
        b_col = cols[:, M_HEADS + h:M_HEADS + h + 1]
        ig_row = gates_t[h:h + 1, :]
        b_row = brow_all[M_HEADS + h:M_HEADS + h + 1, :]
        m_prev = m_ref[h:h + 1, 0:1]
        c_prev = c_ref[0, h]
        n_prev = n_ref[0, h:h + 1, :]

        dlog = jnp.where(causal, b_col + (ig_row - b_row), -jnp.inf)
        inter_log = b_col + m_prev
        m_row = jnp.maximum(inter_log, jnp.max(dlog, axis=-1, keepdims=True))
        w_inter = jnp.exp(inter_log - m_row)
        qk = _nt(qhb, khb) * (jnp.exp(dlog - m_row) * k_scale)
        num = w_inter * _mm(qhb, c_prev.astype(BF16)) + _mm(qk.astype(BF16), vhb)
        qn = _nt(qhb, jnp.broadcast_to(n_prev, (SUBLANES, M_DH)).astype(BF16))[:, 0:1]
        den = w_inter * qn + jnp.sum(qk, axis=-1, keepdims=True)
        den = jnp.maximum(jnp.abs(den), jnp.exp(-m_row))
        hh = num * (1.0 / den)

        b_end = b_col[tt - 1:tt, :]
        s_log = b_end - b_col + ig_col
        m_new = jnp.maximum(b_end + m_prev, jnp.max(s_log, axis=0, keepdims=True))
        a = jnp.exp(b_end + m_prev - m_new)
        kw = khb.astype(F32) * (jnp.exp(s_log - m_new) * k_scale)
        c_ref[0, h] = a * c_prev + _tn(kw.astype(BF16), vhb)
        n_ref[0, h:h + 1, :] = a * n_prev + jnp.sum(kw, axis=0, keepdims=True)
        m_ref[h:h + 1, :] = jnp.broadcast_to(m_new, (1, 128))

        hc = _head_norm(hh, normg_ref[:, sl], True) + skip_ref[:, sl] * xc_ref[:, sl]
        z = _mm(hn, win_ref[:, M_INNER + h * M_DH:M_INNER + (h + 1) * M_DH])
        y = y + _mm((hc * _silu(z)).astype(BF16), wdown_ref[sl, :])

    y_ref[0] = y
    m_out_ref[0] = m_ref[...]


def _odd_prompt(x, g_all, gi, ml, tt):
    bsz, seq, _ = x.shape
    bt = jnp.broadcast_to(ml['bgate_col'], (SUBLANES, tt))
    return pl.pallas_call(
        functools.partial(_odd_prompt_kernel, tt=tt, gi=gi),
        grid=(bsz, seq // tt),
        in_specs=[pl.BlockSpec((1, tt, D_MODEL), lambda b, t: (b, t, 0)),
                  _const_spec(g_all.shape),
                  _const_spec((D_MODEL, 2 * M_INNER)),
                  _const_spec((M_CONV, M_INNER)),
                  _const_spec((1, M_INNER)),
                  _const_spec((M_INNER // BD, BD, BD)),
                  _const_spec((M_INNER // BD, BD, BD)),
                  _const_spec((M_INNER // BD, BD, BD)),
                  _const_spec((3, SUBLANES, M_INNER)),
                  _const_spec((SUBLANES, tt)),
                  _const_spec((1, M_INNER)),
                  _const_spec((1, M_INNER)),
                  _const_spec((M_INNER, D_MODEL))],
        out_specs=[pl.BlockSpec((1, tt, D_MODEL), lambda b, t: (b, t, 0)),
                   pl.BlockSpec((1, M_HEADS, M_DH, M_DH), lambda b, t: (b, 0, 0, 0)),
                   pl.BlockSpec((1, M_HEADS, M_DH), lambda b, t: (b, 0, 0)),
                   pl.BlockSpec((1, SUBLANES, 128), lambda b, t: (b, 0, 0)),
                   pl.BlockSpec((1, M_CONV - 1, M_INNER), lambda b, t: (b, 0, 0))],
        out_shape=[jax.ShapeDtypeStruct(x.shape, F32),
                   jax.ShapeDtypeStruct((bsz, M_HEADS, M_DH, M_DH), F32),
                   jax.ShapeDtypeStruct((bsz, M_HEADS, M_DH), F32),
                   jax.ShapeDtypeStruct((bsz, SUBLANES, 128), F32),
                   jax.ShapeDtypeStruct((bsz, M_CONV - 1, M_INNER), F32)],
        scratch_shapes=[pltpu.VMEM((SUBLANES, M_INNER), F32), pltpu.VMEM((SUBLANES, 128), F32),
                        pltpu.VMEM((tt, M_INNER), F32), pltpu.VMEM((tt, M_INNER), BF16),
                        pltpu.VMEM((tt, M_INNER), BF16), pltpu.VMEM((tt, M_INNER), BF16)],
        compiler_params=_params(("arbitrary", "arbitrary")),
        name="odd_prompt",
    )(x, g_all, ml['win'], ml['convw'], ml['convb'], ml['wq'], ml['wk'], ml['wv'],
      ml['wgate_t'], bt, ml['normg'], ml['skip'], ml['wdown'])


def _pick_row(ref, b):
    r0 = pl.multiple_of((b // SUBLANES) * SUBLANES, SUBLANES)
    blk = ref[pl.ds(r0, SUBLANES), :]
    row = lax.broadcasted_iota(jnp.int32, blk.shape, 0)
    return jnp.sum(jnp.where(row == b % SUBLANES, blk, 0.0), axis=0, keepdims=True)


def _put_row(ref, b, val):
    r0 = pl.multiple_of((b // SUBLANES) * SUBLANES, SUBLANES)
    blk = ref[pl.ds(r0, SUBLANES), :]
    row = lax.broadcasted_iota(jnp.int32, blk.shape, 0)
    ref[pl.ds(r0, SUBLANES), :] = jnp.where(row == b % SUBLANES, jnp.broadcast_to(val, blk.shape), blk)


def _odd_sample_kernel(x_ref, g_ref, win_ref, convw_ref, convb_ref, cv_ref,
                       wq_ref, wk_ref, wv_ref, wgate_ref, bgate_ref, normg_ref, skip_ref, wdown_ref,
                       m_in_ref, c_in, n_in,
                       y_ref, c_out, n_out, m_out_ref, cv_out_ref,
                       q_ref, k_ref, v_ref, gate_ref, xc_ref, z_ref, h_ref, *, gi):
    b = pl.program_id(0)
    n_b = pl.num_programs(0)

    @pl.when(b == 0)
    def _():
        hn = _rms(x_ref[...], g_ref[gi:gi + 1, :]).astype(BF16)
        p = _mm(hn, win_ref[...])
        xm = p[:, :M_INNER]
        z_ref[...] = p[:, M_INNER:]
        cv0 = cv_ref[:, :M_INNER]
        cv1 = cv_ref[:, M_INNER:2 * M_INNER]
        cv2 = cv_ref[:, 2 * M_INNER:]
        cv_out_ref[:, :M_INNER] = cv1
        cv_out_ref[:, M_INNER:2 * M_INNER] = cv2
        cv_out_ref[:, 2 * M_INNER:] = xm
        conv = (convb_ref[...] + convw_ref[3:4, :] * xm + convw_ref[2:3, :] * cv2
                + convw_ref[1:2, :] * cv1 + convw_ref[0:1, :] * cv0)
        xc = _silu(conv)
        xc_ref[...] = xc
        xcb = xc.astype(BF16)
        q = _headwise(xcb, wq_ref)
        k = _headwise(xcb, wk_ref)
        v = _headwise(xm.astype(BF16), wv_ref)
        q_ref[...] = q
        k_ref[...] = k * (M_DH ** -0.5)
        v_ref[...] = v
        gate_ref[...] = (_mm(q.astype(BF16), wgate_ref[0]) + _mm(k.astype(BF16), wgate_ref[1])
                         + _mm(v.astype(BF16), wgate_ref[2])) + bgate_ref[...]
        m_out_ref[...] = jnp.zeros_like(m_out_ref)
        h_ref[...] = jnp.zeros_like(h_ref)

    r0 = pl.multiple_of((b // SUBLANES) * SUBLANES, SUBLANES)
    row8 = lax.broadcasted_iota(jnp.int32, (SUBLANES, M_INNER), 0)
    sel = row8 == b % SUBLANES
    q8 = jnp.where(sel, q_ref[pl.ds(r0, SUBLANES), :], 0.0)
    k8 = jnp.where(sel, k_ref[pl.ds(r0, SUBLANES), :], 0.0)
    v8 = jnp.where(sel, v_ref[pl.ds(r0, SUBLANES), :], 0.0)
    q8b = q8.astype(BF16)
    k8b = k8.astype(BF16)
    v8b = v8.astype(BF16)
    k_row = jnp.sum(k8, axis=0, keepdims=True)
    q_row = jnp.sum(q8, axis=0, keepdims=True)
    gate = _pick_row(gate_ref, b)
    m_all = _pick_row(m_in_ref, b)
    lane = lax.broadcasted_iota(jnp.int32, (1, 128), 1)
    m_new_all = jnp.zeros((1, 128), F32)
    h_parts = []
    for h in range(M_HEADS):
        sl = slice(h * M_DH, (h + 1) * M_DH)
        ig = gate[:, h:h + 1]
        lf = _log_sigmoid(gate[:, M_HEADS + h:M_HEADS + h + 1])
        m_prev = m_all[:, h:h + 1]
        m_new = jnp.maximum(lf + m_prev, ig)
        a = jnp.exp(lf + m_prev - m_new)
        ws = jnp.exp(ig - m_new)
        c_new = a * c_in[0, h] + ws * _tn(k8b[:, sl], v8b[:, sl])
        c_out[0, h] = c_new
        n_new = a * n_in[0, h:h + 1, :] + ws * k_row[:, sl]
        n_out[0, h:h + 1, :] = n_new
        num = jnp.sum(_mm(q8b[:, sl], c_new.astype(BF16)), axis=0, keepdims=True)
        den = jnp.sum(q_row[:, sl] * n_new, axis=-1, keepdims=True)
        den = jnp.maximum(jnp.abs(den), jnp.exp(-m_new))
        h_parts.append(num / den)
        m_new_all = jnp.where(lane == h, m_new, m_new_all)
    _put_row(h_ref, b, jnp.concatenate(h_parts, axis=1))
    _put_row(m_out_ref, b, m_new_all)

    @pl.when(b == n_b - 1)
    def _():
        hfull = h_ref[...]
        hs = [hfull[:, h * M_DH:(h + 1) * M_DH] for h in range(M_HEADS)]
        y_ref[...] = x_ref[...] + _mlstm_out(hs, xc_ref[...], z_ref[...], normg_ref, skip_ref, wdown_ref)


def _odd_sample(x, g_all, gi, ml, c0, n0, m0, conv0):
    n = x.shape[0]
    m_pad = jnp.pad(m0, ((0, 0), (0, 128 - M_HEADS)))
    cw = (M_CONV - 1) * M_INNER
    full = lambda shape: pl.BlockSpec(shape, lambda b: (0,) * len(shape))
    outs = pl.pallas_call(
        functools.partial(_odd_sample_kernel, gi=gi),
        grid=(n,),
        in_specs=[_const_spec((n, D_MODEL)),
                  _const_spec(g_all.shape),
                  _const_spec((D_MODEL, 2 * M_INNER)),
                  _const_spec((M_CONV, M_INNER)),
                  _const_spec((1, M_INNER)),
                  _const_spec((n, cw)),
                  _const_spec((M_INNER // BD, BD, BD)),
                  _const_spec((M_INNER // BD, BD, BD)),
                  _const_spec((M_INNER // BD, BD, BD)),
                  _const_spec((3, M_INNER, 128)),
                  _const_spec((1, 128)),
                  _const_spec((1, M_INNER)),
                  _const_spec((1, M_INNER)),
                  _const_spec((M_INNER, D_MODEL)),
                  _const_spec((n, 128)),
                  pl.BlockSpec((1, M_HEADS, M_DH, M_DH), lambda b: (b, 0, 0, 0)),
                  pl.BlockSpec((1, M_HEADS, M_DH), lambda b: (b, 0, 0))],
        out_specs=[full((n, D_MODEL)),
                   pl.BlockSpec((1, M_HEADS, M_DH, M_DH), lambda b: (b, 0, 0, 0)),
                   pl.BlockSpec((1, M_HEADS, M_DH), lambda b: (b, 0, 0)),
                   full((n, 128)),
                   full((n, cw))],
        out_shape=[jax.ShapeDtypeStruct((n, D_MODEL), F32),
                   jax.ShapeDtypeStruct(c0.shape, F32),
                   jax.ShapeDtypeStruct(n0.shape, F32),
                   jax.ShapeDtypeStruct((n, 128), F32),
                   jax.ShapeDtypeStruct((n, cw), F32)],
        scratch_shapes=[pltpu.VMEM((n, M_INNER), F32), pltpu.VMEM((n, M_INNER), F32), pltpu.VMEM((n, M_INNER), F32),
                        pltpu.VMEM((n, 128), F32), pltpu.VMEM((n, M_INNER), F32), pltpu.VMEM((n, M_INNER), F32),
                        pltpu.VMEM((n, M_INNER), F32)],
        compiler_params=_params(("arbitrary",)),
        name="odd_sample",
    )(x, g_all, ml['win'], ml['convw'], ml['convb'], conv0.reshape(n, cw),
      ml['wq'], ml['wk'], ml['wv'], ml['wgate'], ml['bgate'], ml['normg'], ml['skip'], ml['wdown'],
      m_pad, c0, n0)
    y, c, nn, m_new, conv_new = outs
    return y, c, nn, m_new[:, :M_HEADS], conv_new.reshape(n, M_CONV - 1, M_INNER)


TM_FFN = 512
TT_EVEN = 256
TT_ODD = 512


def kernel(x_prompt, x_sample, state_ret, state_hgrn, state_mlstm_C, state_mlstm_n, state_mlstm_m, state_mlstm_conv,
           norm_g, final_norm_g, ffn_w_gate, ffn_w_up, ffn_w_down, ev_w_in, ev_w_out, ret_norm_g, hg_norm_g,
           hg_lb_logits, ml_w_in, ml_conv_w, ml_conv_b, ml_w_q, ml_w_k, ml_w_v, ml_w_ig, ml_b_ig, ml_w_fg,
           ml_b_fg, ml_norm_g, ml_skip, ml_w_down):
    bp, tp, _ = x_prompt.shape
    ns = x_sample.shape[0]

    g_all = norm_g.reshape(-1, D_MODEL)
    gfin = final_norm_g.reshape(1, D_MODEL)
    lb_all = jnp.cumsum(jax.nn.softmax(hg_lb_logits.astype(F32), axis=0), axis=0)
    lb = lb_all[0].reshape(1, G_HEADS * G_EXP)
    retg = ret_norm_g[0]
    hgg = hg_norm_g[0]
    wgate = _gate_weights(ml_w_ig[0], ml_w_fg[0])
    bgate = jnp.pad(jnp.concatenate([ml_b_ig[0], ml_b_fg[0]]), (0, 128 - 2 * M_HEADS)).reshape(1, 128)
    ml = {
        'convw': ml_conv_w[0],
        'convb': ml_conv_b[0].reshape(1, M_INNER),
        'wq': _block_diag(ml_w_q[0]).astype(BF16),
        'wk': _block_diag(ml_w_k[0]).astype(BF16),
        'wv': _block_diag(ml_w_v[0]).astype(BF16),
        'wgate': wgate.astype(BF16),
        'wgate_t': jnp.swapaxes(wgate[:, :, :SUBLANES], 1, 2).astype(BF16),
        'bgate': bgate,
        'bgate_col': bgate[0, :SUBLANES].reshape(SUBLANES, 1),
        'normg': ml_norm_g[0].reshape(1, M_INNER),
        'skip': ml_skip[0].reshape(1, M_INNER),
    }

    xp = x_prompt.reshape(bp * tp, D_MODEL)
    xs = x_sample.reshape(ns, D_MODEL)
    ffn_w = (ffn_w_gate, ffn_w_up, ffn_w_down)
    next_ffn = lambda layer, idx: [(w, (layer, idx)) for w in ffn_w]

    w00 = [w[0, 0].astype(BF16) for w in ffn_w]
    xp, xs, cast = _ffn(xp, xs, g_all, gfin, *w00, 0, TM_FFN,
                        casts=next_ffn(0, 1) + [(ev_w_in, (0,)), (ev_w_out, (0,))])
    w01, (ev_in, ev_out) = cast[:3], cast[3:]
    xp, ret_p, hg_p = _even_prompt(xp.reshape(bp, tp, D_MODEL), g_all, 1, ev_in, ev_out, retg, hgg, lb, TT_EVEN)
    xs, ret_s, hg_s = _even_sample(xs, g_all, 1, ev_in, ev_out, retg, hgg, lb, state_ret[:, 0], state_hgrn[:, 0])
    xp, xs, cast = _ffn(xp.reshape(bp * tp, D_MODEL), xs, g_all, gfin, *w01, 2, TM_FFN,
                        casts=next_ffn(1, 0) + [(ml_w_in, (0,)), (ml_w_down, (0,))])
    w10, (ml['win'], ml['wdown']) = cast[:3], cast[3:]
    xp, xs, w11 = _ffn(xp, xs, g_all, gfin, *w10, 3, TM_FFN, casts=next_ffn(1, 1))
    xp, c_p, n_p, m_p, conv_p = _odd_prompt(xp.reshape(bp, tp, D_MODEL), g_all, 4, ml, TT_ODD)
    xs, c_s, n_s, m_s, conv_s = _odd_sample(xs, g_all, 4, ml, state_mlstm_C[:, 0], state_mlstm_n[:, 0],
                                            state_mlstm_m[:, 0], state_mlstm_conv[:, 0])
    y_p, y_s, _ = _ffn(xp.reshape(bp * tp, D_MODEL), xs, g_all, gfin, *w11, 5, TM_FFN, final=True)

    return (y_p.reshape(bp, tp, D_MODEL), y_s.reshape(ns, 1, D_MODEL),
            ret_p[:, None], hg_p[:, None], c_p[:, None], n_p[:, None], m_p[:, None, :M_HEADS, 0], conv_p[:, None],
            ret_s[:, None], hg_s[:, None], c_s[:, None], n_s[:, None], m_s[:, None], conv_s[:, None])
```

```python
import functools
import math

import jax
import jax.numpy as jnp
from jax import lax
from jax.experimental import pallas as pl
from jax.experimental.pallas import tpu as pltpu

D_MODEL = 1024
PAST_LEN = 16384
R_HEADS = 4
R_DK = 128
R_DV = 128
G_HEADS = 4
G_EXP = 128
G_DV = 128
M_INNER = 2 * D_MODEL
M_HEADS = 4
M_DH = M_INNER // M_HEADS
M_CONV = 4
QKV_BLOCK = 4
D_FF = 2816
EPS = 1e-6
ROPE_BASE = 10000.0
EVEN_IN = 4096
EVEN_OUT = 1024

F32 = jnp.float32
BF16 = jnp.bfloat16

VMEM_LIMIT_BYTES = 56 * 1024 * 1024

HG_CHUNK = 64
HG_SUB = 16
HG_SAFE_LOG_DECAY = -60.0
MXU_TILE = 256
FFN_BOUNDS = (0, 6 * MXU_TILE, D_FF)
BD = MXU_TILE
SUBLANES = 8


def _nt(a, b):
    return lax.dot_general(a, b, (((1,), (1,)), ((), ())), preferred_element_type=F32)


def _tn(a, b):
    return lax.dot_general(a, b, (((0,), (0,)), ((), ())), preferred_element_type=F32)


def _mm(a, b):
    return jnp.dot(a, b, preferred_element_type=F32)


def _sigmoid(x):
    return 1.0 / (1.0 + jnp.exp(-x))


def _silu(x):
    return x * _sigmoid(x)


def _log_sigmoid(x):
    return jnp.minimum(x, 0.0) - jnp.log(1.0 + jnp.exp(-jnp.abs(x)))


def _rms(x, g):
    return x * lax.rsqrt(jnp.mean(x * x, axis=-1, keepdims=True) + EPS) * g


def _head_norm(x, g, center):
    if center:
        x = x - jnp.mean(x, axis=-1, keepdims=True)
    return x * lax.rsqrt(jnp.mean(x * x, axis=-1, keepdims=True) + EPS) * g


def _rotary(x, cos, sin_signed):
    return x * cos + pltpu.roll(x, 64, 1) * sin_signed


def _const_spec(shape):
    n = len(shape)
    return pl.BlockSpec(shape, lambda *_: (0,) * n, pipeline_mode=pl.Buffered(1))


def _params(sem):
    return pltpu.CompilerParams(dimension_semantics=sem, vmem_limit_bytes=VMEM_LIMIT_BYTES)


def _ffn_rows(x, g, wg_ref, wu_ref, wd_ref, gfin):
    h = _rms(x, g).astype(BF16)
    y = jnp.zeros_like(x)
    for lo, hi in zip(FFN_BOUNDS[:-1], FFN_BOUNDS[1:]):
        gt = _mm(h, wg_ref[:, lo:hi])
        ut = _mm(h, wu_ref[:, lo:hi])
        a = (_silu(gt) * ut).astype(BF16)
        y = y + _mm(a, wd_ref[lo:hi, :])
    out = x + 0.5 * y
    if gfin is not None:
        out = _rms(out, gfin)
    return out


def _ffn_kernel(xp_ref, xs_ref, g_ref, gfin_ref, wg_ref, wu_ref, wd_ref, *rest, gi, final, n_cast):
    cast_in = rest[:n_cast]
    op_ref, os_ref = rest[n_cast:n_cast + 2]
    cast_out = rest[n_cast + 2:]
    g = g_ref[gi:gi + 1, :]
    gfin = gfin_ref[...] if final else None
    op_ref[...] = _ffn_rows(xp_ref[...], g, wg_ref, wu_ref, wd_ref, gfin)
    for src, dst in zip(cast_in, cast_out):
        dst[...] = src[...].astype(BF16)

    @pl.when(pl.program_id(0) == pl.num_programs(0) - 1)
    def _():
        os_ref[...] = _ffn_rows(xs_ref[...], g, wg_ref, wu_ref, wd_ref, gfin)


BF16_ROWS = 16


def _cast_specs(arr, lead, steps):
    rows, cols = arr.shape[-2:]
    per = 1 if (rows // steps) % BF16_ROWS == 0 else 2
    br = rows * per // steps
    in_spec = pl.BlockSpec((None,) * len(lead) + (br, cols), lambda i: tuple(lead) + (i // per, 0))
    out_spec = pl.BlockSpec((br, cols), lambda i: (i // per, 0))
    return in_spec, out_spec, jax.ShapeDtypeStruct((rows, cols), BF16)


def _ffn(xp, xs, g_all, gfin, wg, wu, wd, gi, tm, final=False, casts=()):
    n = xp.shape[0]
    ns = xs.shape[0]
    steps = n // tm
    cast_specs = [_cast_specs(arr, lead, steps) for arr, lead in casts]
    outs = pl.pallas_call(
        functools.partial(_ffn_kernel, gi=gi, final=final, n_cast=len(casts)),
        grid=(steps,),
        in_specs=[pl.BlockSpec((tm, D_MODEL), lambda i: (i, 0)),
                  _const_spec((ns, D_MODEL)),
                  _const_spec(g_all.shape),
                  _const_spec((1, D_MODEL)),
                  _const_spec((D_MODEL, D_FF)), _const_spec((D_MODEL, D_FF)), _const_spec((D_FF, D_MODEL))]
                 + [c[0] for c in cast_specs],
        out_specs=[pl.BlockSpec((tm, D_MODEL), lambda i: (i, 0)),
                   pl.BlockSpec((ns, D_MODEL), lambda i: (0, 0))] + [c[1] for c in cast_specs],
        out_shape=[jax.ShapeDtypeStruct((n, D_MODEL), F32), jax.ShapeDtypeStruct((ns, D_MODEL), F32)]
                  + [c[2] for c in cast_specs],
        compiler_params=_params(("arbitrary",)),
        name="ffn_final" if final else "ffn",
    )(xp, xs, g_all, gfin, wg, wu, wd, *[arr for arr, _ in casts])
    return outs[0], outs[1], outs[2:]


def _ret_log_gamma(h):
    return math.log(1.0 - 2.0 ** (-5.0 - h))


def _head_parts(p, mixer, h):
    base = mixer * 4 * R_HEADS * R_DK
    return [p[:, base + (j * R_HEADS + h) * R_DK:base + (j * R_HEADS + h + 1) * R_DK] for j in range(4)]


def _hgrn_gates(gq, gf, lb):
    f = lb + (1.0 - lb) * _sigmoid(gf)
    kk = (1.0 - lb) * _sigmoid(-gf)
    qq = _silu(gq)
    return qq, kk, f


def _split3(x):
    hi = x.astype(BF16)
    r1 = x - hi.astype(F32)
    mid = r1.astype(BF16)
    lo = (r1 - mid.astype(F32)).astype(BF16)
    return hi, mid, lo


def _shift_rows(bases, d):
    base = bases[d % SUBLANES]
    full = (d // SUBLANES) * SUBLANES
    return pltpu.roll(base, full, 0) if full else base


def _hgrn_tile_factorised(qq, kk, vv, b, st):
    tt = qq[0].shape[0]
    heads = range(len(qq))
    qx = [(qq[h] * jnp.exp(b[h])).astype(BF16) for h in heads]
    kx = [kk[h] * jnp.exp(-b[h]) for h in heads]
    kxb = [kx[h].astype(BF16) for h in heads]
    vb = [vv[h].astype(BF16) for h in heads]
    ti = lax.broadcasted_iota(jnp.int32, (HG_CHUNK, HG_CHUNK), 0)
    si = lax.broadcasted_iota(jnp.int32, (HG_CHUNK, HG_CHUNK), 1)
    causal = si <= ti
    st = list(st)
    o_chunks = [[] for _ in heads]
    for c in range(tt // HG_CHUNK):
        rs = slice(c * HG_CHUNK, (c + 1) * HG_CHUNK)
        for h in heads:
            a = jnp.where(causal, _nt(qx[h][rs], kxb[h][rs]), 0.0)
            o_chunks[h].append(_mm(a.astype(BF16), vb[h][rs]) + _nt(qx[h][rs], st[h].astype(BF16)))
            etot = jnp.exp(b[h][(c + 1) * HG_CHUNK - 1:(c + 1) * HG_CHUNK])
            st[h] = st[h] * etot + _tn(vb[h][rs], (kx[h][rs] * etot).astype(BF16))
    return [jnp.concatenate(o_chunks[h], axis=0) for h in heads], st


def _hgrn_tile_guarded(qq_all, kk_all, f_all, vv_all, b_all, st_all):
    tt = qq_all.shape[0]
    hd = G_EXP
    row = lax.broadcasted_iota(jnp.int32, (tt, hd), 0)
    rsub = row % HG_SUB
    sub = (lax.broadcasted_iota(jnp.int32, (HG_CHUNK, hd), 0)) // HG_SUB
    n_sub = HG_CHUNK // HG_SUB
    outs, states = [], []
    for h in range(G_HEADS):
        hs = slice(h * hd, (h + 1) * hd)
        qq = qq_all[:, hs]
        kk = kk_all[:, hs]
        ff = f_all[:, hs]
        bc_all = b_all[:, hs]
        vv = vv_all[:, hs]
        f_sh = [ff] + [pltpu.roll(ff, r, 0) for r in range(1, SUBLANES)]
        k_sh = [kk] + [pltpu.roll(kk, r, 0) for r in range(1, SUBLANES)]
        v_sh = [vv] + [pltpu.roll(vv, r, 0) for r in range(1, SUBLANES)]
        o_band = jnp.sum(qq * kk, axis=-1, keepdims=True) * vv
        decay = None
        for d in range(1, HG_SUB):
            fd = _shift_rows(f_sh, d - 1)
            decay = jnp.where(rsub >= d, fd if decay is None else decay * fd, 0.0)
            term = qq * _shift_rows(k_sh, d) * decay
            o_band = o_band + jnp.sum(term, axis=-1, keepdims=True) * _shift_rows(v_sh, d)
        st = st_all[h]
        o_chunks = []
        for c in range(tt // HG_CHUNK):
            r0 = c * HG_CHUNK
            bc = bc_all[r0:r0 + HG_CHUNK]
            qc = qq[r0:r0 + HG_CHUNK]
            kc = kk[r0:r0 + HG_CHUNK]
            vcb = vv[r0:r0 + HG_CHUNK].astype(BF16)
            refs = [bc[i * HG_SUB - 1:i * HG_SUB] for i in range(1, n_sub)]
            refrow = refs[-1]
            for i in range(n_sub - 2, 0, -1):
                refrow = jnp.where(sub == i, refs[i - 1], refrow)
            qp = qc * jnp.exp(bc - refrow)
            lhs = jnp.concatenate([jnp.where(sub == i, qp, 0.0) for i in range(1, n_sub)], axis=1)
            kcat = jnp.concatenate([jnp.where(sub < i, kc * jnp.exp(refs[i - 1] - bc), 0.0)
                                    for i in range(1, n_sub)], axis=1)
            a = _nt(lhs.astype(BF16), kcat.astype(BF16))
            qb = (qc * jnp.exp(bc)).astype(BF16)
            o_chunks.append(_mm(a.astype(BF16), vcb) + _nt(qb, st.astype(BF16)))
            btot = bc[HG_CHUNK - 1:HG_CHUNK]
            ke = (kc * jnp.exp(btot - bc)).astype(BF16)
            st = st * jnp.exp(btot) + _tn(vcb, ke)
        outs.append(o_band + jnp.concatenate(o_chunks, axis=0))
        states.append(st)
    return jnp.concatenate(outs, axis=1), jnp.stack(states)


def _even_prompt_kernel(x_ref, g_ref, win_ref, wout_ref, cos_ref, sin_ref, retg_ref, hgg_ref, lb_ref,
                        y_ref, sret_ref, shg_ref, dmat_ref, tri_ref, st_ref, ohg_ref, *, tt, gi):
    b_id = pl.program_id(0)
    t_id = pl.program_id(1)
    n_t = pl.num_programs(1)
    hd = 128

    @pl.when(jnp.logical_and(b_id == 0, t_id == 0))
    def _():
        ti = lax.broadcasted_iota(jnp.int32, (tt, tt), 0)
        si = lax.broadcasted_iota(jnp.int32, (tt, tt), 1)
        diff = (ti - si).astype(F32)
        for h in range(R_HEADS):
            dmat_ref[h] = jnp.where(diff >= 0.0, jnp.exp(_ret_log_gamma(h) * jnp.maximum(diff, 0.0)), 0.0)
        same_chunk = (ti // HG_CHUNK) == (si // HG_CHUNK)
        tri_ref[...] = jnp.where(jnp.logical_and(same_chunk, si <= ti), 1.0, 0.0).astype(BF16)

    @pl.when(t_id == 0)
    def _():
        sret_ref[...] = jnp.zeros_like(sret_ref)
        st_ref[...] = jnp.zeros_like(st_ref)

    x = x_ref[0]
    hn = _rms(x, g_ref[gi:gi + 1, :]).astype(BF16)
    cos = cos_ref[...]
    sin = sin_ref[...]
    rowf = lax.broadcasted_iota(jnp.int32, (tt, hd), 0).astype(F32)
    outs = []
    hg = []
    logf_parts = []
    st_old = st_ref[...]

    p = _mm(hn, win_ref[...])
    for i in range(R_HEADS + G_HEADS):
        h = i // 2
        pa, pb, pc, pd = _head_parts(p, i % 2, h)
        if i % 2 == 0:
            lg = _ret_log_gamma(h)
            q = _rotary(pa, cos, sin)
            k = _rotary(pb, cos, sin) * (R_DK ** -0.5)
            v = pc
            rg = pd
            s0 = sret_ref[0, h]
            qb = q.astype(BF16)
            vb = v.astype(BF16)
            inter = _mm(qb, s0.astype(BF16)) * jnp.exp(lg * (rowf + 1.0))
            scores = _nt(qb, k.astype(BF16)) * dmat_ref[h]
            intra = _mm(scores.astype(BF16), vb)
            kd = (k * jnp.exp(lg * (tt - 1.0 - rowf))).astype(BF16)
            sret_ref[0, h] = math.exp(lg * tt) * s0 + _tn(kd, vb)
            outs.append(_head_norm(inter + intra, retg_ref[h:h + 1, :], False) * _silu(rg))
        else:
            qq, kk, ff = _hgrn_gates(pa, pb, lb_ref[:, h * hd:(h + 1) * hd])
            hg.append((qq, kk, ff, pc, _sigmoid(pd)))
            logf_parts.extend(_split3(jnp.log(ff)))

    cs = _mm(tri_ref[...], jnp.concatenate(logf_parts, axis=1))
    b = [cs[:, (3 * h) * hd:(3 * h + 1) * hd] + cs[:, (3 * h + 1) * hd:(3 * h + 2) * hd]
         + cs[:, (3 * h + 2) * hd:(3 * h + 3) * hd] for h in range(G_HEADS)]
    b_all = jnp.concatenate(b, axis=1)
    o_fast, st_fast = _hgrn_tile_factorised([t[0] for t in hg], [t[1] for t in hg], [t[3] for t in hg], b,
                                            [st_old[h] for h in range(G_HEADS)])
    for h in range(G_HEADS):
        ohg_ref[:, h * hd:(h + 1) * hd] = o_fast[h]
        st_ref[h] = st_fast[h]

    @pl.when(jnp.min(b_all) < HG_SAFE_LOG_DECAY)
    def _():
        cat = lambda j: jnp.concatenate([t[j] for t in hg], axis=1)
        o_safe, st_safe = _hgrn_tile_guarded(cat(0), cat(1), cat(2), cat(3), b_all, st_old)
        ohg_ref[...] = o_safe
        st_ref[...] = st_safe

    for h in range(G_HEADS):
        outs.append(_head_norm(ohg_ref[:, h * hd:(h + 1) * hd], hgg_ref[h:h + 1, :], False) * hg[h][4])


    ycat = jnp.concatenate(outs, axis=1).astype(BF16)
    y_ref[0] = x + _mm(ycat, wout_ref[...])

    @pl.when(t_id == n_t - 1)
    def _():
        for h in range(G_HEADS):
            shg_ref[0, h] = st_ref[h].T


def _rope_tables(pos):
    half = R_DK // 2
    inv = ROPE_BASE ** (-jnp.arange(half, dtype=F32) / half)
    ang = pos.astype(F32)[:, None] * inv[None, :]
    cos = jnp.cos(ang)
    sin = jnp.sin(ang)
    return jnp.concatenate([cos, cos], axis=-1), jnp.concatenate([-sin, sin], axis=-1)


def _even_prompt(x, g_all, gi, win, wout, retg, hgg, lb, tt):
    bsz, seq, _ = x.shape
    cos, sin = _rope_tables(jnp.arange(seq, dtype=jnp.int32))
    state_spec = pl.BlockSpec((1, 4, 128, 128), lambda b, t: (b, 0, 0, 0))
    return pl.pallas_call(
        functools.partial(_even_prompt_kernel, tt=tt, gi=gi),
        grid=(bsz, seq // tt),
        in_specs=[pl.BlockSpec((1, tt, D_MODEL), lambda b, t: (b, t, 0)),
                  _const_spec(g_all.shape),
                  _const_spec((D_MODEL, EVEN_IN)),
                  _const_spec((EVEN_OUT, D_MODEL)),
                  pl.BlockSpec((tt, 128), lambda b, t: (t, 0)),
                  pl.BlockSpec((tt, 128), lambda b, t: (t, 0)),
                  _const_spec((4, 128)), _const_spec((4, 128)), _const_spec((1, 512))],
        out_specs=[pl.BlockSpec((1, tt, D_MODEL), lambda b, t: (b, t, 0)), state_spec, state_spec],
        out_shape=[jax.ShapeDtypeStruct(x.shape, F32),
                   jax.ShapeDtypeStruct((bsz, 4, 128, 128), F32),
                   jax.ShapeDtypeStruct((bsz, 4, 128, 128), F32)],
        scratch_shapes=[pltpu.VMEM((4, tt, tt), F32), pltpu.VMEM((tt, tt), BF16), pltpu.VMEM((4, 128, 128), F32),
                        pltpu.VMEM((tt, G_HEADS * G_DV), F32)],
        compiler_params=_params(("arbitrary", "arbitrary")),
        name="even_prompt",
    )(x, g_all, win, wout, cos, sin, retg, hgg, lb)


SB = 8


def _even_sample_kernel(x_ref, g_ref, win_ref, wout_ref, cos_ref, sin_ref, retg_ref, hgg_ref, lb_ref,
                        sret_in, shg_in, y_ref, sret_out, shg_out, p_ref, o_ref, *, gi):
    i = pl.program_id(0)
    n_i = pl.num_programs(0)
    hd = 128

    @pl.when(i == 0)
    def _():
        hn = _rms(x_ref[...], g_ref[gi:gi + 1, :]).astype(BF16)
        p_ref[...] = _mm(hn, win_ref[...])

    r0 = pl.multiple_of(i * SB, SB)
    p = p_ref[pl.ds(r0, SB), :]
    cos = cos_ref[...]
    sin = sin_ref[...]
    row = lax.broadcasted_iota(jnp.int32, (SB, hd), 0)
    outs = []
    for h in range(R_HEADS):
        gamma = math.exp(_ret_log_gamma(h))
        pa, pb, pc, rg = _head_parts(p, 0, h)
        q = _rotary(pa, cos, sin).astype(BF16)
        k = _rotary(pb, cos, sin) * (R_DK ** -0.5)
        vb = pc.astype(BF16)
        o = jnp.zeros((SB, hd), F32)
        for j in range(SB):
            kj = jnp.where(row == j, k, 0.0).astype(BF16)
            s_new = gamma * sret_in[j, h] + _tn(kj, vb)
            sret_out[j, h] = s_new
            o = jnp.where(row == j, _mm(q, s_new.astype(BF16)), o)
        outs.append(_head_norm(o, retg_ref[h:h + 1, :], False) * _silu(rg))
    for h in range(G_HEADS):
        pa, pb, pc, gg = _head_parts(p, 1, h)
        qq, kk, ff = _hgrn_gates(pa, pb, lb_ref[:, h * hd:(h + 1) * hd])
        vb = pc.astype(BF16)
        f_cols = jnp.concatenate([ff, jnp.zeros((hd - SB, hd), F32)], axis=0).T
        qb = qq.astype(BF16)
        o = jnp.zeros((SB, hd), F32)
        for j in range(SB):
            kj = jnp.where(row == j, kk, 0.0).astype(BF16)
            s_new = f_cols[:, j:j + 1] * shg_in[j, h] + _tn(kj, vb)
            shg_out[j, h] = s_new
            o = jnp.where(row == j, _mm(qb, s_new.astype(BF16)), o)
        outs.append(_head_norm(o, hgg_ref[h:h + 1, :], False) * _sigmoid(gg))
    o_ref[pl.ds(r0, SB), :] = jnp.concatenate(outs, axis=1)

    @pl.when(i == n_i - 1)
    def _():
        y_ref[...] = x_ref[...] + _mm(o_ref[...].astype(BF16), wout_ref[...])


def _even_sample(x, g_all, gi, win, wout, retg, hgg, lb, sret, shg):
    n = x.shape[0]
    cos, sin = _rope_tables(jnp.full((1,), PAST_LEN, dtype=jnp.int32))
    state_spec = pl.BlockSpec((SB, 4, 128, 128), lambda i: (i, 0, 0, 0))
    return pl.pallas_call(
        functools.partial(_even_sample_kernel, gi=gi),
        grid=(n // SB,),
        in_specs=[_const_spec((n, D_MODEL)),
                  _const_spec(g_all.shape),
                  _const_spec((D_MODEL, EVEN_IN)),
                  _const_spec((EVEN_OUT, D_MODEL)),
                  _const_spec((1, 128)), _const_spec((1, 128)),
                  _const_spec((4, 128)), _const_spec((4, 128)), _const_spec((1, 512)),
                  state_spec, state_spec],
        out_specs=[pl.BlockSpec((n, D_MODEL), lambda i: (0, 0)), state_spec, state_spec],
        out_shape=[jax.ShapeDtypeStruct((n, D_MODEL), F32),
                   jax.ShapeDtypeStruct(sret.shape, F32),
                   jax.ShapeDtypeStruct(shg.shape, F32)],
        scratch_shapes=[pltpu.VMEM((n, EVEN_IN), F32), pltpu.VMEM((n, EVEN_OUT), F32)],
        compiler_params=_params(("arbitrary",)),
        name="even_sample",
    )(x, g_all, win, wout, cos, sin, retg, hgg, lb, sret, shg)


def _block_diag(w):
    wr = w.reshape(M_INNER // BD, BD, QKV_BLOCK)
    tiled = jnp.tile(wr, (1, 1, BD // QKV_BLOCK))
    rb = lax.broadcasted_iota(jnp.int32, (BD, BD), 0) // QKV_BLOCK
    cb = lax.broadcasted_iota(jnp.int32, (BD, BD), 1) // QKV_BLOCK
    return jnp.where((rb == cb)[None], tiled, 0.0)


def _headwise(xb, w_ref, g0=0):
    return jnp.concatenate([_mm(xb[:, g * BD:(g + 1) * BD], w_ref[g0 + g]) for g in range(xb.shape[1] // BD)], axis=1)


def _gate_weights(w_ig, w_fg):
    w = jnp.concatenate([w_ig, w_fg], axis=1)
    w = jnp.pad(w, ((0, 0), (0, 128 - 2 * M_HEADS)))
    return w.reshape(3, M_INNER, 128)


def _mlstm_out(hs, xc, z, normg_ref, skip_ref, wdown_ref):
    hc = jnp.concatenate([_head_norm(hs[h], normg_ref[:, h * M_DH:(h + 1) * M_DH], True) for h in range(M_HEADS)],
                         axis=1)
    hc = hc + skip_ref[...] * xc
    return _mm((hc * _silu(z)).astype(BF16), wdown_ref[...])


def _odd_prompt_kernel(x_ref, g_ref, win_ref, convw_ref, convb_ref, wq_ref, wk_ref, wv_ref,
                       wgate_t_ref, bgate_t_ref, normg_ref, skip_ref, wdown_ref,
                       y_ref, c_ref, n_ref, m_out_ref, conv_out_ref,
                       carry_ref, m_ref, xc_ref, q_ref, k_ref, v_ref, *, tt, gi):
    t_id = pl.program_id(1)
    k_scale = M_DH ** -0.5

    @pl.when(t_id == 0)
    def _():
        c_ref[...] = jnp.zeros_like(c_ref)
        n_ref[...] = jnp.zeros_like(n_ref)
        m_ref[...] = jnp.zeros_like(m_ref)
        carry_ref[...] = jnp.zeros_like(carry_ref)

    x = x_ref[0]
    hn = _rms(x, g_ref[gi:gi + 1, :]).astype(BF16)

    gates_t = bgate_t_ref[...]
    row8 = lax.broadcasted_iota(jnp.int32, (SUBLANES, M_DH), 0)
    tiles = M_DH // BD
    xm_next = _mm(hn, win_ref[:, :M_DH])
    for h in range(M_HEADS):
        sl = slice(h * M_DH, (h + 1) * M_DH)
        xm = xm_next
        if h + 1 < M_HEADS:
            xm_next = _mm(hn, win_ref[:, (h + 1) * M_DH:(h + 2) * M_DH])
        carry = carry_ref[:, sl]
        conv = convb_ref[:, sl] + convw_ref[M_CONV - 1:M_CONV, sl] * xm
        for j in range(1, M_CONV):
            rolled = pltpu.roll(xm, j, 0)
            head = jnp.where(row8 < j, pltpu.roll(carry, j, 0), rolled[:SUBLANES])
            shifted = jnp.concatenate([head, rolled[SUBLANES:]], axis=0)
            conv = conv + convw_ref[M_CONV - 1 - j:M_CONV - j, sl] * shifted
        carry_ref[:, sl] = xm[tt - SUBLANES:, :]
        conv_out_ref[0, :, sl] = xm[tt - (M_CONV - 1):, :]
        xc = _silu(conv)
        xc_ref[:, sl] = xc
        xcb = xc.astype(BF16)
        qb = _headwise(xcb, wq_ref, h * tiles).astype(BF16)
        kb = _headwise(xcb, wk_ref, h * tiles).astype(BF16)
        vb = _headwise(xm.astype(BF16), wv_ref, h * tiles).astype(BF16)
        q_ref[:, sl] = qb
        k_ref[:, sl] = kb
        v_ref[:, sl] = vb
        gates_t = gates_t + (_nt(wgate_t_ref[0, :, sl], qb) + _nt(wgate_t_ref[1, :, sl], kb)
                             + _nt(wgate_t_ref[2, :, sl], vb))

    lane8 = lax.broadcasted_iota(jnp.int32, (SUBLANES, tt), 1)
    row8t = lax.broadcasted_iota(jnp.int32, (SUBLANES, tt), 0)
    brow_all = _log_sigmoid(gates_t)
    d = 1
    while d < tt:
        brow_all = brow_all + jnp.where(lane8 >= d, pltpu.roll(brow_all, d, 1), 0.0)
        d *= 2
    rows = jnp.where(row8t < M_HEADS, gates_t, brow_all)
    pad = jnp.zeros((128 - SUBLANES, 128), F32)
    cols = jnp.concatenate([jnp.concatenate([rows[:, j * 128:(j + 1) * 128], pad], axis=0).T
                            for j in range(tt // 128)], axis=0)

    ti = lax.broadcasted_iota(jnp.int32, (tt, tt), 0)
    si = lax.broadcasted_iota(jnp.int32, (tt, tt), 1)
    causal = si <= ti
    y = x
    for h in range(M_HEADS):
        sl = slice(h * M_DH, (h + 1) * M_DH)
        qhb = q_ref[:, sl]
        khb = k_ref[:, sl]
        vhb = v_ref[:, sl]
        ig_col = cols[:, h:h + 1]
        b_col = cols[:, M_HEADS + h:M_HEADS + h + 1]
        ig_row = gates_t[h:h + 1, :]
        b_row = brow_all[M_HEADS + h:M_HEADS + h + 1, :]
        m_prev = m_ref[h:h + 1, 0:1]
        c_prev = c_ref[0, h]
        n_prev = n_ref[0, h:h + 1, :]

        dlog = jnp.where(causal, b_col + (ig_row - b_row), -jnp.inf)
        inter_log = b_col + m_prev
        m_row = jnp.maximum(inter_log, jnp.max(dlog, axis=-1, keepdims=True))
        w_inter = jnp.exp(inter_log - m_row)
        qk = _nt(qhb, khb) * (jnp.exp(dlog - m_row) * k_scale)
        num = w_inter * _mm(qhb, c_prev.astype(BF16)) + _mm(qk.astype(BF16), vhb)
        qn = _nt(qhb, jnp.broadcast_to(n_prev, (SUBLANES, M_DH)).astype(BF16))[:, 0:1]
        den = w_inter * qn + jnp.sum(qk, axis=-1, keepdims=True)
        den = jnp.maximum(jnp.abs(den), jnp.exp(-m_row))
        hh = num * (1.0 / den)

        b_end = b_col[tt - 1:tt, :]
        s_log = b_end - b_col + ig_col
        m_new = jnp.maximum(b_end + m_prev, jnp.max(s_log, axis=0, keepdims=True))
        a = jnp.exp(b_end + m_prev - m_new)
        kw = khb.astype(F32) * (jnp.exp(s_log - m_new) * k_scale)
        c_ref[0, h] = a * c_prev + _tn(kw.astype(BF16), vhb)
        n_ref[0, h:h + 1, :] = a * n_prev + jnp.sum(kw, axis=0, keepdims=True)
        m_ref[h:h + 1, :] = jnp.broadcast_to(m_new, (1, 128))

        hc = _head_norm(hh, normg_ref[:, sl], True) + skip_ref[:, sl] * xc_ref[:, sl]
        z = _mm(hn, win_ref[:, M_INNER + h * M_DH:M_INNER + (h + 1) * M_DH])
        y = y + _mm((hc * _silu(z)).astype(BF16), wdown_ref[sl, :])

    y_ref[0] = y
    m_out_ref[0] = m_ref[...]


def _odd_prompt(x, g_all, gi, ml, tt):
    bsz, seq, _ = x.shape
    bt = jnp.broadcast_to(ml['bgate_col'], (SUBLANES, tt))
    return pl.pallas_call(
        functools.partial(_odd_prompt_kernel, tt=tt, gi=gi),
        grid=(bsz, seq // tt),
        in_specs=[pl.BlockSpec((1, tt, D_MODEL), lambda b, t: (b, t, 0)),
                  _const_spec(g_all.shape),
                  _const_spec((D_MODEL, 2 * M_INNER)),
                  _const_spec((M_CONV, M_INNER)),
                  _const_spec((1, M_INNER)),
                  _const_spec((M_INNER // BD, BD, BD)),
                  _const_spec((M_INNER // BD, BD, BD)),
                  _const_spec((M_INNER // BD, BD, BD)),
                  _const_spec((3, SUBLANES, M_INNER)),
                  _const_spec((SUBLANES, tt)),
                  _const_spec((1, M_INNER)),
                  _const_spec((1, M_INNER)),
                  _const_spec((M_INNER, D_MODEL))],
        out_specs=[pl.BlockSpec((1, tt, D_MODEL), lambda b, t: (b, t, 0)),
                   pl.BlockSpec((1, M_HEADS, M_DH, M_DH), lambda b, t: (b, 0, 0, 0)),
                   pl.BlockSpec((1, M_HEADS, M_DH), lambda b, t: (b, 0, 0)),
                   pl.BlockSpec((1, SUBLANES, 128), lambda b, t: (b, 0, 0)),
                   pl.BlockSpec((1, M_CONV - 1, M_INNER), lambda b, t: (b, 0, 0))],
        out_shape=[jax.ShapeDtypeStruct(x.shape, F32),
                   jax.ShapeDtypeStruct((bsz, M_HEADS, M_DH, M_DH), F32),
                   jax.ShapeDtypeStruct((bsz, M_HEADS, M_DH), F32),
                   jax.ShapeDtypeStruct((bsz, SUBLANES, 128), F32),
                   jax.ShapeDtypeStruct((bsz, M_CONV - 1, M_INNER), F32)],
        scratch_shapes=[pltpu.VMEM((SUBLANES, M_INNER), F32), pltpu.VMEM((SUBLANES, 128), F32),
                        pltpu.VMEM((tt, M_INNER), F32), pltpu.VMEM((tt, M_INNER), BF16),
                        pltpu.VMEM((tt, M_INNER), BF16), pltpu.VMEM((tt, M_INNER), BF16)],
        compiler_params=_params(("arbitrary", "arbitrary")),
        name="odd_prompt",
    )(x, g_all, ml['win'], ml['convw'], ml['convb'], ml['wq'], ml['wk'], ml['wv'],
      ml['wgate_t'], bt, ml['normg'], ml['skip'], ml['wdown'])


def _pick_row(ref, b):
    r0 = pl.multiple_of((b // SUBLANES) * SUBLANES, SUBLANES)
    blk = ref[pl.ds(r0, SUBLANES), :]
    row = lax.broadcasted_iota(jnp.int32, blk.shape, 0)
    return jnp.sum(jnp.where(row == b % SUBLANES, blk, 0.0), axis=0, keepdims=True)


def _put_row(ref, b, val):
    r0 = pl.multiple_of((b // SUBLANES) * SUBLANES, SUBLANES)
    blk = ref[pl.ds(r0, SUBLANES), :]
    row = lax.broadcasted_iota(jnp.int32, blk.shape, 0)
    ref[pl.ds(r0, SUBLANES), :] = jnp.where(row == b % SUBLANES, jnp.broadcast_to(val, blk.shape), blk)


def _odd_sample_kernel(x_ref, g_ref, win_ref, convw_ref, convb_ref, cv_ref,
                       wq_ref, wk_ref, wv_ref, wgate_ref, bgate_ref, normg_ref, skip_ref, wdown_ref,
                       m_in_ref, c_in, n_in,
                       y_ref, c_out, n_out, m_out_ref, cv_out_ref,
                       q_ref, k_ref, v_ref, gate_ref, xc_ref, z_ref, h_ref, *, gi):
    b = pl.program_id(0)
    n_b = pl.num_programs(0)

    @pl.when(b == 0)
    def _():
        hn = _rms(x_ref[...], g_ref[gi:gi + 1, :]).astype(BF16)
        p = _mm(hn, win_ref[...])
        xm = p[:, :M_INNER]
        z_ref[...] = p[:, M_INNER:]
        cv0 = cv_ref[:, :M_INNER]
        cv1 = cv_ref[:, M_INNER:2 * M_INNER]
        cv2 = cv_ref[:, 2 * M_INNER:]
        cv_out_ref[:, :M_INNER] = cv1
        cv_out_ref[:, M_INNER:2 * M_INNER] = cv2
        cv_out_ref[:, 2 * M_INNER:] = xm
        conv = (convb_ref[...] + convw_ref[3:4, :] * xm + convw_ref[2:3, :] * cv2
                + convw_ref[1:2, :] * cv1 + convw_ref[0:1, :] * cv0)
        xc = _silu(conv)
        xc_ref[...] = xc
        xcb = xc.astype(BF16)
        q = _headwise(xcb, wq_ref)
        k = _headwise(xcb, wk_ref)
        v = _headwise(xm.astype(BF16), wv_ref)
        q_ref[...] = q
        k_ref[...] = k * (M_DH ** -0.5)
        v_ref[...] = v
        gate_ref[...] = (_mm(q.astype(BF16), wgate_ref[0]) + _mm(k.astype(BF16), wgate_ref[1])
                         + _mm(v.astype(BF16), wgate_ref[2])) + bgate_ref[...]
        m_out_ref[...] = jnp.zeros_like(m_out_ref)
        h_ref[...] = jnp.zeros_like(h_ref)

    r0 = pl.multiple_of((b // SUBLANES) * SUBLANES, SUBLANES)
    row8 = lax.broadcasted_iota(jnp.int32, (SUBLANES, M_INNER), 0)
    sel = row8 == b % SUBLANES
    q8 = jnp.where(sel, q_ref[pl.ds(r0, SUBLANES), :], 0.0)
    k8 = jnp.where(sel, k_ref[pl.ds(r0, SUBLANES), :], 0.0)
    v8 = jnp.where(sel, v_ref[pl.ds(r0, SUBLANES), :], 0.0)
    q8b = q8.astype(BF16)
    k8b = k8.astype(BF16)
    v8b = v8.astype(BF16)
    k_row = jnp.sum(k8, axis=0, keepdims=True)
    q_row = jnp.sum(q8, axis=0, keepdims=True)
    gate = _pick_row(gate_ref, b)
    m_all = _pick_row(m_in_ref, b)
    lane = lax.broadcasted_iota(jnp.int32, (1, 128), 1)
    m_new_all = jnp.zeros((1, 128), F32)
    h_parts = []
    for h in range(M_HEADS):
        sl = slice(h * M_DH, (h + 1) * M_DH)
        ig = gate[:, h:h + 1]
        lf = _log_sigmoid(gate[:, M_HEADS + h:M_HEADS + h + 1])
        m_prev = m_all[:, h:h + 1]
        m_new = jnp.maximum(lf + m_prev, ig)
        a = jnp.exp(lf + m_prev - m_new)
        ws = jnp.exp(ig - m_new)
        c_new = a * c_in[0, h] + ws * _tn(k8b[:, sl], v8b[:, sl])
        c_out[0, h] = c_new
        n_new = a * n_in[0, h:h + 1, :] + ws * k_row[:, sl]
        n_out[0, h:h + 1, :] = n_new
        num = jnp.sum(_mm(q8b[:, sl], c_new.astype(BF16)), axis=0, keepdims=True)
        den = jnp.sum(q_row[:, sl] * n_new, axis=-1, keepdims=True)
        den = jnp.maximum(jnp.abs(den), jnp.exp(-m_new))
        h_parts.append(num / den)
        m_new_all = jnp.where(lane == h, m_new, m_new_all)
    _put_row(h_ref, b, jnp.concatenate(h_parts, axis=1))
    _put_row(m_out_ref, b, m_new_all)

    @pl.when(b == n_b - 1)
    def _():
        hfull = h_ref[...]
        hs = [hfull[:, h * M_DH:(h + 1) * M_DH] for h in range(M_HEADS)]
        y_ref[...] = x_ref[...] + _mlstm_out(hs, xc_ref[...], z_ref[...], normg_ref, skip_ref, wdown_ref)


def _odd_sample(x, g_all, gi, ml, c0, n0, m0, conv0):
    n = x.shape[0]
    m_pad = jnp.pad(m0, ((0, 0), (0, 128 - M_HEADS)))
    cw = (M_CONV - 1) * M_INNER
    full = lambda shape: pl.BlockSpec(shape, lambda b: (0,) * len(shape))
    outs = pl.pallas_call(
        functools.partial(_odd_sample_kernel, gi=gi),
        grid=(n,),
        in_specs=[_const_spec((n, D_MODEL)),
                  _const_spec(g_all.shape),
                  _const_spec((D_MODEL, 2 * M_INNER)),
                  _const_spec((M_CONV, M_INNER)),
                  _const_spec((1, M_INNER)),
                  _const_spec((n, cw)),
                  _const_spec((M_INNER // BD, BD, BD)),
                  _const_spec((M_INNER // BD, BD, BD)),
                  _const_spec((M_INNER // BD, BD, BD)),
                  _const_spec((3, M_INNER, 128)),
                  _const_spec((1, 128)),
                  _const_spec((1, M_INNER)),
                  _const_spec((1, M_INNER)),
                  _const_spec((M_INNER, D_MODEL)),
                  _const_spec((n, 128)),
                  pl.BlockSpec((1, M_HEADS, M_DH, M_DH), lambda b: (b, 0, 0, 0)),
                  pl.BlockSpec((1, M_HEADS, M_DH), lambda b: (b, 0, 0))],
        out_specs=[full((n, D_MODEL)),
                   pl.BlockSpec((1, M_HEADS, M_DH, M_DH), lambda b: (b, 0, 0, 0)),
                   pl.BlockSpec((1, M_HEADS, M_DH), lambda b: (b, 0, 0)),
                   full((n, 128)),
                   full((n, cw))],
        out_shape=[jax.ShapeDtypeStruct((n, D_MODEL), F32),
                   jax.ShapeDtypeStruct(c0.shape, F32),
                   jax.ShapeDtypeStruct(n0.shape, F32),
                   jax.ShapeDtypeStruct((n, 128), F32),
                   jax.ShapeDtypeStruct((n, cw), F32)],
        scratch_shapes=[pltpu.VMEM((n, M_INNER), F32), pltpu.VMEM((n, M_INNER), F32), pltpu.VMEM((n, M_INNER), F32),
                        pltpu.VMEM((n, 128), F32), pltpu.VMEM((n, M_INNER), F32), pltpu.VMEM((n, M_INNER), F32),
                        pltpu.VMEM((n, M_INNER), F32)],
        compiler_params=_params(("arbitrary",)),
        name="odd_sample",
    )(x, g_all, ml['win'], ml['convw'], ml['convb'], conv0.reshape(n, cw),
      ml['wq'], ml['wk'], ml['wv'], ml['wgate'], ml['bgate'], ml['normg'], ml['skip'], ml['wdown'],
      m_pad, c0, n0)
    y, c, nn, m_new, conv_new = outs
    return y, c, nn, m_new[:, :M_HEADS], conv_new.reshape(n, M_CONV - 1, M_INNER)


def _odd_sample_proj_kernel(x_ref, g_ref, win_ref, convw_ref, convb_ref, cv_ref, wq_ref, wk_ref, wv_ref,
                            wgate_ref, bgate_ref, m_in_ref, n_in_ref,
                            qt_ref, kwt_ref, v_ref, a_ref, den_ref, n_out_ref, m_out_ref, cv_out_ref,
                            xc_ref, z_ref, *, gi):
    hn = _rms(x_ref[...], g_ref[gi:gi + 1, :]).astype(BF16)
    p = _mm(hn, win_ref[...])
    xm = p[:, :M_INNER]
    z_ref[...] = p[:, M_INNER:]
    cv0 = cv_ref[:, :M_INNER]
    cv1 = cv_ref[:, M_INNER:2 * M_INNER]
    cv2 = cv_ref[:, 2 * M_INNER:]
    cv_out_ref[:, :M_INNER] = cv1
    cv_out_ref[:, M_INNER:2 * M_INNER] = cv2
    cv_out_ref[:, 2 * M_INNER:] = xm
    conv = (convb_ref[...] + convw_ref[3:4, :] * xm + convw_ref[2:3, :] * cv2
            + convw_ref[1:2, :] * cv1 + convw_ref[0:1, :] * cv0)
    xc = _silu(conv)
    xc_ref[...] = xc
    xcb = xc.astype(BF16)
    q = _headwise(xcb, wq_ref)
    k = _headwise(xcb, wk_ref)
    v = _headwise(xm.astype(BF16), wv_ref)
    v_ref[...] = v
    gate = (_mm(q.astype(BF16), wgate_ref[0]) + _mm(k.astype(BF16), wgate_ref[1])
            + _mm(v.astype(BF16), wgate_ref[2])) + bgate_ref[...]
    m_in = m_in_ref[...]
    lane = lax.broadcasted_iota(jnp.int32, m_in.shape, 1)
    a_all = jnp.zeros_like(m_in)
    den_all = jnp.zeros_like(m_in)
    m_all = jnp.zeros_like(m_in)
    for h in range(M_HEADS):
        sl = slice(h * M_DH, (h + 1) * M_DH)
        ig = gate[:, h:h + 1]
        lf = _log_sigmoid(gate[:, M_HEADS + h:M_HEADS + h + 1])
        m_prev = m_in[:, h:h + 1]
        m_new = jnp.maximum(lf + m_prev, ig)
        a = jnp.exp(lf + m_prev - m_new)
        kw = (jnp.exp(ig - m_new) * (M_DH ** -0.5)) * k[:, sl]
        n_new = a * n_in_ref[:, sl] + kw
        n_out_ref[:, sl] = n_new
        qh = q[:, sl]
        den = jnp.maximum(jnp.abs(jnp.sum(qh * n_new, axis=-1, keepdims=True)), jnp.exp(-m_new))
        a_all = jnp.where(lane == h, a, a_all)
        den_all = jnp.where(lane == h, den, den_all)
        m_all = jnp.where(lane == h, m_new, m_all)
        for c in range(M_DH // 128):
            qt_ref[h, c * 128:(c + 1) * 128, :] = qh[:, c * 128:(c + 1) * 128].T
            kwt_ref[h, c * 128:(c + 1) * 128, :] = kw[:, c * 128:(c + 1) * 128].T
    a_ref[...] = a_all
    den_ref[...] = den_all
    m_out_ref[...] = m_all


def _odd_sample_proj(x, g_all, gi, ml, n0, m0, conv0):
    n = x.shape[0]
    assert n == 128, "the per-head transposes assume one 128-lane tile of sequences"
    m_pad = jnp.pad(m0, ((0, 0), (0, 128 - M_HEADS)))
    cw = (M_CONV - 1) * M_INNER
    f32 = lambda *shape: jax.ShapeDtypeStruct(shape, F32)
    shapes = [f32(M_HEADS, M_DH, n), f32(M_HEADS, M_DH, n), f32(n, M_INNER), f32(n, 128), f32(n, 128),
              f32(n, M_INNER), f32(n, 128), f32(n, cw), f32(n, M_INNER), f32(n, M_INNER)]
    args = (x, g_all, ml['win'], ml['convw'], ml['convb'], conv0.reshape(n, cw), ml['wq'], ml['wk'], ml['wv'],
            ml['wgate'], ml['bgate'], m_pad, n0.reshape(n, M_INNER))
    outs = pl.pallas_call(
        functools.partial(_odd_sample_proj_kernel, gi=gi),
        grid=(1,),
        in_specs=[_const_spec(a.shape) for a in args],
        out_specs=[pl.BlockSpec(s.shape, lambda i, nd=len(s.shape): (0,) * nd) for s in shapes],
        out_shape=shapes,
        compiler_params=_params(("arbitrary",)),
        name="odd_sample_proj",
    )(*args)
    qt, kwt, v, a, den, n_new, m_new, conv_new, xc, z = outs
    return (qt, kwt, v, a, den, n_new.reshape(n, M_HEADS, M_DH), m_new[:, :M_HEADS],
            conv_new.reshape(n, M_CONV - 1, M_INNER), xc, z)


STREAM_BUFS = 4
STREAM_GROUPS = (3, 3, 3, 3, 2, 2)
STREAM_ROWS = 32


def _ffn_stream_kernel(xp_ref, g_ref, gfin_ref, wg_ref, wu_ref, wd_ref, qt_ref, kwt_ref, v_ref, a_ref, c_in,
                       op_ref, c_out, hnum_ref, cin_buf, cout_buf, sem_in, sem_out, *, gi, per_step):
    i = pl.program_id(0)
    n_pieces = per_step * M_HEADS
    assert sum(STREAM_GROUPS) == n_pieces

    def seq(p):
        return i * per_step + p // M_HEADS

    def in_copy(p):
        slot = p % STREAM_BUFS
        return pltpu.make_async_copy(c_in.at[seq(p), p % M_HEADS], cin_buf.at[slot], sem_in.at[slot])

    def out_copy(p):
        slot = p % STREAM_BUFS
        return pltpu.make_async_copy(cout_buf.at[slot], c_out.at[seq(p), p % M_HEADS], sem_out.at[slot])

    @pl.when(i == 0)
    def _():
        hnum_ref[...] = jnp.zeros_like(hnum_ref)

    for p in range(STREAM_BUFS):
        in_copy(p).start()

    shift = (128 - (i * per_step) % 128) % 128
    rolled = {}

    def columns(ref, h):
        if (id(ref), h) not in rolled:
            rolled[(id(ref), h)] = pltpu.roll(ref[h], shift, 1)
        return rolled[(id(ref), h)]

    rows = {}

    def seq_rows(j):
        if j not in rows:
            s = i * per_step + j
            rows[j] = (_pick_row(a_ref, s), _pick_row(v_ref, s))
        return rows[j]

    h_parts = {}

    def piece(p):
        j, h = divmod(p, M_HEADS)
        slot = p % STREAM_BUFS
        a_row, v_row = seq_rows(j)
        in_copy(p).wait()
        if p >= STREAM_BUFS:
            out_copy(p - STREAM_BUFS).wait()
        kw_col = columns(kwt_ref, h)[:, j:j + 1]
        q_col = columns(qt_ref, h)[:, j:j + 1]
        a = a_row[:, h:h + 1]
        v_h = v_row[:, h * M_DH:(h + 1) * M_DH]
        acc = jnp.zeros((STREAM_ROWS, M_DH), F32)
        for r0 in range(0, M_DH, STREAM_ROWS):
            rs = slice(r0, r0 + STREAM_ROWS)
            c_new = a * cin_buf[slot, rs, :] + kw_col[rs] * v_h
            cout_buf[slot, rs, :] = c_new
            acc = acc + q_col[rs] * c_new
        out_copy(p).start()
        if p + STREAM_BUFS < n_pieces:
            in_copy(p + STREAM_BUFS).start()
        h_parts[(j, h)] = jnp.sum(acc, axis=0, keepdims=True)
        if h == M_HEADS - 1:
            _put_row(hnum_ref, i * per_step + j, jnp.concatenate([h_parts[(j, hh)] for hh in range(M_HEADS)], axis=1))

    todo = iter(range(n_pieces))

    def stream(group):
        for _ in range(STREAM_GROUPS[group]):
            piece(next(todo))

    x = xp_ref[...]
    hb = _rms(x, g_ref[gi:gi + 1, :]).astype(BF16)
    stream(0)
    lo, mid, hi = FFN_BOUNDS
    gt = _mm(hb, wg_ref[:, lo:mid])
    stream(1)
    ut = _mm(hb, wu_ref[:, lo:mid])
    stream(2)
    y = _mm((_silu(gt) * ut).astype(BF16), wd_ref[lo:mid, :])
    stream(3)
    gt = _mm(hb, wg_ref[:, mid:hi])
    stream(4)
    ut = _mm(hb, wu_ref[:, mid:hi])
    stream(5)
    y = y + _mm((_silu(gt) * ut).astype(BF16), wd_ref[mid:hi, :])
    op_ref[...] = _rms(x + 0.5 * y, gfin_ref[...])
    for p in range(n_pieces - STREAM_BUFS, n_pieces):
        out_copy(p).wait()


def _ffn_stream(xp, g_all, gfin, wg, wu, wd, gi, tm, qt, kwt, v, a, c0):
    n = xp.shape[0]
    ns = v.shape[0]
    steps = n // tm
    per_step = ns // steps
    assert per_step * steps == ns and 128 % per_step == 0
    any_spec = pl.BlockSpec(memory_space=pl.ANY)
    return pl.pallas_call(
        functools.partial(_ffn_stream_kernel, gi=gi, per_step=per_step),
        grid=(steps,),
        in_specs=[pl.BlockSpec((tm, D_MODEL), lambda i: (i, 0)),
                  _const_spec(g_all.shape),
                  _const_spec((1, D_MODEL)),
                  _const_spec((D_MODEL, D_FF)), _const_spec((D_MODEL, D_FF)), _const_spec((D_FF, D_MODEL)),
                  _const_spec(qt.shape), _const_spec(kwt.shape), _const_spec(v.shape), _const_spec(a.shape),
                  any_spec],
        out_specs=[pl.BlockSpec((tm, D_MODEL), lambda i: (i, 0)),
                   any_spec,
                   pl.BlockSpec((ns, M_INNER), lambda i: (0, 0))],
        out_shape=[jax.ShapeDtypeStruct((n, D_MODEL), F32),
                   jax.ShapeDtypeStruct(c0.shape, F32),
                   jax.ShapeDtypeStruct((ns, M_INNER), F32)],
        scratch_shapes=[pltpu.VMEM((STREAM_BUFS, M_DH, M_DH), F32), pltpu.VMEM((STREAM_BUFS, M_DH, M_DH), F32),
                        pltpu.SemaphoreType.DMA((STREAM_BUFS,)), pltpu.SemaphoreType.DMA((STREAM_BUFS,))],
        compiler_params=_params(("arbitrary",)),
        name="ffn_final_stream",
    )(xp, g_all, gfin, wg, wu, wd, qt, kwt, v, a, c0)


def _odd_sample_out_kernel(x_ref, hnum_ref, den_ref, xc_ref, z_ref, normg_ref, skip_ref, wdown_ref,
                           g_ref, gfin_ref, wg_ref, wu_ref, wd_ref, y_ref, *, gi):
    hnum = hnum_ref[...]
    den = den_ref[...]
    hs = [hnum[:, h * M_DH:(h + 1) * M_DH] * (1.0 / den[:, h:h + 1]) for h in range(M_HEADS)]
    xs = x_ref[...] + _mlstm_out(hs, xc_ref[...], z_ref[...], normg_ref, skip_ref, wdown_ref)
    y_ref[...] = _ffn_rows(xs, g_ref[gi:gi + 1, :], wg_ref, wu_ref, wd_ref, gfin_ref[...])


def _odd_sample_out(x, hnum, den, xc, z, ml, g_all, gfin, wg, wu, wd, gi):
    args = (x, hnum, den, xc, z, ml['normg'], ml['skip'], ml['wdown'], g_all, gfin, wg, wu, wd)
    return pl.pallas_call(
        functools.partial(_odd_sample_out_kernel, gi=gi),
        grid=(1,),
        in_specs=[_const_spec(a.shape) for a in args],
        out_specs=pl.BlockSpec(x.shape, lambda i: (0, 0)),
        out_shape=jax.ShapeDtypeStruct(x.shape, F32),
        compiler_params=_params(("arbitrary",)),
        name="odd_sample_out",
    )(*args)


TM_FFN = 512
TT_EVEN = 256
TT_ODD = 512


def kernel(x_prompt, x_sample, state_ret, state_hgrn, state_mlstm_C, state_mlstm_n, state_mlstm_m, state_mlstm_conv,
           norm_g, final_norm_g, ffn_w_gate, ffn_w_up, ffn_w_down, ev_w_in, ev_w_out, ret_norm_g, hg_norm_g,
           hg_lb_logits, ml_w_in, ml_conv_w, ml_conv_b, ml_w_q, ml_w_k, ml_w_v, ml_w_ig, ml_b_ig, ml_w_fg,
           ml_b_fg, ml_norm_g, ml_skip, ml_w_down):
    bp, tp, _ = x_prompt.shape
    ns = x_sample.shape[0]

    g_all = norm_g.reshape(-1, D_MODEL)
    gfin = final_norm_g.reshape(1, D_MODEL)
    lb_all = jnp.cumsum(jax.nn.softmax(hg_lb_logits.astype(F32), axis=0), axis=0)
    lb = lb_all[0].reshape(1, G_HEADS * G_EXP)
    retg = ret_norm_g[0]
    hgg = hg_norm_g[0]
    wgate = _gate_weights(ml_w_ig[0], ml_w_fg[0])
    bgate = jnp.pad(jnp.concatenate([ml_b_ig[0], ml_b_fg[0]]), (0, 128 - 2 * M_HEADS)).reshape(1, 128)
    ml = {
        'convw': ml_conv_w[0],
        'convb': ml_conv_b[0].reshape(1, M_INNER),
        'wq': _block_diag(ml_w_q[0]).astype(BF16),
        'wk': _block_diag(ml_w_k[0]).astype(BF16),
        'wv': _block_diag(ml_w_v[0]).astype(BF16),
        'wgate': wgate.astype(BF16),
        'wgate_t': jnp.swapaxes(wgate[:, :, :SUBLANES], 1, 2).astype(BF16),
        'bgate': bgate,
        'bgate_col': bgate[0, :SUBLANES].reshape(SUBLANES, 1),
        'normg': ml_norm_g[0].reshape(1, M_INNER),
        'skip': ml_skip[0].reshape(1, M_INNER),
    }

    xp = x_prompt.reshape(bp * tp, D_MODEL)
    xs = x_sample.reshape(ns, D_MODEL)
    ffn_w = (ffn_w_gate, ffn_w_up, ffn_w_down)
    next_ffn = lambda layer, idx: [(w, (layer, idx)) for w in ffn_w]

    w00 = [w[0, 0].astype(BF16) for w in ffn_w]
    xp, xs, cast = _ffn(xp, xs, g_all, gfin, *w00, 0, TM_FFN,
                        casts=next_ffn(0, 1) + [(ev_w_in, (0,)), (ev_w_out, (0,))])
    w01, (ev_in, ev_out) = cast[:3], cast[3:]
    xp, ret_p, hg_p = _even_prompt(xp.reshape(bp, tp, D_MODEL), g_all, 1, ev_in, ev_out, retg, hgg, lb, TT_EVEN)
    xs, ret_s, hg_s = _even_sample(xs, g_all, 1, ev_in, ev_out, retg, hgg, lb, state_ret[:, 0], state_hgrn[:, 0])
    xp, xs, cast = _ffn(xp.reshape(bp * tp, D_MODEL), xs, g_all, gfin, *w01, 2, TM_FFN,
                        casts=next_ffn(1, 0) + [(ml_w_in, (0,)), (ml_w_down, (0,))])
    w10, (ml['win'], ml['wdown']) = cast[:3], cast[3:]
    xp, xs, w11 = _ffn(xp, xs, g_all, gfin, *w10, 3, TM_FFN, casts=next_ffn(1, 1))
    xp, c_p, n_p, m_p, conv_p = _odd_prompt(xp.reshape(bp, tp, D_MODEL), g_all, 4, ml, TT_ODD)
    qt, kwt, v_s, a_s, den_s, n_s, m_s, conv_s, xc_s, z_s = _odd_sample_proj(
        xs, g_all, 4, ml, state_mlstm_n[:, 0], state_mlstm_m[:, 0], state_mlstm_conv[:, 0])
    y_p, c_s, hnum = _ffn_stream(xp.reshape(bp * tp, D_MODEL), g_all, gfin, *w11, 5, TM_FFN,
                                 qt, kwt, v_s, a_s, state_mlstm_C[:, 0])
    y_s = _odd_sample_out(xs, hnum, den_s, xc_s, z_s, ml, g_all, gfin, *w11, 5)

    return (y_p.reshape(bp, tp, D_MODEL), y_s.reshape(ns, 1, D_MODEL),
            ret_p[:, None], hg_p[:, None], c_p[:, None], n_p[:, None], m_p[:, None, :M_HEADS, 0], conv_p[:, None],
            ret_s[:, None], hg_s[:, None], c_s[:, None], n_s[:, None], m_s[:, None], conv_s[:, None])
```

```python
import functools
import math

import jax
import jax.numpy as jnp
from jax import lax
from jax.experimental import pallas as pl
from jax.experimental.pallas import tpu as pltpu

D_MODEL = 1024
PAST_LEN = 16384
R_HEADS = 4
R_DK = 128
R_DV = 128
G_HEADS = 4
G_EXP = 128
G_DV = 128
M_INNER = 2 * D_MODEL
M_HEADS = 4
M_DH = M_INNER // M_HEADS
M_CONV = 4
QKV_BLOCK = 4
D_FF = 2816
EPS = 1e-6
ROPE_BASE = 10000.0
EVEN_IN = 4096
EVEN_OUT = 1024

F32 = jnp.float32
BF16 = jnp.bfloat16

VMEM_LIMIT_BYTES = 56 * 1024 * 1024

HG_CHUNK = 64
HG_SUB = 16
HG_SAFE_LOG_DECAY = -60.0
MXU_TILE = 256
FFN_BOUNDS = (0, 6 * MXU_TILE, D_FF)
BD = MXU_TILE
SUBLANES = 8


def _nt(a, b):
    return lax.dot_general(a, b, (((1,), (1,)), ((), ())), preferred_element_type=F32)


def _tn(a, b):
    return lax.dot_general(a, b, (((0,), (0,)), ((), ())), preferred_element_type=F32)


def _mm(a, b):
    return jnp.dot(a, b, preferred_element_type=F32)


def _sigmoid(x):
    return 1.0 / (1.0 + jnp.exp(-x))


def _silu(x):
    return x * _sigmoid(x)


def _log_sigmoid(x):
    return jnp.minimum(x, 0.0) - jnp.log(1.0 + jnp.exp(-jnp.abs(x)))


def _rms(x, g):
    return x * lax.rsqrt(jnp.mean(x * x, axis=-1, keepdims=True) + EPS) * g


def _head_norm(x, g, center):
    if center:
        x = x - jnp.mean(x, axis=-1, keepdims=True)
    return x * lax.rsqrt(jnp.mean(x * x, axis=-1, keepdims=True) + EPS) * g


def _rotary(x, cos, sin_signed):
    return x * cos + pltpu.roll(x, 64, 1) * sin_signed


def _const_spec(shape):
    n = len(shape)
    return pl.BlockSpec(shape, lambda *_: (0,) * n, pipeline_mode=pl.Buffered(1))


def _params(sem):
    return pltpu.CompilerParams(dimension_semantics=sem, vmem_limit_bytes=VMEM_LIMIT_BYTES)


def _ffn_rows(x, g, wg_ref, wu_ref, wd_ref, gfin):
    h = _rms(x, g).astype(BF16)
    y = jnp.zeros_like(x)
    for lo, hi in zip(FFN_BOUNDS[:-1], FFN_BOUNDS[1:]):
        gt = _mm(h, wg_ref[:, lo:hi])
        ut = _mm(h, wu_ref[:, lo:hi])
        a = (_silu(gt) * ut).astype(BF16)
        y = y + _mm(a, wd_ref[lo:hi, :])
    out = x + 0.5 * y
    if gfin is not None:
        out = _rms(out, gfin)
    return out


def _ffn_kernel(xp_ref, xs_ref, g_ref, gfin_ref, wg_ref, wu_ref, wd_ref, *rest, gi, final, n_cast):
    cast_in = rest[:n_cast]
    op_ref, os_ref = rest[n_cast:n_cast + 2]
    cast_out = rest[n_cast + 2:]
    g = g_ref[gi:gi + 1, :]
    gfin = gfin_ref[...] if final else None
    op_ref[...] = _ffn_rows(xp_ref[...], g, wg_ref, wu_ref, wd_ref, gfin)
    for src, dst in zip(cast_in, cast_out):
        dst[...] = src[...].astype(BF16)

    @pl.when(pl.program_id(0) == pl.num_programs(0) - 1)
    def _():
        os_ref[...] = _ffn_rows(xs_ref[...], g, wg_ref, wu_ref, wd_ref, gfin)


BF16_ROWS = 16


def _cast_specs(arr, lead, steps):
    rows, cols = arr.shape[-2:]
    per = 1 if (rows // steps) % BF16_ROWS == 0 else 2
    br = rows * per // steps
    in_spec = pl.BlockSpec((None,) * len(lead) + (br, cols), lambda i: tuple(lead) + (i // per, 0))
    out_spec = pl.BlockSpec((br, cols), lambda i: (i // per, 0))
    return in_spec, out_spec, jax.ShapeDtypeStruct((rows, cols), BF16)


def _ffn(xp, xs, g_all, gfin, wg, wu, wd, gi, tm, final=False, casts=()):
    n = xp.shape[0]
    ns = xs.shape[0]
    steps = n // tm
    cast_specs = [_cast_specs(arr, lead, steps) for arr, lead in casts]
    outs = pl.pallas_call(
        functools.partial(_ffn_kernel, gi=gi, final=final, n_cast=len(casts)),
        grid=(steps,),
        in_specs=[pl.BlockSpec((tm, D_MODEL), lambda i: (i, 0)),
                  _const_spec((ns, D_MODEL)),
                  _const_spec(g_all.shape),
                  _const_spec((1, D_MODEL)),
                  _const_spec((D_MODEL, D_FF)), _const_spec((D_MODEL, D_FF)), _const_spec((D_FF, D_MODEL))]
                 + [c[0] for c in cast_specs],
        out_specs=[pl.BlockSpec((tm, D_MODEL), lambda i: (i, 0)),
                   pl.BlockSpec((ns, D_MODEL), lambda i: (0, 0))] + [c[1] for c in cast_specs],
        out_shape=[jax.ShapeDtypeStruct((n, D_MODEL), F32), jax.ShapeDtypeStruct((ns, D_MODEL), F32)]
                  + [c[2] for c in cast_specs],
        compiler_params=_params(("arbitrary",)),
        name="ffn_final" if final else "ffn",
    )(xp, xs, g_all, gfin, wg, wu, wd, *[arr for arr, _ in casts])
    return outs[0], outs[1], outs[2:]


def _ret_log_gamma(h):
    return math.log(1.0 - 2.0 ** (-5.0 - h))


def _head_parts(p, mixer, h):
    base = mixer * 4 * R_HEADS * R_DK
    return [p[:, base + (j * R_HEADS + h) * R_DK:base + (j * R_HEADS + h + 1) * R_DK] for j in range(4)]


def _hgrn_gates(gq, gf, lb):
    f = lb + (1.0 - lb) * _sigmoid(gf)
    kk = (1.0 - lb) * _sigmoid(-gf)
    qq = _silu(gq)
    return qq, kk, f


def _split3(x):
    hi = x.astype(BF16)
    r1 = x - hi.astype(F32)
    mid = r1.astype(BF16)
    lo = (r1 - mid.astype(F32)).astype(BF16)
    return hi, mid, lo


def _shift_rows(bases, d):
    base = bases[d % SUBLANES]
    full = (d // SUBLANES) * SUBLANES
    return pltpu.roll(base, full, 0) if full else base


def _hgrn_tile_factorised(qq, kk, vv, b, st):
    tt = qq[0].shape[0]
    heads = range(len(qq))
    qx = [(qq[h] * jnp.exp(b[h])).astype(BF16) for h in heads]
    kx = [kk[h] * jnp.exp(-b[h]) for h in heads]
    kxb = [kx[h].astype(BF16) for h in heads]
    vb = [vv[h].astype(BF16) for h in heads]
    ti = lax.broadcasted_iota(jnp.int32, (HG_CHUNK, HG_CHUNK), 0)
    si = lax.broadcasted_iota(jnp.int32, (HG_CHUNK, HG_CHUNK), 1)
    causal = si <= ti
    st = list(st)
    o_chunks = [[] for _ in heads]
    for c in range(tt // HG_CHUNK):
        rs = slice(c * HG_CHUNK, (c + 1) * HG_CHUNK)
        for h in heads:
            a = jnp.where(causal, _nt(qx[h][rs], kxb[h][rs]), 0.0)
            o_chunks[h].append(_mm(a.astype(BF16), vb[h][rs]) + _nt(qx[h][rs], st[h].astype(BF16)))
            etot = jnp.exp(b[h][(c + 1) * HG_CHUNK - 1:(c + 1) * HG_CHUNK])
            st[h] = st[h] * etot + _tn(vb[h][rs], (kx[h][rs] * etot).astype(BF16))
    return [jnp.concatenate(o_chunks[h], axis=0) for h in heads], st


def _hgrn_tile_guarded(qq_all, kk_all, f_all, vv_all, b_all, st_all):
    tt = qq_all.shape[0]
    hd = G_EXP
    row = lax.broadcasted_iota(jnp.int32, (tt, hd), 0)
    rsub = row % HG_SUB
    sub = (lax.broadcasted_iota(jnp.int32, (HG_CHUNK, hd), 0)) // HG_SUB
    n_sub = HG_CHUNK // HG_SUB
    outs, states = [], []
    for h in range(G_HEADS):
        hs = slice(h * hd, (h + 1) * hd)
        qq = qq_all[:, hs]
        kk = kk_all[:, hs]
        ff = f_all[:, hs]
        bc_all = b_all[:, hs]
        vv = vv_all[:, hs]
        f_sh = [ff] + [pltpu.roll(ff, r, 0) for r in range(1, SUBLANES)]
        k_sh = [kk] + [pltpu.roll(kk, r, 0) for r in range(1, SUBLANES)]
        v_sh = [vv] + [pltpu.roll(vv, r, 0) for r in range(1, SUBLANES)]
        o_band = jnp.sum(qq * kk, axis=-1, keepdims=True) * vv
        decay = None
        for d in range(1, HG_SUB):
            fd = _shift_rows(f_sh, d - 1)
            decay = jnp.where(rsub >= d, fd if decay is None else decay * fd, 0.0)
            term = qq * _shift_rows(k_sh, d) * decay
            o_band = o_band + jnp.sum(term, axis=-1, keepdims=True) * _shift_rows(v_sh, d)
        st = st_all[h]
        o_chunks = []
        for c in range(tt // HG_CHUNK):
            r0 = c * HG_CHUNK
            bc = bc_all[r0:r0 + HG_CHUNK]
            qc = qq[r0:r0 + HG_CHUNK]
            kc = kk[r0:r0 + HG_CHUNK]
            vcb = vv[r0:r0 + HG_CHUNK].astype(BF16)
            refs = [bc[i * HG_SUB - 1:i * HG_SUB] for i in range(1, n_sub)]
            refrow = refs[-1]
            for i in range(n_sub - 2, 0, -1):
                refrow = jnp.where(sub == i, refs[i - 1], refrow)
            qp = qc * jnp.exp(bc - refrow)
            lhs = jnp.concatenate([jnp.where(sub == i, qp, 0.0) for i in range(1, n_sub)], axis=1)
            kcat = jnp.concatenate([jnp.where(sub < i, kc * jnp.exp(refs[i - 1] - bc), 0.0)
                                    for i in range(1, n_sub)], axis=1)
            a = _nt(lhs.astype(BF16), kcat.astype(BF16))
            qb = (qc * jnp.exp(bc)).astype(BF16)
            o_chunks.append(_mm(a.astype(BF16), vcb) + _nt(qb, st.astype(BF16)))
            btot = bc[HG_CHUNK - 1:HG_CHUNK]
            ke = (kc * jnp.exp(btot - bc)).astype(BF16)
            st = st * jnp.exp(btot) + _tn(vcb, ke)
        outs.append(o_band + jnp.concatenate(o_chunks, axis=0))
        states.append(st)
    return jnp.concatenate(outs, axis=1), jnp.stack(states)


def _even_prompt_kernel(x_ref, g_ref, win_ref, wout_ref, cos_ref, sin_ref, retg_ref, hgg_ref, lb_ref,
                        y_ref, sret_ref, shg_ref, dmat_ref, tri_ref, st_ref, ohg_ref, *, tt, gi):
    b_id = pl.program_id(0)
    t_id = pl.program_id(1)
    n_t = pl.num_programs(1)
    hd = 128

    @pl.when(jnp.logical_and(b_id == 0, t_id == 0))
    def _():
        ti = lax.broadcasted_iota(jnp.int32, (tt, tt), 0)
        si = lax.broadcasted_iota(jnp.int32, (tt, tt), 1)
        diff = (ti - si).astype(F32)
        for h in range(R_HEADS):
            dmat_ref[h] = jnp.where(diff >= 0.0, jnp.exp(_ret_log_gamma(h) * jnp.maximum(diff, 0.0)), 0.0)
        same_chunk = (ti // HG_CHUNK) == (si // HG_CHUNK)
        tri_ref[...] = jnp.where(jnp.logical_and(same_chunk, si <= ti), 1.0, 0.0).astype(BF16)

    @pl.when(t_id == 0)
    def _():
        sret_ref[...] = jnp.zeros_like(sret_ref)
        st_ref[...] = jnp.zeros_like(st_ref)

    x = x_ref[0]
    hn = _rms(x, g_ref[gi:gi + 1, :]).astype(BF16)
    cos = cos_ref[...]
    sin = sin_ref[...]
    rowf = lax.broadcasted_iota(jnp.int32, (tt, hd), 0).astype(F32)
    outs = []
    hg = []
    logf_parts = []
    st_old = st_ref[...]

    p = _mm(hn, win_ref[...])
    for i in range(R_HEADS + G_HEADS):
        h = i // 2
        pa, pb, pc, pd = _head_parts(p, i % 2, h)
        if i % 2 == 0:
            lg = _ret_log_gamma(h)
            q = _rotary(pa, cos, sin)
            k = _rotary(pb, cos, sin) * (R_DK ** -0.5)
            v = pc
            rg = pd
            s0 = sret_ref[0, h]
            qb = q.astype(BF16)
            vb = v.astype(BF16)
            inter = _mm(qb, s0.astype(BF16)) * jnp.exp(lg * (rowf + 1.0))
            scores = _nt(qb, k.astype(BF16)) * dmat_ref[h]
            intra = _mm(scores.astype(BF16), vb)
            kd = (k * jnp.exp(lg * (tt - 1.0 - rowf))).astype(BF16)
            sret_ref[0, h] = math.exp(lg * tt) * s0 + _tn(kd, vb)
            outs.append(_head_norm(inter + intra, retg_ref[h:h + 1, :], False) * _silu(rg))
        else:
            qq, kk, ff = _hgrn_gates(pa, pb, lb_ref[:, h * hd:(h + 1) * hd])
            hg.append((qq, kk, ff, pc, _sigmoid(pd)))
            logf_parts.extend(_split3(jnp.log(ff)))

    cs = _mm(tri_ref[...], jnp.concatenate(logf_parts, axis=1))
    b = [cs[:, (3 * h) * hd:(3 * h + 1) * hd] + cs[:, (3 * h + 1) * hd:(3 * h + 2) * hd]
         + cs[:, (3 * h + 2) * hd:(3 * h + 3) * hd] for h in range(G_HEADS)]
    b_all = jnp.concatenate(b, axis=1)
    o_fast, st_fast = _hgrn_tile_factorised([t[0] for t in hg], [t[1] for t in hg], [t[3] for t in hg], b,
                                            [st_old[h] for h in range(G_HEADS)])
    for h in range(G_HEADS):
        ohg_ref[:, h * hd:(h + 1) * hd] = o_fast[h]
        st_ref[h] = st_fast[h]

    @pl.when(jnp.min(b_all) < HG_SAFE_LOG_DECAY)
    def _():
        cat = lambda j: jnp.concatenate([t[j] for t in hg], axis=1)
        o_safe, st_safe = _hgrn_tile_guarded(cat(0), cat(1), cat(2), cat(3), b_all, st_old)
        ohg_ref[...] = o_safe
        st_ref[...] = st_safe

    for h in range(G_HEADS):
        outs.append(_head_norm(ohg_ref[:, h * hd:(h + 1) * hd], hgg_ref[h:h + 1, :], False) * hg[h][4])


    ycat = jnp.concatenate(outs, axis=1).astype(BF16)
    y_ref[0] = x + _mm(ycat, wout_ref[...])

    @pl.when(t_id == n_t - 1)
    def _():
        for h in range(G_HEADS):
            shg_ref[0, h] = st_ref[h].T


def _rope_tables(pos):
    half = R_DK // 2
    inv = ROPE_BASE ** (-jnp.arange(half, dtype=F32) / half)
    ang = pos.astype(F32)[:, None] * inv[None, :]
    cos = jnp.cos(ang)
    sin = jnp.sin(ang)
    return jnp.concatenate([cos, cos], axis=-1), jnp.concatenate([-sin, sin], axis=-1)


def _even_prompt(x, g_all, gi, win, wout, retg, hgg, lb, tt):
    bsz, seq, _ = x.shape
    cos, sin = _rope_tables(jnp.arange(seq, dtype=jnp.int32))
    state_spec = pl.BlockSpec((1, 4, 128, 128), lambda b, t: (b, 0, 0, 0))
    return pl.pallas_call(
        functools.partial(_even_prompt_kernel, tt=tt, gi=gi),
        grid=(bsz, seq // tt),
        in_specs=[pl.BlockSpec((1, tt, D_MODEL), lambda b, t: (b, t, 0)),
                  _const_spec(g_all.shape),
                  _const_spec((D_MODEL, EVEN_IN)),
                  _const_spec((EVEN_OUT, D_MODEL)),
                  pl.BlockSpec((tt, 128), lambda b, t: (t, 0)),
                  pl.BlockSpec((tt, 128), lambda b, t: (t, 0)),
                  _const_spec((4, 128)), _const_spec((4, 128)), _const_spec((1, 512))],
        out_specs=[pl.BlockSpec((1, tt, D_MODEL), lambda b, t: (b, t, 0)), state_spec, state_spec],
        out_shape=[jax.ShapeDtypeStruct(x.shape, F32),
                   jax.ShapeDtypeStruct((bsz, 4, 128, 128), F32),
                   jax.ShapeDtypeStruct((bsz, 4, 128, 128), F32)],
        scratch_shapes=[pltpu.VMEM((4, tt, tt), F32), pltpu.VMEM((tt, tt), BF16), pltpu.VMEM((4, 128, 128), F32),
                        pltpu.VMEM((tt, G_HEADS * G_DV), F32)],
        compiler_params=_params(("arbitrary", "arbitrary")),
        name="even_prompt",
    )(x, g_all, win, wout, cos, sin, retg, hgg, lb)


SB = 8


def _even_sample_kernel(x_ref, g_ref, win_ref, wout_ref, cos_ref, sin_ref, retg_ref, hgg_ref, lb_ref,
                        sret_in, shg_in, y_ref, sret_out, shg_out, p_ref, o_ref, *, gi):
    i = pl.program_id(0)
    n_i = pl.num_programs(0)
    hd = 128

    @pl.when(i == 0)
    def _():
        hn = _rms(x_ref[...], g_ref[gi:gi + 1, :]).astype(BF16)
        p_ref[...] = _mm(hn, win_ref[...])

    r0 = pl.multiple_of(i * SB, SB)
    p = p_ref[pl.ds(r0, SB), :]
    cos = cos_ref[...]
    sin = sin_ref[...]
    row = lax.broadcasted_iota(jnp.int32, (SB, hd), 0)
    outs = []
    for h in range(R_HEADS):
        gamma = math.exp(_ret_log_gamma(h))
        pa, pb, pc, rg = _head_parts(p, 0, h)
        q = _rotary(pa, cos, sin).astype(BF16)
        k = _rotary(pb, cos, sin) * (R_DK ** -0.5)
        vb = pc.astype(BF16)
        o = jnp.zeros((SB, hd), F32)
        for j in range(SB):
            kj = jnp.where(row == j, k, 0.0).astype(BF16)
            s_new = gamma * sret_in[j, h] + _tn(kj, vb)
            sret_out[j, h] = s_new
            o = jnp.where(row == j, _mm(q, s_new.astype(BF16)), o)
        outs.append(_head_norm(o, retg_ref[h:h + 1, :], False) * _silu(rg))
    for h in range(G_HEADS):
        pa, pb, pc, gg = _head_parts(p, 1, h)
        qq, kk, ff = _hgrn_gates(pa, pb, lb_ref[:, h * hd:(h + 1) * hd])
        vb = pc.astype(BF16)
        f_cols = jnp.concatenate([ff, jnp.zeros((hd - SB, hd), F32)], axis=0).T
        qb = qq.astype(BF16)
        o = jnp.zeros((SB, hd), F32)
        for j in range(SB):
            kj = jnp.where(row == j, kk, 0.0).astype(BF16)
            s_new = f_cols[:, j:j + 1] * shg_in[j, h] + _tn(kj, vb)
            shg_out[j, h] = s_new
            o = jnp.where(row == j, _mm(qb, s_new.astype(BF16)), o)
        outs.append(_head_norm(o, hgg_ref[h:h + 1, :], False) * _sigmoid(gg))
    o_ref[pl.ds(r0, SB), :] = jnp.concatenate(outs, axis=1)

    @pl.when(i == n_i - 1)
    def _():
        y_ref[...] = x_ref[...] + _mm(o_ref[...].astype(BF16), wout_ref[...])


def _even_sample(x, g_all, gi, win, wout, retg, hgg, lb, sret, shg):
    n = x.shape[0]
    cos, sin = _rope_tables(jnp.full((1,), PAST_LEN, dtype=jnp.int32))
    state_spec = pl.BlockSpec((SB, 4, 128, 128), lambda i: (i, 0, 0, 0))
    return pl.pallas_call(
        functools.partial(_even_sample_kernel, gi=gi),
        grid=(n // SB,),
        in_specs=[_const_spec((n, D_MODEL)),
                  _const_spec(g_all.shape),
                  _const_spec((D_MODEL, EVEN_IN)),
                  _const_spec((EVEN_OUT, D_MODEL)),
                  _const_spec((1, 128)), _const_spec((1, 128)),
                  _const_spec((4, 128)), _const_spec((4, 128)), _const_spec((1, 512)),
                  state_spec, state_spec],
        out_specs=[pl.BlockSpec((n, D_MODEL), lambda i: (0, 0)), state_spec, state_spec],
        out_shape=[jax.ShapeDtypeStruct((n, D_MODEL), F32),
                   jax.ShapeDtypeStruct(sret.shape, F32),
                   jax.ShapeDtypeStruct(shg.shape, F32)],
        scratch_shapes=[pltpu.VMEM((n, EVEN_IN), F32), pltpu.VMEM((n, EVEN_OUT), F32)],
        compiler_params=_params(("arbitrary",)),
        name="even_sample",
    )(x, g_all, win, wout, cos, sin, retg, hgg, lb, sret, shg)


def _block_diag(w):
    wr = w.reshape(M_INNER // BD, BD, QKV_BLOCK)
    tiled = jnp.tile(wr, (1, 1, BD // QKV_BLOCK))
    rb = lax.broadcasted_iota(jnp.int32, (BD, BD), 0) // QKV_BLOCK
    cb = lax.broadcasted_iota(jnp.int32, (BD, BD), 1) // QKV_BLOCK
    return jnp.where((rb == cb)[None], tiled, 0.0)


def _headwise(xb, w_ref, g0=0):
    return jnp.concatenate([_mm(xb[:, g * BD:(g + 1) * BD], w_ref[g0 + g]) for g in range(xb.shape[1] // BD)], axis=1)


def _gate_weights(w_ig, w_fg):
    w = jnp.concatenate([w_ig, w_fg], axis=1)
    w = jnp.pad(w, ((0, 0), (0, 128 - 2 * M_HEADS)))
    return w.reshape(3, M_INNER, 128)


def _mlstm_out(hs, xc, z, normg_ref, skip_ref, wdown_ref):
    hc = jnp.concatenate([_head_norm(hs[h], normg_ref[:, h * M_DH:(h + 1) * M_DH], True) for h in range(M_HEADS)],
                         axis=1)
    hc = hc + skip_ref[...] * xc
    return _mm((hc * _silu(z)).astype(BF16), wdown_ref[...])


def _odd_prompt_kernel(x_ref, g_ref, win_ref, convw_ref, convb_ref, wq_ref, wk_ref, wv_ref,
                       wgate_t_ref, bgate_t_ref, normg_ref, skip_ref, wdown_ref,
                       y_ref, c_ref, n_ref, m_out_ref, conv_out_ref,
                       carry_ref, m_ref, xc_ref, q_ref, k_ref, v_ref, *, tt, gi):
    t_id = pl.program_id(1)
    k_scale = M_DH ** -0.5

    @pl.when(t_id == 0)
    def _():
        c_ref[...] = jnp.zeros_like(c_ref)
        n_ref[...] = jnp.zeros_like(n_ref)
        m_ref[...] = jnp.zeros_like(m_ref)
        carry_ref[...] = jnp.zeros_like(carry_ref)

    x = x_ref[0]
    hn = _rms(x, g_ref[gi:gi + 1, :]).astype(BF16)

    gates_t = bgate_t_ref[...]
    row8 = lax.broadcasted_iota(jnp.int32, (SUBLANES, M_DH), 0)
    tiles = M_DH // BD
    xm_next = _mm(hn, win_ref[:, :M_DH])
    for h in range(M_HEADS):
        sl = slice(h * M_DH, (h + 1) * M_DH)
        xm = xm_next
        if h + 1 < M_HEADS:
            xm_next = _mm(hn, win_ref[:, (h + 1) * M_DH:(h + 2) * M_DH])
        carry = carry_ref[:, sl]
        conv = convb_ref[:, sl] + convw_ref[M_CONV - 1:M_CONV, sl] * xm
        for j in range(1, M_CONV):
            rolled = pltpu.roll(xm, j, 0)
            head = jnp.where(row8 < j, pltpu.roll(carry, j, 0), rolled[:SUBLANES])
            shifted = jnp.concatenate([head, rolled[SUBLANES:]], axis=0)
            conv = conv + convw_ref[M_CONV - 1 - j:M_CONV - j, sl] * shifted
        carry_ref[:, sl] = xm[tt - SUBLANES:, :]
        conv_out_ref[0, :, sl] = xm[tt - (M_CONV - 1):, :]
        xc = _silu(conv)
        xc_ref[:, sl] = xc
        xcb = xc.astype(BF16)
        qb = _headwise(xcb, wq_ref, h * tiles).astype(BF16)
        kb = _headwise(xcb, wk_ref, h * tiles).astype(BF16)
        vb = _headwise(xm.astype(BF16), wv_ref, h * tiles).astype(BF16)
        q_ref[:, sl] = qb
        k_ref[:, sl] = kb
        v_ref[:, sl] = vb
        gates_t = gates_t + (_nt(wgate_t_ref[0, :, sl], qb) + _nt(wgate_t_ref[1, :, sl], kb)
                             + _nt(wgate_t_ref[2, :, sl], vb))

    lane8 = lax.broadcasted_iota(jnp.int32, (SUBLANES, tt), 1)
    row8t = lax.broadcasted_iota(jnp.int32, (SUBLANES, tt), 0)
    brow_all = _log_sigmoid(gates_t)
    d = 1
    while d < tt:
        brow_all = brow_all + jnp.where(lane8 >= d, pltpu.roll(brow_all, d, 1), 0.0)
        d *= 2
    rows = jnp.where(row8t < M_HEADS, gates_t, brow_all)
    pad = jnp.zeros((128 - SUBLANES, 128), F32)
    cols = jnp.concatenate([jnp.concatenate([rows[:, j * 128:(j + 1) * 128], pad], axis=0).T
                            for j in range(tt // 128)], axis=0)

    ti = lax.broadcasted_iota(jnp.int32, (tt, tt), 0)
    si = lax.broadcasted_iota(jnp.int32, (tt, tt), 1)
    causal = si <= ti
    y = x
    for h in range(M_HEADS):
        sl = slice(h * M_DH, (h + 1) * M_DH)
        qhb = q_ref[:, sl]
        khb = k_ref[:, sl]
        vhb = v_ref[:, sl]
        ig_col = cols[:, h:h + 1]
        b_col = cols[:, M_HEADS + h:M_HEADS + h + 1]
        ig_row = gates_t[h:h + 1, :]
        b_row = brow_all[M_HEADS + h:M_HEADS + h + 1, :]
        m_prev = m_ref[h:h + 1, 0:1]
        c_prev = c_ref[0, h]
        n_prev = n_ref[0, h:h + 1, :]

        dlog = jnp.where(causal, b_col + (ig_row - b_row), -jnp.inf)
        inter_log = b_col + m_prev
        m_row = jnp.maximum(inter_log, jnp.max(dlog, axis=-1, keepdims=True))
        w_inter = jnp.exp(inter_log - m_row)
        qk = _nt(qhb, khb) * (jnp.exp(dlog - m_row) * k_scale)
        num = w_inter * _mm(qhb, c_prev.astype(BF16)) + _mm(qk.astype(BF16), vhb)
        qn = _nt(qhb, jnp.broadcast_to(n_prev, (SUBLANES, M_DH)).astype(BF16))[:, 0:1]
        den = w_inter * qn + jnp.sum(qk, axis=-1, keepdims=True)
        den = jnp.maximum(jnp.abs(den), jnp.exp(-m_row))
        hh = num * (1.0 / den)

        b_end = b_col[tt - 1:tt, :]
        s_log = b_end - b_col + ig_col
        m_new = jnp.maximum(b_end + m_prev, jnp.max(s_log, axis=0, keepdims=True))
        a = jnp.exp(b_end + m_prev - m_new)
        kw = khb.astype(F32) * (jnp.exp(s_log - m_new) * k_scale)
        c_ref[0, h] = a * c_prev + _tn(kw.astype(BF16), vhb)
        n_ref[0, h:h + 1, :] = a * n_prev + jnp.sum(kw, axis=0, keepdims=True)
        m_ref[h:h + 1, :] = jnp.broadcast_to(m_new, (1, 128))

        hc = _head_norm(hh, normg_ref[:, sl], True) + skip_ref[:, sl] * xc_ref[:, sl]
        z = _mm(hn, win_ref[:, M_INNER + h * M_DH:M_INNER + (h + 1) * M_DH])
        y = y + _mm((hc * _silu(z)).astype(BF16), wdown_ref[sl, :])

    y_ref[0] = y
    m_out_ref[0] = m_ref[...]


def _odd_prompt(x, g_all, gi, ml, tt):
    bsz, seq, _ = x.shape
    bt = jnp.broadcast_to(ml['bgate_col'], (SUBLANES, tt))
    return pl.pallas_call(
        functools.partial(_odd_prompt_kernel, tt=tt, gi=gi),
        grid=(bsz, seq // tt),
        in_specs=[pl.BlockSpec((1, tt, D_MODEL), lambda b, t: (b, t, 0)),
                  _const_spec(g_all.shape),
                  _const_spec((D_MODEL, 2 * M_INNER)),
                  _const_spec((M_CONV, M_INNER)),
                  _const_spec((1, M_INNER)),
                  _const_spec((M_INNER // BD, BD, BD)),
                  _const_spec((M_INNER // BD, BD, BD)),
                  _const_spec((M_INNER // BD, BD, BD)),
                  _const_spec((3, SUBLANES, M_INNER)),
                  _const_spec((SUBLANES, tt)),
                  _const_spec((1, M_INNER)),
                  _const_spec((1, M_INNER)),
                  _const_spec((M_INNER, D_MODEL))],
        out_specs=[pl.BlockSpec((1, tt, D_MODEL), lambda b, t: (b, t, 0)),
                   pl.BlockSpec((1, M_HEADS, M_DH, M_DH), lambda b, t: (b, 0, 0, 0)),
                   pl.BlockSpec((1, M_HEADS, M_DH), lambda b, t: (b, 0, 0)),
                   pl.BlockSpec((1, SUBLANES, 128), lambda b, t: (b, 0, 0)),
                   pl.BlockSpec((1, M_CONV - 1, M_INNER), lambda b, t: (b, 0, 0))],
        out_shape=[jax.ShapeDtypeStruct(x.shape, F32),
                   jax.ShapeDtypeStruct((bsz, M_HEADS, M_DH, M_DH), F32),
                   jax.ShapeDtypeStruct((bsz, M_HEADS, M_DH), F32),
                   jax.ShapeDtypeStruct((bsz, SUBLANES, 128), F32),
                   jax.ShapeDtypeStruct((bsz, M_CONV - 1, M_INNER), F32)],
        scratch_shapes=[pltpu.VMEM((SUBLANES, M_INNER), F32), pltpu.VMEM((SUBLANES, 128), F32),
                        pltpu.VMEM((tt, M_INNER), F32), pltpu.VMEM((tt, M_INNER), BF16),
                        pltpu.VMEM((tt, M_INNER), BF16), pltpu.VMEM((tt, M_INNER), BF16)],
        compiler_params=_params(("arbitrary", "arbitrary")),
        name="odd_prompt",
    )(x, g_all, ml['win'], ml['convw'], ml['convb'], ml['wq'], ml['wk'], ml['wv'],
      ml['wgate_t'], bt, ml['normg'], ml['skip'], ml['wdown'])


def _pick_row(ref, b):
    r0 = pl.multiple_of((b // SUBLANES) * SUBLANES, SUBLANES)
    blk = ref[pl.ds(r0, SUBLANES), :]
    row = lax.broadcasted_iota(jnp.int32, blk.shape, 0)
    return jnp.sum(jnp.where(row == b % SUBLANES, blk, 0.0), axis=0, keepdims=True)


def _put_row(ref, b, val):
    r0 = pl.multiple_of((b // SUBLANES) * SUBLANES, SUBLANES)
    blk = ref[pl.ds(r0, SUBLANES), :]
    row = lax.broadcasted_iota(jnp.int32, blk.shape, 0)
    ref[pl.ds(r0, SUBLANES), :] = jnp.where(row == b % SUBLANES, jnp.broadcast_to(val, blk.shape), blk)


def _odd_sample_kernel(x_ref, g_ref, win_ref, convw_ref, convb_ref, cv_ref,
                       wq_ref, wk_ref, wv_ref, wgate_ref, bgate_ref, normg_ref, skip_ref, wdown_ref,
                       m_in_ref, c_in, n_in,
                       y_ref, c_out, n_out, m_out_ref, cv_out_ref,
                       q_ref, k_ref, v_ref, gate_ref, xc_ref, z_ref, h_ref, *, gi):
    b = pl.program_id(0)
    n_b = pl.num_programs(0)

    @pl.when(b == 0)
    def _():
        hn = _rms(x_ref[...], g_ref[gi:gi + 1, :]).astype(BF16)
        p = _mm(hn, win_ref[...])
        xm = p[:, :M_INNER]
        z_ref[...] = p[:, M_INNER:]
        cv0 = cv_ref[:, :M_INNER]
        cv1 = cv_ref[:, M_INNER:2 * M_INNER]
        cv2 = cv_ref[:, 2 * M_INNER:]
        cv_out_ref[:, :M_INNER] = cv1
        cv_out_ref[:, M_INNER:2 * M_INNER] = cv2
        cv_out_ref[:, 2 * M_INNER:] = xm
        conv = (convb_ref[...] + convw_ref[3:4, :] * xm + convw_ref[2:3, :] * cv2
                + convw_ref[1:2, :] * cv1 + convw_ref[0:1, :] * cv0)
        xc = _silu(conv)
        xc_ref[...] = xc
        xcb = xc.astype(BF16)
        q = _headwise(xcb, wq_ref)
        k = _headwise(xcb, wk_ref)
        v = _headwise(xm.astype(BF16), wv_ref)
        q_ref[...] = q
        k_ref[...] = k * (M_DH ** -0.5)
        v_ref[...] = v
        gate_ref[...] = (_mm(q.astype(BF16), wgate_ref[0]) + _mm(k.astype(BF16), wgate_ref[1])
                         + _mm(v.astype(BF16), wgate_ref[2])) + bgate_ref[...]
        m_out_ref[...] = jnp.zeros_like(m_out_ref)
        h_ref[...] = jnp.zeros_like(h_ref)

    r0 = pl.multiple_of((b // SUBLANES) * SUBLANES, SUBLANES)
    row8 = lax.broadcasted_iota(jnp.int32, (SUBLANES, M_INNER), 0)
    sel = row8 == b % SUBLANES
    q8 = jnp.where(sel, q_ref[pl.ds(r0, SUBLANES), :], 0.0)
    k8 = jnp.where(sel, k_ref[pl.ds(r0, SUBLANES), :], 0.0)
    v8 = jnp.where(sel, v_ref[pl.ds(r0, SUBLANES), :], 0.0)
    q8b = q8.astype(BF16)
    k8b = k8.astype(BF16)
    v8b = v8.astype(BF16)
    k_row = jnp.sum(k8, axis=0, keepdims=True)
    q_row = jnp.sum(q8, axis=0, keepdims=True)
    gate = _pick_row(gate_ref, b)
    m_all = _pick_row(m_in_ref, b)
    lane = lax.broadcasted_iota(jnp.int32, (1, 128), 1)
    m_new_all = jnp.zeros((1, 128), F32)
    h_parts = []
    for h in range(M_HEADS):
        sl = slice(h * M_DH, (h + 1) * M_DH)
        ig = gate[:, h:h + 1]
        lf = _log_sigmoid(gate[:, M_HEADS + h:M_HEADS + h + 1])
        m_prev = m_all[:, h:h + 1]
        m_new = jnp.maximum(lf + m_prev, ig)
        a = jnp.exp(lf + m_prev - m_new)
        ws = jnp.exp(ig - m_new)
        c_new = a * c_in[0, h] + ws * _tn(k8b[:, sl], v8b[:, sl])
        c_out[0, h] = c_new
        n_new = a * n_in[0, h:h + 1, :] + ws * k_row[:, sl]
        n_out[0, h:h + 1, :] = n_new
        num = jnp.sum(_mm(q8b[:, sl], c_new.astype(BF16)), axis=0, keepdims=True)
        den = jnp.sum(q_row[:, sl] * n_new, axis=-1, keepdims=True)
        den = jnp.maximum(jnp.abs(den), jnp.exp(-m_new))
        h_parts.append(num / den)
        m_new_all = jnp.where(lane == h, m_new, m_new_all)
    _put_row(h_ref, b, jnp.concatenate(h_parts, axis=1))
    _put_row(m_out_ref, b, m_new_all)

    @pl.when(b == n_b - 1)
    def _():
        hfull = h_ref[...]
        hs = [hfull[:, h * M_DH:(h + 1) * M_DH] for h in range(M_HEADS)]
        y_ref[...] = x_ref[...] + _mlstm_out(hs, xc_ref[...], z_ref[...], normg_ref, skip_ref, wdown_ref)


def _odd_sample(x, g_all, gi, ml, c0, n0, m0, conv0):
    n = x.shape[0]
    m_pad = jnp.pad(m0, ((0, 0), (0, 128 - M_HEADS)))
    cw = (M_CONV - 1) * M_INNER
    full = lambda shape: pl.BlockSpec(shape, lambda b: (0,) * len(shape))
    outs = pl.pallas_call(
        functools.partial(_odd_sample_kernel, gi=gi),
        grid=(n,),
        in_specs=[_const_spec((n, D_MODEL)),
                  _const_spec(g_all.shape),
                  _const_spec((D_MODEL, 2 * M_INNER)),
                  _const_spec((M_CONV, M_INNER)),
                  _const_spec((1, M_INNER)),
                  _const_spec((n, cw)),
                  _const_spec((M_INNER // BD, BD, BD)),
                  _const_spec((M_INNER // BD, BD, BD)),
                  _const_spec((M_INNER // BD, BD, BD)),
                  _const_spec((3, M_INNER, 128)),
                  _const_spec((1, 128)),
                  _const_spec((1, M_INNER)),
                  _const_spec((1, M_INNER)),
                  _const_spec((M_INNER, D_MODEL)),
                  _const_spec((n, 128)),
                  pl.BlockSpec((1, M_HEADS, M_DH, M_DH), lambda b: (b, 0, 0, 0)),
                  pl.BlockSpec((1, M_HEADS, M_DH), lambda b: (b, 0, 0))],
        out_specs=[full((n, D_MODEL)),
                   pl.BlockSpec((1, M_HEADS, M_DH, M_DH), lambda b: (b, 0, 0, 0)),
                   pl.BlockSpec((1, M_HEADS, M_DH), lambda b: (b, 0, 0)),
                   full((n, 128)),
                   full((n, cw))],
        out_shape=[jax.ShapeDtypeStruct((n, D_MODEL), F32),
                   jax.ShapeDtypeStruct(c0.shape, F32),
                   jax.ShapeDtypeStruct(n0.shape, F32),
                   jax.ShapeDtypeStruct((n, 128), F32),
                   jax.ShapeDtypeStruct((n, cw), F32)],
        scratch_shapes=[pltpu.VMEM((n, M_INNER), F32), pltpu.VMEM((n, M_INNER), F32), pltpu.VMEM((n, M_INNER), F32),
                        pltpu.VMEM((n, 128), F32), pltpu.VMEM((n, M_INNER), F32), pltpu.VMEM((n, M_INNER), F32),
                        pltpu.VMEM((n, M_INNER), F32)],
        compiler_params=_params(("arbitrary",)),
        name="odd_sample",
    )(x, g_all, ml['win'], ml['convw'], ml['convb'], conv0.reshape(n, cw),
      ml['wq'], ml['wk'], ml['wv'], ml['wgate'], ml['bgate'], ml['normg'], ml['skip'], ml['wdown'],
      m_pad, c0, n0)
    y, c, nn, m_new, conv_new = outs
    return y, c, nn, m_new[:, :M_HEADS], conv_new.reshape(n, M_CONV - 1, M_INNER)


def _odd_sample_proj_kernel(x_ref, g_ref, win_ref, convw_ref, convb_ref, cv_ref, wq_ref, wk_ref, wv_ref,
                            wgate_ref, bgate_ref, m_in_ref, n_in_ref,
                            qt_ref, kwt_ref, v_ref, a_ref, den_ref, n_out_ref, m_out_ref, cv_out_ref,
                            xc_ref, z_ref, *, gi):
    hn = _rms(x_ref[...], g_ref[gi:gi + 1, :]).astype(BF16)
    p = _mm(hn, win_ref[...])
    xm = p[:, :M_INNER]
    z_ref[...] = p[:, M_INNER:]
    cv0 = cv_ref[:, :M_INNER]
    cv1 = cv_ref[:, M_INNER:2 * M_INNER]
    cv2 = cv_ref[:, 2 * M_INNER:]
    cv_out_ref[:, :M_INNER] = cv1
    cv_out_ref[:, M_INNER:2 * M_INNER] = cv2
    cv_out_ref[:, 2 * M_INNER:] = xm
    conv = (convb_ref[...] + convw_ref[3:4, :] * xm + convw_ref[2:3, :] * cv2
            + convw_ref[1:2, :] * cv1 + convw_ref[0:1, :] * cv0)
    xc = _silu(conv)
    xc_ref[...] = xc
    xcb = xc.astype(BF16)
    q = _headwise(xcb, wq_ref)
    k = _headwise(xcb, wk_ref)
    v = _headwise(xm.astype(BF16), wv_ref)
    v_ref[...] = v
    gate = (_mm(q.astype(BF16), wgate_ref[0]) + _mm(k.astype(BF16), wgate_ref[1])
            + _mm(v.astype(BF16), wgate_ref[2])) + bgate_ref[...]
    m_in = m_in_ref[...]
    lane = lax.broadcasted_iota(jnp.int32, m_in.shape, 1)
    a_all = jnp.zeros_like(m_in)
    den_all = jnp.zeros_like(m_in)
    m_all = jnp.zeros_like(m_in)
    for h in range(M_HEADS):
        sl = slice(h * M_DH, (h + 1) * M_DH)
        ig = gate[:, h:h + 1]
        lf = _log_sigmoid(gate[:, M_HEADS + h:M_HEADS + h + 1])
        m_prev = m_in[:, h:h + 1]
        m_new = jnp.maximum(lf + m_prev, ig)
        a = jnp.exp(lf + m_prev - m_new)
        kw = (jnp.exp(ig - m_new) * (M_DH ** -0.5)) * k[:, sl]
        n_new = a * n_in_ref[:, sl] + kw
        n_out_ref[:, sl] = n_new
        qh = q[:, sl]
        den = jnp.maximum(jnp.abs(jnp.sum(qh * n_new, axis=-1, keepdims=True)), jnp.exp(-m_new))
        a_all = jnp.where(lane == h, a, a_all)
        den_all = jnp.where(lane == h, den, den_all)
        m_all = jnp.where(lane == h, m_new, m_all)
        for c in range(M_DH // 128):
            qt_ref[h, c * 128:(c + 1) * 128, :] = qh[:, c * 128:(c + 1) * 128].T
            kwt_ref[h, c * 128:(c + 1) * 128, :] = kw[:, c * 128:(c + 1) * 128].T
    a_ref[...] = a_all
    den_ref[...] = den_all
    m_out_ref[...] = m_all


def _odd_sample_proj(x, g_all, gi, ml, n0, m0, conv0):
    n = x.shape[0]
    assert n == 128, "the per-head transposes assume one 128-lane tile of sequences"
    m_pad = jnp.pad(m0, ((0, 0), (0, 128 - M_HEADS)))
    cw = (M_CONV - 1) * M_INNER
    f32 = lambda *shape: jax.ShapeDtypeStruct(shape, F32)
    shapes = [f32(M_HEADS, M_DH, n), f32(M_HEADS, M_DH, n), f32(n, M_INNER), f32(n, 128), f32(n, 128),
              f32(n, M_INNER), f32(n, 128), f32(n, cw), f32(n, M_INNER), f32(n, M_INNER)]
    args = (x, g_all, ml['win'], ml['convw'], ml['convb'], conv0.reshape(n, cw), ml['wq'], ml['wk'], ml['wv'],
            ml['wgate'], ml['bgate'], m_pad, n0.reshape(n, M_INNER))
    outs = pl.pallas_call(
        functools.partial(_odd_sample_proj_kernel, gi=gi),
        grid=(1,),
        in_specs=[_const_spec(a.shape) for a in args],
        out_specs=[pl.BlockSpec(s.shape, lambda i, nd=len(s.shape): (0,) * nd) for s in shapes],
        out_shape=shapes,
        compiler_params=_params(("arbitrary",)),
        name="odd_sample_proj",
    )(*args)
    qt, kwt, v, a, den, n_new, m_new, conv_new, xc, z = outs
    return (qt, kwt, v, a, den, n_new.reshape(n, M_HEADS, M_DH), m_new[:, :M_HEADS],
            conv_new.reshape(n, M_CONV - 1, M_INNER), xc, z)


STREAM_BUFS = 4
STREAM_ROWS = 32


def _ffn_stream_kernel(xp_ref, g_ref, gfin_ref, wg_ref, wu_ref, wd_ref, qt_ref, kwt_ref, v_ref, a_ref, c_in,
                       op_ref, c_out, hnum_ref, cin_buf, cout_buf, sem_in, sem_out, *, gi, per_step):
    i = pl.program_id(0)
    n_pieces = per_step * M_HEADS

    def seq(p):
        return i * per_step + p // M_HEADS

    def in_copy(p):
        slot = p % STREAM_BUFS
        return pltpu.make_async_copy(c_in.at[seq(p), p % M_HEADS], cin_buf.at[slot], sem_in.at[slot])

    def out_copy(p):
        slot = p % STREAM_BUFS
        return pltpu.make_async_copy(cout_buf.at[slot], c_out.at[seq(p), p % M_HEADS], sem_out.at[slot])

    @pl.when(i == 0)
    def _():
        hnum_ref[...] = jnp.zeros_like(hnum_ref)

    for p in range(STREAM_BUFS):
        in_copy(p).start()

    shift = (128 - (i * per_step) % 128) % 128
    rolled = {}

    def columns(ref, h):
        if (id(ref), h) not in rolled:
            rolled[(id(ref), h)] = pltpu.roll(ref[h], shift, 1)
        return rolled[(id(ref), h)]

    rows = {}

    def seq_rows(j):
        if j not in rows:
            s = i * per_step + j
            rows[j] = (_pick_row(a_ref, s), _pick_row(v_ref, s))
        return rows[j]

    h_parts = {}

    def update(p):
        j, h = divmod(p, M_HEADS)
        slot = p % STREAM_BUFS
        a_row, v_row = seq_rows(j)
        kw_col = columns(kwt_ref, h)[:, j:j + 1]
        q_col = columns(qt_ref, h)[:, j:j + 1]
        a = a_row[:, h:h + 1]
        v_h = v_row[:, h * M_DH:(h + 1) * M_DH]
        acc = jnp.zeros((STREAM_ROWS, M_DH), F32)
        for r0 in range(0, M_DH, STREAM_ROWS):
            rs = slice(r0, r0 + STREAM_ROWS)
            c_new = a * cin_buf[slot, rs, :] + kw_col[rs] * v_h
            cout_buf[slot, rs, :] = c_new
            acc = acc + q_col[rs] * c_new
        h_parts[(j, h)] = jnp.sum(acc, axis=0, keepdims=True)
        if h == M_HEADS - 1:
            _put_row(hnum_ref, i * per_step + j, jnp.concatenate([h_parts[(j, hh)] for hh in range(M_HEADS)], axis=1))

    def region(first, matmul):
        ps = (first, first + 1)
        for p in ps:
            in_copy(p).wait()
            if p >= STREAM_BUFS:
                out_copy(p - STREAM_BUFS).wait()
        out = matmul()
        for p in ps:
            update(p)
        for p in ps:
            out_copy(p).start()
            if p + STREAM_BUFS < n_pieces:
                in_copy(p + STREAM_BUFS).start()
        return out

    x = xp_ref[...]
    hb = _rms(x, g_ref[gi:gi + 1, :]).astype(BF16)
    y = None
    first = 0
    for lo, hi in zip(FFN_BOUNDS[:-1], FFN_BOUNDS[1:]):
        half = lo + (hi - lo + MXU_TILE) // (2 * MXU_TILE) * MXU_TILE
        parts = []
        for w_ref in (wg_ref, wu_ref):
            for c0, c1 in ((lo, half), (half, hi)):
                parts.append(region(first, lambda w_ref=w_ref, c0=c0, c1=c1: _mm(hb, w_ref[:, c0:c1])))
                first += 2
        gt = jnp.concatenate(parts[:2], axis=1)
        ut = jnp.concatenate(parts[2:], axis=1)
        d = _mm((_silu(gt) * ut).astype(BF16), wd_ref[lo:hi, :])
        y = d if y is None else y + d
    assert first == n_pieces
    op_ref[...] = _rms(x + 0.5 * y, gfin_ref[...])
    for p in range(n_pieces - STREAM_BUFS, n_pieces):
        out_copy(p).wait()


def _ffn_stream(xp, g_all, gfin, wg, wu, wd, gi, tm, qt, kwt, v, a, c0):
    n = xp.shape[0]
    ns = v.shape[0]
    steps = n // tm
    per_step = ns // steps
    assert per_step * steps == ns and 128 % per_step == 0
    any_spec = pl.BlockSpec(memory_space=pl.ANY)
    return pl.pallas_call(
        functools.partial(_ffn_stream_kernel, gi=gi, per_step=per_step),
        grid=(steps,),
        in_specs=[pl.BlockSpec((tm, D_MODEL), lambda i: (i, 0)),
                  _const_spec(g_all.shape),
                  _const_spec((1, D_MODEL)),
                  _const_spec((D_MODEL, D_FF)), _const_spec((D_MODEL, D_FF)), _const_spec((D_FF, D_MODEL)),
                  _const_spec(qt.shape), _const_spec(kwt.shape), _const_spec(v.shape), _const_spec(a.shape),
                  any_spec],
        out_specs=[pl.BlockSpec((tm, D_MODEL), lambda i: (i, 0)),
                   any_spec,
                   pl.BlockSpec((ns, M_INNER), lambda i: (0, 0))],
        out_shape=[jax.ShapeDtypeStruct((n, D_MODEL), F32),
                   jax.ShapeDtypeStruct(c0.shape, F32),
                   jax.ShapeDtypeStruct((ns, M_INNER), F32)],
        scratch_shapes=[pltpu.VMEM((STREAM_BUFS, M_DH, M_DH), F32), pltpu.VMEM((STREAM_BUFS, M_DH, M_DH), F32),
                        pltpu.SemaphoreType.DMA((STREAM_BUFS,)), pltpu.SemaphoreType.DMA((STREAM_BUFS,))],
        compiler_params=_params(("arbitrary",)),
        name="ffn_final_stream",
    )(xp, g_all, gfin, wg, wu, wd, qt, kwt, v, a, c0)


def _odd_sample_out_kernel(x_ref, hnum_ref, den_ref, xc_ref, z_ref, normg_ref, skip_ref, wdown_ref,
                           g_ref, gfin_ref, wg_ref, wu_ref, wd_ref, y_ref, *, gi):
    hnum = hnum_ref[...]
    den = den_ref[...]
    hs = [hnum[:, h * M_DH:(h + 1) * M_DH] * (1.0 / den[:, h:h + 1]) for h in range(M_HEADS)]
    xs = x_ref[...] + _mlstm_out(hs, xc_ref[...], z_ref[...], normg_ref, skip_ref, wdown_ref)
    y_ref[...] = _ffn_rows(xs, g_ref[gi:gi + 1, :], wg_ref, wu_ref, wd_ref, gfin_ref[...])


def _odd_sample_out(x, hnum, den, xc, z, ml, g_all, gfin, wg, wu, wd, gi):
    args = (x, hnum, den, xc, z, ml['normg'], ml['skip'], ml['wdown'], g_all, gfin, wg, wu, wd)
    return pl.pallas_call(
        functools.partial(_odd_sample_out_kernel, gi=gi),
        grid=(1,),
        in_specs=[_const_spec(a.shape) for a in args],
        out_specs=pl.BlockSpec(x.shape, lambda i: (0, 0)),
        out_shape=jax.ShapeDtypeStruct(x.shape, F32),
        compiler_params=_params(("arbitrary",)),
        name="odd_sample_out",
    )(*args)


TM_FFN = 512
TT_EVEN = 256
TT_ODD = 512


def kernel(x_prompt, x_sample, state_ret, state_hgrn, state_mlstm_C, state_mlstm_n, state_mlstm_m, state_mlstm_conv,
           norm_g, final_norm_g, ffn_w_gate, ffn_w_up, ffn_w_down, ev_w_in, ev_w_out, ret_norm_g, hg_norm_g,
           hg_lb_logits, ml_w_in, ml_conv_w, ml_conv_b, ml_w_q, ml_w_k, ml_w_v, ml_w_ig, ml_b_ig, ml_w_fg,
           ml_b_fg, ml_norm_g, ml_skip, ml_w_down):
    bp, tp, _ = x_prompt.shape
    ns = x_sample.shape[0]

    g_all = norm_g.reshape(-1, D_MODEL)
    gfin = final_norm_g.reshape(1, D_MODEL)
    lb_all = jnp.cumsum(jax.nn.softmax(hg_lb_logits.astype(F32), axis=0), axis=0)
    lb = lb_all[0].reshape(1, G_HEADS * G_EXP)
    retg = ret_norm_g[0]
    hgg = hg_norm_g[0]
    wgate = _gate_weights(ml_w_ig[0], ml_w_fg[0])
    bgate = jnp.pad(jnp.concatenate([ml_b_ig[0], ml_b_fg[0]]), (0, 128 - 2 * M_HEADS)).reshape(1, 128)
    ml = {
        'convw': ml_conv_w[0],
        'convb': ml_conv_b[0].reshape(1, M_INNER),
        'wq': _block_diag(ml_w_q[0]).astype(BF16),
        'wk': _block_diag(ml_w_k[0]).astype(BF16),
        'wv': _block_diag(ml_w_v[0]).astype(BF16),
        'wgate': wgate.astype(BF16),
        'wgate_t': jnp.swapaxes(wgate[:, :, :SUBLANES], 1, 2).astype(BF16),
        'bgate': bgate,
        'bgate_col': bgate[0, :SUBLANES].reshape(SUBLANES, 1),
        'normg': ml_norm_g[0].reshape(1, M_INNER),
        'skip': ml_skip[0].reshape(1, M_INNER),
    }

    xp = x_prompt.reshape(bp * tp, D_MODEL)
    xs = x_sample.reshape(ns, D_MODEL)
    ffn_w = (ffn_w_gate, ffn_w_up, ffn_w_down)
    next_ffn = lambda layer, idx: [(w, (layer, idx)) for w in ffn_w]

    w00 = [w[0, 0].astype(BF16) for w in ffn_w]
    xp, xs, cast = _ffn(xp, xs, g_all, gfin, *w00, 0, TM_FFN,
                        casts=next_ffn(0, 1) + [(ev_w_in, (0,)), (ev_w_out, (0,))])
    w01, (ev_in, ev_out) = cast[:3], cast[3:]
    xp, ret_p, hg_p = _even_prompt(xp.reshape(bp, tp, D_MODEL), g_all, 1, ev_in, ev_out, retg, hgg, lb, TT_EVEN)
    xs, ret_s, hg_s = _even_sample(xs, g_all, 1, ev_in, ev_out, retg, hgg, lb, state_ret[:, 0], state_hgrn[:, 0])
    xp, xs, cast = _ffn(xp.reshape(bp * tp, D_MODEL), xs, g_all, gfin, *w01, 2, TM_FFN,
                        casts=next_ffn(1, 0) + [(ml_w_in, (0,)), (ml_w_down, (0,))])
    w10, (ml['win'], ml['wdown']) = cast[:3], cast[3:]
    xp, xs, w11 = _ffn(xp, xs, g_all, gfin, *w10, 3, TM_FFN, casts=next_ffn(1, 1))
    xp, c_p, n_p, m_p, conv_p = _odd_prompt(xp.reshape(bp, tp, D_MODEL), g_all, 4, ml, TT_ODD)
    qt, kwt, v_s, a_s, den_s, n_s, m_s, conv_s, xc_s, z_s = _odd_sample_proj(
        xs, g_all, 4, ml, state_mlstm_n[:, 0], state_mlstm_m[:, 0], state_mlstm_conv[:, 0])
    y_p, c_s, hnum = _ffn_stream(xp.reshape(bp * tp, D_MODEL), g_all, gfin, *w11, 5, TM_FFN,
                                 qt, kwt, v_s, a_s, state_mlstm_C[:, 0])
    y_s = _odd_sample_out(xs, hnum, den_s, xc_s, z_s, ml, g_all, gfin, *w11, 5)

    return (y_p.reshape(bp, tp, D_MODEL), y_s.reshape(ns, 1, D_MODEL),
            ret_p[:, None], hg_p[:, None], c_p[:, None], n_p[:, None], m_p[:, None, :M_HEADS, 0], conv_p[:, None],
            ret_s[:, None], hg_s[:, None], c_s[:, None], n_s[:, None], m_s[:, None], conv_s[:, None])
```

```python
import functools
import math

import jax
import jax.numpy as jnp
from jax import lax
from jax.experimental import pallas as pl
from jax.experimental.pallas import tpu as pltpu

D_MODEL = 1024
PAST_LEN = 16384
R_HEADS = 4
R_DK = 128
R_DV = 128
G_HEADS = 4
G_EXP = 128
G_DV = 128
M_INNER = 2 * D_MODEL
M_HEADS = 4
M_DH = M_INNER // M_HEADS
M_CONV = 4
QKV_BLOCK = 4
D_FF = 2816
EPS = 1e-6
ROPE_BASE = 10000.0
EVEN_IN = 4096
EVEN_OUT = 1024

F32 = jnp.float32
BF16 = jnp.bfloat16

VMEM_LIMIT_BYTES = 56 * 1024 * 1024

HG_CHUNK = 64
HG_SUB = 16
HG_SAFE_LOG_DECAY = -60.0
MXU_TILE = 256
FFN_BOUNDS = (0, 6 * MXU_TILE, D_FF)
BD = MXU_TILE
SUBLANES = 8


def _nt(a, b):
    return lax.dot_general(a, b, (((1,), (1,)), ((), ())), preferred_element_type=F32)


def _tn(a, b):
    return lax.dot_general(a, b, (((0,), (0,)), ((), ())), preferred_element_type=F32)


def _mm(a, b):
    return jnp.dot(a, b, preferred_element_type=F32)


def _sigmoid(x):
    return 1.0 / (1.0 + jnp.exp(-x))


def _silu(x):
    return x * _sigmoid(x)


def _log_sigmoid(x):
    return jnp.minimum(x, 0.0) - jnp.log(1.0 + jnp.exp(-jnp.abs(x)))


def _rms(x, g):
    return x * lax.rsqrt(jnp.mean(x * x, axis=-1, keepdims=True) + EPS) * g


def _head_norm(x, g, center):
    if center:
        x = x - jnp.mean(x, axis=-1, keepdims=True)
    return x * lax.rsqrt(jnp.mean(x * x, axis=-1, keepdims=True) + EPS) * g


def _rotary(x, cos, sin_signed):
    return x * cos + pltpu.roll(x, 64, 1) * sin_signed


def _const_spec(shape):
    n = len(shape)
    return pl.BlockSpec(shape, lambda *_: (0,) * n, pipeline_mode=pl.Buffered(1))


def _params(sem):
    return pltpu.CompilerParams(dimension_semantics=sem, vmem_limit_bytes=VMEM_LIMIT_BYTES)


def _ffn_rows(x, g, wg_ref, wu_ref, wd_ref, gfin):
    h = _rms(x, g).astype(BF16)
    y = jnp.zeros_like(x)
    for lo, hi in zip(FFN_BOUNDS[:-1], FFN_BOUNDS[1:]):
        gt = _mm(h, wg_ref[:, lo:hi])
        ut = _mm(h, wu_ref[:, lo:hi])
        a = (_silu(gt) * ut).astype(BF16)
        y = y + _mm(a, wd_ref[lo:hi, :])
    out = x + 0.5 * y
    if gfin is not None:
        out = _rms(out, gfin)
    return out


def _ffn_kernel(xp_ref, xs_ref, g_ref, gfin_ref, wg_ref, wu_ref, wd_ref, *rest, gi, final, n_cast):
    cast_in = rest[:n_cast]
    op_ref, os_ref = rest[n_cast:n_cast + 2]
    cast_out = rest[n_cast + 2:]
    g = g_ref[gi:gi + 1, :]
    gfin = gfin_ref[...] if final else None
    op_ref[...] = _ffn_rows(xp_ref[...], g, wg_ref, wu_ref, wd_ref, gfin)
    for src, dst in zip(cast_in, cast_out):
        dst[...] = src[...].astype(BF16)

    @pl.when(pl.program_id(0) == pl.num_programs(0) - 1)
    def _():
        os_ref[...] = _ffn_rows(xs_ref[...], g, wg_ref, wu_ref, wd_ref, gfin)


BF16_ROWS = 16


def _cast_specs(arr, lead, steps):
    rows, cols = arr.shape[-2:]
    per = 1 if (rows // steps) % BF16_ROWS == 0 else 2
    br = rows * per // steps
    in_spec = pl.BlockSpec((None,) * len(lead) + (br, cols), lambda i: tuple(lead) + (i // per, 0))
    out_spec = pl.BlockSpec((br, cols), lambda i: (i // per, 0))
    return in_spec, out_spec, jax.ShapeDtypeStruct((rows, cols), BF16)


def _ffn(xp, xs, g_all, gfin, wg, wu, wd, gi, tm, final=False, casts=()):
    n = xp.shape[0]
    ns = xs.shape[0]
    steps = n // tm
    cast_specs = [_cast_specs(arr, lead, steps) for arr, lead in casts]
    outs = pl.pallas_call(
        functools.partial(_ffn_kernel, gi=gi, final=final, n_cast=len(casts)),
        grid=(steps,),
        in_specs=[pl.BlockSpec((tm, D_MODEL), lambda i: (i, 0)),
                  _const_spec((ns, D_MODEL)),
                  _const_spec(g_all.shape),
                  _const_spec((1, D_MODEL)),
                  _const_spec((D_MODEL, D_FF)), _const_spec((D_MODEL, D_FF)), _const_spec((D_FF, D_MODEL))]
                 + [c[0] for c in cast_specs],
        out_specs=[pl.BlockSpec((tm, D_MODEL), lambda i: (i, 0)),
                   pl.BlockSpec((ns, D_MODEL), lambda i: (0, 0))] + [c[1] for c in cast_specs],
        out_shape=[jax.ShapeDtypeStruct((n, D_MODEL), F32), jax.ShapeDtypeStruct((ns, D_MODEL), F32)]
                  + [c[2] for c in cast_specs],
        compiler_params=_params(("arbitrary",)),
        name="ffn_final" if final else "ffn",
    )(xp, xs, g_all, gfin, wg, wu, wd, *[arr for arr, _ in casts])
    return outs[0], outs[1], outs[2:]


def _ret_log_gamma(h):
    return math.log(1.0 - 2.0 ** (-5.0 - h))


def _head_parts(p, mixer, h):
    base = mixer * 4 * R_HEADS * R_DK
    return [p[:, base + (j * R_HEADS + h) * R_DK:base + (j * R_HEADS + h + 1) * R_DK] for j in range(4)]


def _hgrn_gates(gq, gf, lb):
    f = lb + (1.0 - lb) * _sigmoid(gf)
    kk = (1.0 - lb) * _sigmoid(-gf)
    qq = _silu(gq)
    return qq, kk, f


def _split3(x):
    hi = x.astype(BF16)
    r1 = x - hi.astype(F32)
    mid = r1.astype(BF16)
    lo = (r1 - mid.astype(F32)).astype(BF16)
    return hi, mid, lo


def _shift_rows(bases, d):
    base = bases[d % SUBLANES]
    full = (d // SUBLANES) * SUBLANES
    return pltpu.roll(base, full, 0) if full else base


def _hgrn_tile_factorised(qq, kk, vv, b, st):
    tt = qq[0].shape[0]
    heads = range(len(qq))
    qx = [(qq[h] * jnp.exp(b[h])).astype(BF16) for h in heads]
    kx = [kk[h] * jnp.exp(-b[h]) for h in heads]
    kxb = [kx[h].astype(BF16) for h in heads]
    vb = [vv[h].astype(BF16) for h in heads]
    ti = lax.broadcasted_iota(jnp.int32, (HG_CHUNK, HG_CHUNK), 0)
    si = lax.broadcasted_iota(jnp.int32, (HG_CHUNK, HG_CHUNK), 1)
    causal = si <= ti
    st = list(st)
    o_chunks = [[] for _ in heads]
    for c in range(tt // HG_CHUNK):
        rs = slice(c * HG_CHUNK, (c + 1) * HG_CHUNK)
        for h in heads:
            a = jnp.where(causal, _nt(qx[h][rs], kxb[h][rs]), 0.0)
            o_chunks[h].append(_mm(a.astype(BF16), vb[h][rs]) + _nt(qx[h][rs], st[h].astype(BF16)))
            etot = jnp.exp(b[h][(c + 1) * HG_CHUNK - 1:(c + 1) * HG_CHUNK])
            st[h] = st[h] * etot + _tn(vb[h][rs], (kx[h][rs] * etot).astype(BF16))
    return [jnp.concatenate(o_chunks[h], axis=0) for h in heads], st


def _hgrn_tile_guarded(qq_all, kk_all, f_all, vv_all, b_all, st_all):
    tt = qq_all.shape[0]
    hd = G_EXP
    row = lax.broadcasted_iota(jnp.int32, (tt, hd), 0)
    rsub = row % HG_SUB
    sub = (lax.broadcasted_iota(jnp.int32, (HG_CHUNK, hd), 0)) // HG_SUB
    n_sub = HG_CHUNK // HG_SUB
    outs, states = [], []
    for h in range(G_HEADS):
        hs = slice(h * hd, (h + 1) * hd)
        qq = qq_all[:, hs]
        kk = kk_all[:, hs]
        ff = f_all[:, hs]
        bc_all = b_all[:, hs]
        vv = vv_all[:, hs]
        f_sh = [ff] + [pltpu.roll(ff, r, 0) for r in range(1, SUBLANES)]
        k_sh = [kk] + [pltpu.roll(kk, r, 0) for r in range(1, SUBLANES)]
        v_sh = [vv] + [pltpu.roll(vv, r, 0) for r in range(1, SUBLANES)]
        o_band = jnp.sum(qq * kk, axis=-1, keepdims=True) * vv
        decay = None
        for d in range(1, HG_SUB):
            fd = _shift_rows(f_sh, d - 1)
            decay = jnp.where(rsub >= d, fd if decay is None else decay * fd, 0.0)
            term = qq * _shift_rows(k_sh, d) * decay
            o_band = o_band + jnp.sum(term, axis=-1, keepdims=True) * _shift_rows(v_sh, d)
        st = st_all[h]
        o_chunks = []
        for c in range(tt // HG_CHUNK):
            r0 = c * HG_CHUNK
            bc = bc_all[r0:r0 + HG_CHUNK]
            qc = qq[r0:r0 + HG_CHUNK]
            kc = kk[r0:r0 + HG_CHUNK]
            vcb = vv[r0:r0 + HG_CHUNK].astype(BF16)
            refs = [bc[i * HG_SUB - 1:i * HG_SUB] for i in range(1, n_sub)]
            refrow = refs[-1]
            for i in range(n_sub - 2, 0, -1):
                refrow = jnp.where(sub == i, refs[i - 1], refrow)
            qp = qc * jnp.exp(bc - refrow)
            lhs = jnp.concatenate([jnp.where(sub == i, qp, 0.0) for i in range(1, n_sub)], axis=1)
            kcat = jnp.concatenate([jnp.where(sub < i, kc * jnp.exp(refs[i - 1] - bc), 0.0)
                                    for i in range(1, n_sub)], axis=1)
            a = _nt(lhs.astype(BF16), kcat.astype(BF16))
            qb = (qc * jnp.exp(bc)).astype(BF16)
            o_chunks.append(_mm(a.astype(BF16), vcb) + _nt(qb, st.astype(BF16)))
            btot = bc[HG_CHUNK - 1:HG_CHUNK]
            ke = (kc * jnp.exp(btot - bc)).astype(BF16)
            st = st * jnp.exp(btot) + _tn(vcb, ke)
        outs.append(o_band + jnp.concatenate(o_chunks, axis=0))
        states.append(st)
    return jnp.concatenate(outs, axis=1), jnp.stack(states)


def _even_prompt_kernel(x_ref, g_ref, win_ref, wout_ref, cos_ref, sin_ref, retg_ref, hgg_ref, lb_ref,
                        y_ref, sret_ref, shg_ref, dmat_ref, tri_ref, st_ref, ohg_ref, *, tt, gi):
    b_id = pl.program_id(0)
    t_id = pl.program_id(1)
    n_t = pl.num_programs(1)
    hd = 128

    @pl.when(jnp.logical_and(b_id == 0, t_id == 0))
    def _():
        ti = lax.broadcasted_iota(jnp.int32, (tt, tt), 0)
        si = lax.broadcasted_iota(jnp.int32, (tt, tt), 1)
        diff = (ti - si).astype(F32)
        for h in range(R_HEADS):
            dmat_ref[h] = jnp.where(diff >= 0.0, jnp.exp(_ret_log_gamma(h) * jnp.maximum(diff, 0.0)), 0.0)
        same_chunk = (ti // HG_CHUNK) == (si // HG_CHUNK)
        tri_ref[...] = jnp.where(jnp.logical_and(same_chunk, si <= ti), 1.0, 0.0).astype(BF16)

    @pl.when(t_id == 0)
    def _():
        sret_ref[...] = jnp.zeros_like(sret_ref)
        st_ref[...] = jnp.zeros_like(st_ref)

    x = x_ref[0]
    hn = _rms(x, g_ref[gi:gi + 1, :]).astype(BF16)
    cos = cos_ref[...]
    sin = sin_ref[...]
    rowf = lax.broadcasted_iota(jnp.int32, (tt, hd), 0).astype(F32)
    outs = []
    hg = []
    logf_parts = []
    st_old = st_ref[...]

    p = _mm(hn, win_ref[...])
    for i in range(R_HEADS + G_HEADS):
        h = i // 2
        pa, pb, pc, pd = _head_parts(p, i % 2, h)
        if i % 2 == 0:
            lg = _ret_log_gamma(h)
            q = _rotary(pa, cos, sin)
            k = _rotary(pb, cos, sin) * (R_DK ** -0.5)
            v = pc
            rg = pd
            s0 = sret_ref[0, h]
            qb = q.astype(BF16)
            vb = v.astype(BF16)
            inter = _mm(qb, s0.astype(BF16)) * jnp.exp(lg * (rowf + 1.0))
            scores = _nt(qb, k.astype(BF16)) * dmat_ref[h]
            intra = _mm(scores.astype(BF16), vb)
            kd = (k * jnp.exp(lg * (tt - 1.0 - rowf))).astype(BF16)
            sret_ref[0, h] = math.exp(lg * tt) * s0 + _tn(kd, vb)
            outs.append(_head_norm(inter + intra, retg_ref[h:h + 1, :], False) * _silu(rg))
        else:
            qq, kk, ff = _hgrn_gates(pa, pb, lb_ref[:, h * hd:(h + 1) * hd])
            hg.append((qq, kk, ff, pc, _sigmoid(pd)))
            logf_parts.extend(_split3(jnp.log(ff)))

    cs = _mm(tri_ref[...], jnp.concatenate(logf_parts, axis=1))
    b = [cs[:, (3 * h) * hd:(3 * h + 1) * hd] + cs[:, (3 * h + 1) * hd:(3 * h + 2) * hd]
         + cs[:, (3 * h + 2) * hd:(3 * h + 3) * hd] for h in range(G_HEADS)]
    b_all = jnp.concatenate(b, axis=1)
    o_fast, st_fast = _hgrn_tile_factorised([t[0] for t in hg], [t[1] for t in hg], [t[3] for t in hg], b,
                                            [st_old[h] for h in range(G_HEADS)])
    for h in range(G_HEADS):
        ohg_ref[:, h * hd:(h + 1) * hd] = o_fast[h]
        st_ref[h] = st_fast[h]

    @pl.when(jnp.min(b_all) < HG_SAFE_LOG_DECAY)
    def _():
        cat = lambda j: jnp.concatenate([t[j] for t in hg], axis=1)
        o_safe, st_safe = _hgrn_tile_guarded(cat(0), cat(1), cat(2), cat(3), b_all, st_old)
        ohg_ref[...] = o_safe
        st_ref[...] = st_safe

    for h in range(G_HEADS):
        outs.append(_head_norm(ohg_ref[:, h * hd:(h + 1) * hd], hgg_ref[h:h + 1, :], False) * hg[h][4])


    ycat = jnp.concatenate(outs, axis=1).astype(BF16)
    y_ref[0] = x + _mm(ycat, wout_ref[...])

    @pl.when(t_id == n_t - 1)
    def _():
        for h in range(G_HEADS):
            shg_ref[0, h] = st_ref[h].T


def _rope_tables(pos):
    half = R_DK // 2
    inv = ROPE_BASE ** (-jnp.arange(half, dtype=F32) / half)
    ang = pos.astype(F32)[:, None] * inv[None, :]
    cos = jnp.cos(ang)
    sin = jnp.sin(ang)
    return jnp.concatenate([cos, cos], axis=-1), jnp.concatenate([-sin, sin], axis=-1)


def _even_prompt(x, g_all, gi, win, wout, retg, hgg, lb, tt):
    bsz, seq, _ = x.shape
    cos, sin = _rope_tables(jnp.arange(seq, dtype=jnp.int32))
    state_spec = pl.BlockSpec((1, 4, 128, 128), lambda b, t: (b, 0, 0, 0))
    return pl.pallas_call(
        functools.partial(_even_prompt_kernel, tt=tt, gi=gi),
        grid=(bsz, seq // tt),
        in_specs=[pl.BlockSpec((1, tt, D_MODEL), lambda b, t: (b, t, 0)),
                  _const_spec(g_all.shape),
                  _const_spec((D_MODEL, EVEN_IN)),
                  _const_spec((EVEN_OUT, D_MODEL)),
                  pl.BlockSpec((tt, 128), lambda b, t: (t, 0)),
                  pl.BlockSpec((tt, 128), lambda b, t: (t, 0)),
                  _const_spec((4, 128)), _const_spec((4, 128)), _const_spec((1, 512))],
        out_specs=[pl.BlockSpec((1, tt, D_MODEL), lambda b, t: (b, t, 0)), state_spec, state_spec],
        out_shape=[jax.ShapeDtypeStruct(x.shape, F32),
                   jax.ShapeDtypeStruct((bsz, 4, 128, 128), F32),
                   jax.ShapeDtypeStruct((bsz, 4, 128, 128), F32)],
        scratch_shapes=[pltpu.VMEM((4, tt, tt), F32), pltpu.VMEM((tt, tt), BF16), pltpu.VMEM((4, 128, 128), F32),
                        pltpu.VMEM((tt, G_HEADS * G_DV), F32)],
        compiler_params=_params(("arbitrary", "arbitrary")),
        name="even_prompt",
    )(x, g_all, win, wout, cos, sin, retg, hgg, lb)


SB = 8


def _even_sample_kernel(x_ref, g_ref, win_ref, wout_ref, cos_ref, sin_ref, retg_ref, hgg_ref, lb_ref,
                        sret_in, shg_in, y_ref, sret_out, shg_out, p_ref, o_ref, *, gi):
    i = pl.program_id(0)
    n_i = pl.num_programs(0)
    hd = 128

    @pl.when(i == 0)
    def _():
        hn = _rms(x_ref[...], g_ref[gi:gi + 1, :]).astype(BF16)
        p_ref[...] = _mm(hn, win_ref[...])

    r0 = pl.multiple_of(i * SB, SB)
    p = p_ref[pl.ds(r0, SB), :]
    cos = cos_ref[...]
    sin = sin_ref[...]
    row = lax.broadcasted_iota(jnp.int32, (SB, hd), 0)
    outs = []
    for h in range(R_HEADS):
        gamma = math.exp(_ret_log_gamma(h))
        pa, pb, pc, rg = _head_parts(p, 0, h)
        q = _rotary(pa, cos, sin).astype(BF16)
        k = _rotary(pb, cos, sin) * (R_DK ** -0.5)
        vb = pc.astype(BF16)
        o = jnp.zeros((SB, hd), F32)
        for j in range(SB):
            kj = jnp.where(row == j, k, 0.0).astype(BF16)
            s_new = gamma * sret_in[j, h] + _tn(kj, vb)
            sret_out[j, h] = s_new
            o = jnp.where(row == j, _mm(q, s_new.astype(BF16)), o)
        outs.append(_head_norm(o, retg_ref[h:h + 1, :], False) * _silu(rg))
    for h in range(G_HEADS):
        pa, pb, pc, gg = _head_parts(p, 1, h)
        qq, kk, ff = _hgrn_gates(pa, pb, lb_ref[:, h * hd:(h + 1) * hd])
        vb = pc.astype(BF16)
        f_cols = jnp.concatenate([ff, jnp.zeros((hd - SB, hd), F32)], axis=0).T
        qb = qq.astype(BF16)
        o = jnp.zeros((SB, hd), F32)
        for j in range(SB):
            kj = jnp.where(row == j, kk, 0.0).astype(BF16)
            s_new = f_cols[:, j:j + 1] * shg_in[j, h] + _tn(kj, vb)
            shg_out[j, h] = s_new
            o = jnp.where(row == j, _mm(qb, s_new.astype(BF16)), o)
        outs.append(_head_norm(o, hgg_ref[h:h + 1, :], False) * _sigmoid(gg))
    o_ref[pl.ds(r0, SB), :] = jnp.concatenate(outs, axis=1)

    @pl.when(i == n_i - 1)
    def _():
        y_ref[...] = x_ref[...] + _mm(o_ref[...].astype(BF16), wout_ref[...])


def _even_sample(x, g_all, gi, win, wout, retg, hgg, lb, sret, shg):
    n = x.shape[0]
    cos, sin = _rope_tables(jnp.full((1,), PAST_LEN, dtype=jnp.int32))
    state_spec = pl.BlockSpec((SB, 4, 128, 128), lambda i: (i, 0, 0, 0))
    return pl.pallas_call(
        functools.partial(_even_sample_kernel, gi=gi),
        grid=(n // SB,),
        in_specs=[_const_spec((n, D_MODEL)),
                  _const_spec(g_all.shape),
                  _const_spec((D_MODEL, EVEN_IN)),
                  _const_spec((EVEN_OUT, D_MODEL)),
                  _const_spec((1, 128)), _const_spec((1, 128)),
                  _const_spec((4, 128)), _const_spec((4, 128)), _const_spec((1, 512)),
                  state_spec, state_spec],
        out_specs=[pl.BlockSpec((n, D_MODEL), lambda i: (0, 0)), state_spec, state_spec],
        out_shape=[jax.ShapeDtypeStruct((n, D_MODEL), F32),
                   jax.ShapeDtypeStruct(sret.shape, F32),
                   jax.ShapeDtypeStruct(shg.shape, F32)],
        scratch_shapes=[pltpu.VMEM((n, EVEN_IN), F32), pltpu.VMEM((n, EVEN_OUT), F32)],
        compiler_params=_params(("arbitrary",)),
        name="even_sample",
    )(x, g_all, win, wout, cos, sin, retg, hgg, lb, sret, shg)


def _block_diag(w):
    wr = w.reshape(M_INNER // BD, BD, QKV_BLOCK)
    tiled = jnp.tile(wr, (1, 1, BD // QKV_BLOCK))
    rb = lax.broadcasted_iota(jnp.int32, (BD, BD), 0) // QKV_BLOCK
    cb = lax.broadcasted_iota(jnp.int32, (BD, BD), 1) // QKV_BLOCK
    return jnp.where((rb == cb)[None], tiled, 0.0)


def _headwise(xb, w_ref, g0=0):
    return jnp.concatenate([_mm(xb[:, g * BD:(g + 1) * BD], w_ref[g0 + g]) for g in range(xb.shape[1] // BD)], axis=1)


def _gate_weights(w_ig, w_fg):
    w = jnp.concatenate([w_ig, w_fg], axis=1)
    w = jnp.pad(w, ((0, 0), (0, 128 - 2 * M_HEADS)))
    return w.reshape(3, M_INNER, 128)


def _mlstm_out(hs, xc, z, normg_ref, skip_ref, wdown_ref):
    hc = jnp.concatenate([_head_norm(hs[h], normg_ref[:, h * M_DH:(h + 1) * M_DH], True) for h in range(M_HEADS)],
                         axis=1)
    hc = hc + skip_ref[...] * xc
    return _mm((hc * _silu(z)).astype(BF16), wdown_ref[...])


def _odd_prompt_kernel(x_ref, g_ref, win_ref, convw_ref, convb_ref, wq_ref, wk_ref, wv_ref,
                       wgate_t_ref, bgate_t_ref, normg_ref, skip_ref, wdown_ref,
                       y_ref, c_ref, n_ref, m_out_ref, conv_out_ref,
                       carry_ref, m_ref, xc_ref, q_ref, k_ref, v_ref, *, tt, gi):
    t_id = pl.program_id(1)
    k_scale = M_DH ** -0.5

    @pl.when(t_id == 0)
    def _():
        c_ref[...] = jnp.zeros_like(c_ref)
        n_ref[...] = jnp.zeros_like(n_ref)
        m_ref[...] = jnp.zeros_like(m_ref)
        carry_ref[...] = jnp.zeros_like(carry_ref)

    x = x_ref[0]
    hn = _rms(x, g_ref[gi:gi + 1, :]).astype(BF16)

    gates_t = bgate_t_ref[...]
    row8 = lax.broadcasted_iota(jnp.int32, (SUBLANES, M_DH), 0)
    tiles = M_DH // BD
    xm_next = _mm(hn, win_ref[:, :M_DH])
    for h in range(M_HEADS):
        sl = slice(h * M_DH, (h + 1) * M_DH)
        xm = xm_next
        if h + 1 < M_HEADS:
            xm_next = _mm(hn, win_ref[:, (h + 1) * M_DH:(h + 2) * M_DH])
        carry = carry_ref[:, sl]
        conv = convb_ref[:, sl] + convw_ref[M_CONV - 1:M_CONV, sl] * xm
        for j in range(1, M_CONV):
            rolled = pltpu.roll(xm, j, 0)
            head = jnp.where(row8 < j, pltpu.roll(carry, j, 0), rolled[:SUBLANES])
            shifted = jnp.concatenate([head, rolled[SUBLANES:]], axis=0)
            conv = conv + convw_ref[M_CONV - 1 - j:M_CONV - j, sl] * shifted
        carry_ref[:, sl] = xm[tt - SUBLANES:, :]
        conv_out_ref[0, :, sl] = xm[tt - (M_CONV - 1):, :]
        xc = _silu(conv)
        xc_ref[:, sl] = xc
        xcb = xc.astype(BF16)
        qb = _headwise(xcb, wq_ref, h * tiles).astype(BF16)
        kb = _headwise(xcb, wk_ref, h * tiles).astype(BF16)
        vb = _headwise(xm.astype(BF16), wv_ref, h * tiles).astype(BF16)
        q_ref[:, sl] = qb
        k_ref[:, sl] = kb
        v_ref[:, sl] = vb
        gates_t = gates_t + (_nt(wgate_t_ref[0, :, sl], qb) + _nt(wgate_t_ref[1, :, sl], kb)
                             + _nt(wgate_t_ref[2, :, sl], vb))

    lane8 = lax.broadcasted_iota(jnp.int32, (SUBLANES, tt), 1)
    row8t = lax.broadcasted_iota(jnp.int32, (SUBLANES, tt), 0)
    brow_all = _log_sigmoid(gates_t)
    d = 1
    while d < tt:
        brow_all = brow_all + jnp.where(lane8 >= d, pltpu.roll(brow_all, d, 1), 0.0)
        d *= 2
    rows = jnp.where(row8t < M_HEADS, gates_t, brow_all)
    pad = jnp.zeros((128 - SUBLANES, 128), F32)
    cols = jnp.concatenate([jnp.concatenate([rows[:, j * 128:(j + 1) * 128], pad], axis=0).T
                            for j in range(tt // 128)], axis=0)

    ti = lax.broadcasted_iota(jnp.int32, (tt, tt), 0)
    si = lax.broadcasted_iota(jnp.int32, (tt, tt), 1)
    causal = si <= ti
    y = x
    for h in range(M_HEADS):
        sl = slice(h * M_DH, (h + 1) * M_DH)
        qhb = q_ref[:, sl]
        khb = k_ref[:, sl]
        vhb = v_ref[:, sl]
        ig_col = cols[:, h:h + 1]
        b_col = cols[:, M_HEADS + h:M_HEADS + h + 1]
        ig_row = gates_t[h:h + 1, :]
        b_row = brow_all[M_HEADS + h:M_HEADS + h + 1, :]
        m_prev = m_ref[h:h + 1, 0:1]
        c_prev = c_ref[0, h]
        n_prev = n_ref[0, h:h + 1, :]

        dlog = jnp.where(causal, b_col + (ig_row - b_row), -jnp.inf)
        inter_log = b_col + m_prev
        m_row = jnp.maximum(inter_log, jnp.max(dlog, axis=-1, keepdims=True))
        w_inter = jnp.exp(inter_log - m_row)
        qk = _nt(qhb, khb) * (jnp.exp(dlog - m_row) * k_scale)
        num = w_inter * _mm(qhb, c_prev.astype(BF16)) + _mm(qk.astype(BF16), vhb)
        qn = _nt(qhb, jnp.broadcast_to(n_prev, (SUBLANES, M_DH)).astype(BF16))[:, 0:1]
        den = w_inter * qn + jnp.sum(qk, axis=-1, keepdims=True)
        den = jnp.maximum(jnp.abs(den), jnp.exp(-m_row))
        hh = num * (1.0 / den)

        b_end = b_col[tt - 1:tt, :]
        s_log = b_end - b_col + ig_col
        m_new = jnp.maximum(b_end + m_prev, jnp.max(s_log, axis=0, keepdims=True))
        a = jnp.exp(b_end + m_prev - m_new)
        kw = khb.astype(F32) * (jnp.exp(s_log - m_new) * k_scale)
        c_ref[0, h] = a * c_prev + _tn(kw.astype(BF16), vhb)
        n_ref[0, h:h + 1, :] = a * n_prev + jnp.sum(kw, axis=0, keepdims=True)
        m_ref[h:h + 1, :] = jnp.broadcast_to(m_new, (1, 128))

        hc = _head_norm(hh, normg_ref[:, sl], True) + skip_ref[:, sl] * xc_ref[:, sl]
        z = _mm(hn, win_ref[:, M_INNER + h * M_DH:M_INNER + (h + 1) * M_DH])
        y = y + _mm((hc * _silu(z)).astype(BF16), wdown_ref[sl, :])

    y_ref[0] = y
    m_out_ref[0] = m_ref[...]


def _odd_prompt(x, g_all, gi, ml, tt):
    bsz, seq, _ = x.shape
    bt = jnp.broadcast_to(ml['bgate_col'], (SUBLANES, tt))
    return pl.pallas_call(
        functools.partial(_odd_prompt_kernel, tt=tt, gi=gi),
        grid=(bsz, seq // tt),
        in_specs=[pl.BlockSpec((1, tt, D_MODEL), lambda b, t: (b, t, 0)),
                  _const_spec(g_all.shape),
                  _const_spec((D_MODEL, 2 * M_INNER)),
                  _const_spec((M_CONV, M_INNER)),
                  _const_spec((1, M_INNER)),
                  _const_spec((M_INNER // BD, BD, BD)),
                  _const_spec((M_INNER // BD, BD, BD)),
                  _const_spec((M_INNER // BD, BD, BD)),
                  _const_spec((3, SUBLANES, M_INNER)),
                  _const_spec((SUBLANES, tt)),
                  _const_spec((1, M_INNER)),
                  _const_spec((1, M_INNER)),
                  _const_spec((M_INNER, D_MODEL))],
        out_specs=[pl.BlockSpec((1, tt, D_MODEL), lambda b, t: (b, t, 0)),
                   pl.BlockSpec((1, M_HEADS, M_DH, M_DH), lambda b, t: (b, 0, 0, 0)),
                   pl.BlockSpec((1, M_HEADS, M_DH), lambda b, t: (b, 0, 0)),
                   pl.BlockSpec((1, SUBLANES, 128), lambda b, t: (b, 0, 0)),
                   pl.BlockSpec((1, M_CONV - 1, M_INNER), lambda b, t: (b, 0, 0))],
        out_shape=[jax.ShapeDtypeStruct(x.shape, F32),
                   jax.ShapeDtypeStruct((bsz, M_HEADS, M_DH, M_DH), F32),
                   jax.ShapeDtypeStruct((bsz, M_HEADS, M_DH), F32),
                   jax.ShapeDtypeStruct((bsz, SUBLANES, 128), F32),
                   jax.ShapeDtypeStruct((bsz, M_CONV - 1, M_INNER), F32)],
        scratch_shapes=[pltpu.VMEM((SUBLANES, M_INNER), F32), pltpu.VMEM((SUBLANES, 128), F32),
                        pltpu.VMEM((tt, M_INNER), F32), pltpu.VMEM((tt, M_INNER), BF16),
                        pltpu.VMEM((tt, M_INNER), BF16), pltpu.VMEM((tt, M_INNER), BF16)],
        compiler_params=_params(("arbitrary", "arbitrary")),
        name="odd_prompt",
    )(x, g_all, ml['win'], ml['convw'], ml['convb'], ml['wq'], ml['wk'], ml['wv'],
      ml['wgate_t'], bt, ml['normg'], ml['skip'], ml['wdown'])


def _pick_row(ref, b):
    r0 = pl.multiple_of((b // SUBLANES) * SUBLANES, SUBLANES)
    blk = ref[pl.ds(r0, SUBLANES), :]
    row = lax.broadcasted_iota(jnp.int32, blk.shape, 0)
    return jnp.sum(jnp.where(row == b % SUBLANES, blk, 0.0), axis=0, keepdims=True)


def _put_row(ref, b, val):
    r0 = pl.multiple_of((b // SUBLANES) * SUBLANES, SUBLANES)
    blk = ref[pl.ds(r0, SUBLANES), :]
    row = lax.broadcasted_iota(jnp.int32, blk.shape, 0)
    ref[pl.ds(r0, SUBLANES), :] = jnp.where(row == b % SUBLANES, jnp.broadcast_to(val, blk.shape), blk)


def _odd_sample_kernel(x_ref, g_ref, win_ref, convw_ref, convb_ref, cv_ref,
                       wq_ref, wk_ref, wv_ref, wgate_ref, bgate_ref, normg_ref, skip_ref, wdown_ref,
                       m_in_ref, c_in, n_in,
                       y_ref, c_out, n_out, m_out_ref, cv_out_ref,
                       q_ref, k_ref, v_ref, gate_ref, xc_ref, z_ref, h_ref, *, gi):
    b = pl.program_id(0)
    n_b = pl.num_programs(0)

    @pl.when(b == 0)
    def _():
        hn = _rms(x_ref[...], g_ref[gi:gi + 1, :]).astype(BF16)
        p = _mm(hn, win_ref[...])
        xm = p[:, :M_INNER]
        z_ref[...] = p[:, M_INNER:]
        cv0 = cv_ref[:, :M_INNER]
        cv1 = cv_ref[:, M_INNER:2 * M_INNER]
        cv2 = cv_ref[:, 2 * M_INNER:]
        cv_out_ref[:, :M_INNER] = cv1
        cv_out_ref[:, M_INNER:2 * M_INNER] = cv2
        cv_out_ref[:, 2 * M_INNER:] = xm
        conv = (convb_ref[...] + convw_ref[3:4, :] * xm + convw_ref[2:3, :] * cv2
                + convw_ref[1:2, :] * cv1 + convw_ref[0:1, :] * cv0)
        xc = _silu(conv)
        xc_ref[...] = xc
        xcb = xc.astype(BF16)
        q = _headwise(xcb, wq_ref)
        k = _headwise(xcb, wk_ref)
        v = _headwise(xm.astype(BF16), wv_ref)
        q_ref[...] = q
        k_ref[...] = k * (M_DH ** -0.5)
        v_ref[...] = v
        gate_ref[...] = (_mm(q.astype(BF16), wgate_ref[0]) + _mm(k.astype(BF16), wgate_ref[1])
                         + _mm(v.astype(BF16), wgate_ref[2])) + bgate_ref[...]
        m_out_ref[...] = jnp.zeros_like(m_out_ref)
        h_ref[...] = jnp.zeros_like(h_ref)

    r0 = pl.multiple_of((b // SUBLANES) * SUBLANES, SUBLANES)
    row8 = lax.broadcasted_iota(jnp.int32, (SUBLANES, M_INNER), 0)
    sel = row8 == b % SUBLANES
    q8 = jnp.where(sel, q_ref[pl.ds(r0, SUBLANES), :], 0.0)
    k8 = jnp.where(sel, k_ref[pl.ds(r0, SUBLANES), :], 0.0)
    v8 = jnp.where(sel, v_ref[pl.ds(r0, SUBLANES), :], 0.0)
    q8b = q8.astype(BF16)
    k8b = k8.astype(BF16)
    v8b = v8.astype(BF16)
    k_row = jnp.sum(k8, axis=0, keepdims=True)
    q_row = jnp.sum(q8, axis=0, keepdims=True)
    gate = _pick_row(gate_ref, b)
    m_all = _pick_row(m_in_ref, b)
    lane = lax.broadcasted_iota(jnp.int32, (1, 128), 1)
    m_new_all = jnp.zeros((1, 128), F32)
    h_parts = []
    for h in range(M_HEADS):
        sl = slice(h * M_DH, (h + 1) * M_DH)
        ig = gate[:, h:h + 1]
        lf = _log_sigmoid(gate[:, M_HEADS + h:M_HEADS + h + 1])
        m_prev = m_all[:, h:h + 1]
        m_new = jnp.maximum(lf + m_prev, ig)
        a = jnp.exp(lf + m_prev - m_new)
        ws = jnp.exp(ig - m_new)
        c_new = a * c_in[0, h] + ws * _tn(k8b[:, sl], v8b[:, sl])
        c_out[0, h] = c_new
        n_new = a * n_in[0, h:h + 1, :] + ws * k_row[:, sl]
        n_out[0, h:h + 1, :] = n_new
        num = jnp.sum(_mm(q8b[:, sl], c_new.astype(BF16)), axis=0, keepdims=True)
        den = jnp.sum(q_row[:, sl] * n_new, axis=-1, keepdims=True)
        den = jnp.maximum(jnp.abs(den), jnp.exp(-m_new))
        h_parts.append(num / den)
        m_new_all = jnp.where(lane == h, m_new, m_new_all)
    _put_row(h_ref, b, jnp.concatenate(h_parts, axis=1))
    _put_row(m_out_ref, b, m_new_all)

    @pl.when(b == n_b - 1)
    def _():
        hfull = h_ref[...]
        hs = [hfull[:, h * M_DH:(h + 1) * M_DH] for h in range(M_HEADS)]
        y_ref[...] = x_ref[...] + _mlstm_out(hs, xc_ref[...], z_ref[...], normg_ref, skip_ref, wdown_ref)


def _odd_sample(x, g_all, gi, ml, c0, n0, m0, conv0):
    n = x.shape[0]
    m_pad = jnp.pad(m0, ((0, 0), (0, 128 - M_HEADS)))
    cw = (M_CONV - 1) * M_INNER
    full = lambda shape: pl.BlockSpec(shape, lambda b: (0,) * len(shape))
    outs = pl.pallas_call(
        functools.partial(_odd_sample_kernel, gi=gi),
        grid=(n,),
        in_specs=[_const_spec((n, D_MODEL)),
                  _const_spec(g_all.shape),
                  _const_spec((D_MODEL, 2 * M_INNER)),
                  _const_spec((M_CONV, M_INNER)),
                  _const_spec((1, M_INNER)),
                  _const_spec((n, cw)),
                  _const_spec((M_INNER // BD, BD, BD)),
                  _const_spec((M_INNER // BD, BD, BD)),
                  _const_spec((M_INNER // BD, BD, BD)),
                  _const_spec((3, M_INNER, 128)),
                  _const_spec((1, 128)),
                  _const_spec((1, M_INNER)),
                  _const_spec((1, M_INNER)),
                  _const_spec((M_INNER, D_MODEL)),
                  _const_spec((n, 128)),
                  pl.BlockSpec((1, M_HEADS, M_DH, M_DH), lambda b: (b, 0, 0, 0)),
                  pl.BlockSpec((1, M_HEADS, M_DH), lambda b: (b, 0, 0))],
        out_specs=[full((n, D_MODEL)),
                   pl.BlockSpec((1, M_HEADS, M_DH, M_DH), lambda b: (b, 0, 0, 0)),
                   pl.BlockSpec((1, M_HEADS, M_DH), lambda b: (b, 0, 0)),
                   full((n, 128)),
                   full((n, cw))],
        out_shape=[jax.ShapeDtypeStruct((n, D_MODEL), F32),
                   jax.ShapeDtypeStruct(c0.shape, F32),
                   jax.ShapeDtypeStruct(n0.shape, F32),
                   jax.ShapeDtypeStruct((n, 128), F32),
                   jax.ShapeDtypeStruct((n, cw), F32)],
        scratch_shapes=[pltpu.VMEM((n, M_INNER), F32), pltpu.VMEM((n, M_INNER), F32), pltpu.VMEM((n, M_INNER), F32),
                        pltpu.VMEM((n, 128), F32), pltpu.VMEM((n, M_INNER), F32), pltpu.VMEM((n, M_INNER), F32),
                        pltpu.VMEM((n, M_INNER), F32)],
        compiler_params=_params(("arbitrary",)),
        name="odd_sample",
    )(x, g_all, ml['win'], ml['convw'], ml['convb'], conv0.reshape(n, cw),
      ml['wq'], ml['wk'], ml['wv'], ml['wgate'], ml['bgate'], ml['normg'], ml['skip'], ml['wdown'],
      m_pad, c0, n0)
    y, c, nn, m_new, conv_new = outs
    return y, c, nn, m_new[:, :M_HEADS], conv_new.reshape(n, M_CONV - 1, M_INNER)


def _odd_sample_proj_kernel(x_ref, g_ref, win_ref, convw_ref, convb_ref, cv_ref, wq_ref, wk_ref, wv_ref,
                            wgate_ref, bgate_ref, m_in_ref, n_in_ref,
                            qt_ref, kwt_ref, v_ref, a_ref, den_ref, n_out_ref, m_out_ref, cv_out_ref,
                            xc_ref, z_ref, *, gi):
    hn = _rms(x_ref[...], g_ref[gi:gi + 1, :]).astype(BF16)
    p = _mm(hn, win_ref[...])
    xm = p[:, :M_INNER]
    z_ref[...] = p[:, M_INNER:]
    cv0 = cv_ref[:, :M_INNER]
    cv1 = cv_ref[:, M_INNER:2 * M_INNER]
    cv2 = cv_ref[:, 2 * M_INNER:]
    cv_out_ref[:, :M_INNER] = cv1
    cv_out_ref[:, M_INNER:2 * M_INNER] = cv2
    cv_out_ref[:, 2 * M_INNER:] = xm
    conv = (convb_ref[...] + convw_ref[3:4, :] * xm + convw_ref[2:3, :] * cv2
            + convw_ref[1:2, :] * cv1 + convw_ref[0:1, :] * cv0)
    xc = _silu(conv)
    xc_ref[...] = xc
    xcb = xc.astype(BF16)
    q = _headwise(xcb, wq_ref)
    k = _headwise(xcb, wk_ref)
    v = _headwise(xm.astype(BF16), wv_ref)
    v_ref[...] = v
    gate = (_mm(q.astype(BF16), wgate_ref[0]) + _mm(k.astype(BF16), wgate_ref[1])
            + _mm(v.astype(BF16), wgate_ref[2])) + bgate_ref[...]
    m_in = m_in_ref[...]
    lane = lax.broadcasted_iota(jnp.int32, m_in.shape, 1)
    a_all = jnp.zeros_like(m_in)
    den_all = jnp.zeros_like(m_in)
    m_all = jnp.zeros_like(m_in)
    for h in range(M_HEADS):
        sl = slice(h * M_DH, (h + 1) * M_DH)
        ig = gate[:, h:h + 1]
        lf = _log_sigmoid(gate[:, M_HEADS + h:M_HEADS + h + 1])
        m_prev = m_in[:, h:h + 1]
        m_new = jnp.maximum(lf + m_prev, ig)
        a = jnp.exp(lf + m_prev - m_new)
        kw = (jnp.exp(ig - m_new) * (M_DH ** -0.5)) * k[:, sl]
        n_new = a * n_in_ref[:, sl] + kw
        n_out_ref[:, sl] = n_new
        qh = q[:, sl]
        den = jnp.maximum(jnp.abs(jnp.sum(qh * n_new, axis=-1, keepdims=True)), jnp.exp(-m_new))
        a_all = jnp.where(lane == h, a, a_all)
        den_all = jnp.where(lane == h, den, den_all)
        m_all = jnp.where(lane == h, m_new, m_all)
        for c in range(M_DH // 128):
            qt_ref[h, c * 128:(c + 1) * 128, :] = qh[:, c * 128:(c + 1) * 128].T
            kwt_ref[h, c * 128:(c + 1) * 128, :] = kw[:, c * 128:(c + 1) * 128].T
    a_ref[...] = a_all
    den_ref[...] = den_all
    m_out_ref[...] = m_all


def _odd_sample_proj(x, g_all, gi, ml, n0, m0, conv0):
    n = x.shape[0]
    assert n == 128, "the per-head transposes assume one 128-lane tile of sequences"
    m_pad = jnp.pad(m0, ((0, 0), (0, 128 - M_HEADS)))
    cw = (M_CONV - 1) * M_INNER
    f32 = lambda *shape: jax.ShapeDtypeStruct(shape, F32)
    shapes = [f32(M_HEADS, M_DH, n), f32(M_HEADS, M_DH, n), f32(n, M_INNER), f32(n, 128), f32(n, 128),
              f32(n, M_INNER), f32(n, 128), f32(n, cw), f32(n, M_INNER), f32(n, M_INNER)]
    args = (x, g_all, ml['win'], ml['convw'], ml['convb'], conv0.reshape(n, cw), ml['wq'], ml['wk'], ml['wv'],
            ml['wgate'], ml['bgate'], m_pad, n0.reshape(n, M_INNER))
    outs = pl.pallas_call(
        functools.partial(_odd_sample_proj_kernel, gi=gi),
        grid=(1,),
        in_specs=[_const_spec(a.shape) for a in args],
        out_specs=[pl.BlockSpec(s.shape, lambda i, nd=len(s.shape): (0,) * nd) for s in shapes],
        out_shape=shapes,
        compiler_params=_params(("arbitrary",)),
        name="odd_sample_proj",
    )(*args)
    qt, kwt, v, a, den, n_new, m_new, conv_new, xc, z = outs
    return (qt, kwt, v, a, den, n_new.reshape(n, M_HEADS, M_DH), m_new[:, :M_HEADS],
            conv_new.reshape(n, M_CONV - 1, M_INNER), xc, z)


STREAM_IN_BUFS = 6
STREAM_OUT_BUFS = 4
STREAM_ROWS = 32


def _ffn_stream_kernel(xp_ref, g_ref, gfin_ref, wg_ref, wu_ref, wd_ref, qt_ref, kwt_ref, v_ref, a_ref, c_in,
                       op_ref, c_out, hnum_ref, cin_buf, cout_buf, sem_in, sem_out, *, gi, per_step):
    i = pl.program_id(0)
    n_pieces = per_step * M_HEADS

    def seq(p):
        return i * per_step + p // M_HEADS

    def in_copy(p):
        slot = p % STREAM_IN_BUFS
        return pltpu.make_async_copy(c_in.at[seq(p), p % M_HEADS], cin_buf.at[slot], sem_in.at[slot])

    def out_copy(p):
        slot = p % STREAM_OUT_BUFS
        return pltpu.make_async_copy(cout_buf.at[slot], c_out.at[seq(p), p % M_HEADS], sem_out.at[slot])

    @pl.when(i == 0)
    def _():
        hnum_ref[...] = jnp.zeros_like(hnum_ref)

    for p in range(STREAM_IN_BUFS):
        in_copy(p).start()

    shift = (128 - (i * per_step) % 128) % 128
    rolled = {}

    def columns(ref, h):
        if (id(ref), h) not in rolled:
            rolled[(id(ref), h)] = pltpu.roll(ref[h], shift, 1)
        return rolled[(id(ref), h)]

    rows = {}

    def seq_rows(j):
        if j not in rows:
            s = i * per_step + j
            rows[j] = (_pick_row(a_ref, s), _pick_row(v_ref, s))
        return rows[j]

    h_parts = {}

    def update(p):
        j, h = divmod(p, M_HEADS)
        a_row, v_row = seq_rows(j)
        kw_col = columns(kwt_ref, h)[:, j:j + 1]
        q_col = columns(qt_ref, h)[:, j:j + 1]
        a = a_row[:, h:h + 1]
        v_h = v_row[:, h * M_DH:(h + 1) * M_DH]
        acc = jnp.zeros((STREAM_ROWS, M_DH), F32)
        for r0 in range(0, M_DH, STREAM_ROWS):
            rs = slice(r0, r0 + STREAM_ROWS)
            c_new = a * cin_buf[p % STREAM_IN_BUFS, rs, :] + kw_col[rs] * v_h
            cout_buf[p % STREAM_OUT_BUFS, rs, :] = c_new
            acc = acc + q_col[rs] * c_new
        h_parts[(j, h)] = jnp.sum(acc, axis=0, keepdims=True)
        if h == M_HEADS - 1:
            _put_row(hnum_ref, i * per_step + j, jnp.concatenate([h_parts[(j, hh)] for hh in range(M_HEADS)], axis=1))

    def region(first, matmul):
        ps = (first, first + 1)
        for p in ps:
            in_copy(p).wait()
            if p >= STREAM_OUT_BUFS:
                out_copy(p - STREAM_OUT_BUFS).wait()
        out = matmul()
        for p in ps:
            update(p)
        for p in ps:
            out_copy(p).start()
            if p + STREAM_IN_BUFS < n_pieces:
                in_copy(p + STREAM_IN_BUFS).start()
        return out

    x = xp_ref[...]
    hb = _rms(x, g_ref[gi:gi + 1, :]).astype(BF16)
    y = None
    first = 0
    for lo, hi in zip(FFN_BOUNDS[:-1], FFN_BOUNDS[1:]):
        half = lo + (hi - lo + MXU_TILE) // (2 * MXU_TILE) * MXU_TILE
        parts = []
        for w_ref in (wg_ref, wu_ref):
            for c0, c1 in ((lo, half), (half, hi)):
                parts.append(region(first, lambda w_ref=w_ref, c0=c0, c1=c1: _mm(hb, w_ref[:, c0:c1])))
                first += 2
        gt = jnp.concatenate(parts[:2], axis=1)
        ut = jnp.concatenate(parts[2:], axis=1)
        d = _mm((_silu(gt) * ut).astype(BF16), wd_ref[lo:hi, :])
        y = d if y is None else y + d
    assert first == n_pieces
    op_ref[...] = _rms(x + 0.5 * y, gfin_ref[...])
    for p in range(n_pieces - STREAM_OUT_BUFS, n_pieces):
        out_copy(p).wait()


def _ffn_stream(xp, g_all, gfin, wg, wu, wd, gi, tm, qt, kwt, v, a, c0):
    n = xp.shape[0]
    ns = v.shape[0]
    steps = n // tm
    per_step = ns // steps
    assert per_step * steps == ns and 128 % per_step == 0
    any_spec = pl.BlockSpec(memory_space=pl.ANY)
    return pl.pallas_call(
        functools.partial(_ffn_stream_kernel, gi=gi, per_step=per_step),
        grid=(steps,),
        in_specs=[pl.BlockSpec((tm, D_MODEL), lambda i: (i, 0)),
                  _const_spec(g_all.shape),
                  _const_spec((1, D_MODEL)),
                  _const_spec((D_MODEL, D_FF)), _const_spec((D_MODEL, D_FF)), _const_spec((D_FF, D_MODEL)),
                  _const_spec(qt.shape), _const_spec(kwt.shape), _const_spec(v.shape), _const_spec(a.shape),
                  any_spec],
        out_specs=[pl.BlockSpec((tm, D_MODEL), lambda i: (i, 0)),
                   any_spec,
                   pl.BlockSpec((ns, M_INNER), lambda i: (0, 0))],
        out_shape=[jax.ShapeDtypeStruct((n, D_MODEL), F32),
                   jax.ShapeDtypeStruct(c0.shape, F32),
                   jax.ShapeDtypeStruct((ns, M_INNER), F32)],
        scratch_shapes=[pltpu.VMEM((STREAM_IN_BUFS, M_DH, M_DH), F32), pltpu.VMEM((STREAM_OUT_BUFS, M_DH, M_DH), F32),
                        pltpu.SemaphoreType.DMA((STREAM_IN_BUFS,)), pltpu.SemaphoreType.DMA((STREAM_OUT_BUFS,))],
        compiler_params=_params(("arbitrary",)),
        name="ffn_final_stream",
    )(xp, g_all, gfin, wg, wu, wd, qt, kwt, v, a, c0)


def _odd_sample_out_kernel(x_ref, hnum_ref, den_ref, xc_ref, z_ref, normg_ref, skip_ref, wdown_ref,
                           g_ref, gfin_ref, wg_ref, wu_ref, wd_ref, y_ref, *, gi):
    hnum = hnum_ref[...]
    den = den_ref[...]
    hs = [hnum[:, h * M_DH:(h + 1) * M_DH] * (1.0 / den[:, h:h + 1]) for h in range(M_HEADS)]
    xs = x_ref[...] + _mlstm_out(hs, xc_ref[...], z_ref[...], normg_ref, skip_ref, wdown_ref)
    y_ref[...] = _ffn_rows(xs, g_ref[gi:gi + 1, :], wg_ref, wu_ref, wd_ref, gfin_ref[...])


def _odd_sample_out(x, hnum, den, xc, z, ml, g_all, gfin, wg, wu, wd, gi):
    args = (x, hnum, den, xc, z, ml['normg'], ml['skip'], ml['wdown'], g_all, gfin, wg, wu, wd)
    return pl.pallas_call(
        functools.partial(_odd_sample_out_kernel, gi=gi),
        grid=(1,),
        in_specs=[_const_spec(a.shape) for a in args],
        out_specs=pl.BlockSpec(x.shape, lambda i: (0, 0)),
        out_shape=jax.ShapeDtypeStruct(x.shape, F32),
        compiler_params=_params(("arbitrary",)),
        name="odd_sample_out",
    )(*args)


TM_FFN = 512
TT_EVEN = 256
TT_ODD = 512


def kernel(x_prompt, x_sample, state_ret, state_hgrn, state_mlstm_C, state_mlstm_n, state_mlstm_m, state_mlstm_conv,
           norm_g, final_norm_g, ffn_w_gate, ffn_w_up, ffn_w_down, ev_w_in, ev_w_out, ret_norm_g, hg_norm_g,
           hg_lb_logits, ml_w_in, ml_conv_w, ml_conv_b, ml_w_q, ml_w_k, ml_w_v, ml_w_ig, ml_b_ig, ml_w_fg,
           ml_b_fg, ml_norm_g, ml_skip, ml_w_down):
    bp, tp, _ = x_prompt.shape
    ns = x_sample.shape[0]

    g_all = norm_g.reshape(-1, D_MODEL)
    gfin = final_norm_g.reshape(1, D_MODEL)
    lb_all = jnp.cumsum(jax.nn.softmax(hg_lb_logits.astype(F32), axis=0), axis=0)
    lb = lb_all[0].reshape(1, G_HEADS * G_EXP)
    retg = ret_norm_g[0]
    hgg = hg_norm_g[0]
    wgate = _gate_weights(ml_w_ig[0], ml_w_fg[0])
    bgate = jnp.pad(jnp.concatenate([ml_b_ig[0], ml_b_fg[0]]), (0, 128 - 2 * M_HEADS)).reshape(1, 128)
    ml = {
        'convw': ml_conv_w[0],
        'convb': ml_conv_b[0].reshape(1, M_INNER),
        'wq': _block_diag(ml_w_q[0]).astype(BF16),
        'wk': _block_diag(ml_w_k[0]).astype(BF16),
        'wv': _block_diag(ml_w_v[0]).astype(BF16),
        'wgate': wgate.astype(BF16),
        'wgate_t': jnp.swapaxes(wgate[:, :, :SUBLANES], 1, 2).astype(BF16),
        'bgate': bgate,
        'bgate_col': bgate[0, :SUBLANES].reshape(SUBLANES, 1),
        'normg': ml_norm_g[0].reshape(1, M_INNER),
        'skip': ml_skip[0].reshape(1, M_INNER),
    }

    xp = x_prompt.reshape(bp * tp, D_MODEL)
    xs = x_sample.reshape(ns, D_MODEL)
    ffn_w = (ffn_w_gate, ffn_w_up, ffn_w_down)
    next_ffn = lambda layer, idx: [(w, (layer, idx)) for w in ffn_w]

    w00 = [w[0, 0].astype(BF16) for w in ffn_w]
    xp, xs, cast = _ffn(xp, xs, g_all, gfin, *w00, 0, TM_FFN,
                        casts=next_ffn(0, 1) + [(ev_w_in, (0,)), (ev_w_out, (0,))])
    w01, (ev_in, ev_out) = cast[:3], cast[3:]
    xp, ret_p, hg_p = _even_prompt(xp.reshape(bp, tp, D_MODEL), g_all, 1, ev_in, ev_out, retg, hgg, lb, TT_EVEN)
    xs, ret_s, hg_s = _even_sample(xs, g_all, 1, ev_in, ev_out, retg, hgg, lb, state_ret[:, 0], state_hgrn[:, 0])
    xp, xs, cast = _ffn(xp.reshape(bp * tp, D_MODEL), xs, g_all, gfin, *w01, 2, TM_FFN,
                        casts=next_ffn(1, 0) + [(ml_w_in, (0,)), (ml_w_down, (0,))])
    w10, (ml['win'], ml['wdown']) = cast[:3], cast[3:]
    xp, xs, w11 = _ffn(xp, xs, g_all, gfin, *w10, 3, TM_FFN, casts=next_ffn(1, 1))
    xp, c_p, n_p, m_p, conv_p = _odd_prompt(xp.reshape(bp, tp, D_MODEL), g_all, 4, ml, TT_ODD)
    qt, kwt, v_s, a_s, den_s, n_s, m_s, conv_s, xc_s, z_s = _odd_sample_proj(
        xs, g_all, 4, ml, state_mlstm_n[:, 0], state_mlstm_m[:, 0], state_mlstm_conv[:, 0])
    y_p, c_s, hnum = _ffn_stream(xp.reshape(bp * tp, D_MODEL), g_all, gfin, *w11, 5, TM_FFN,
                                 qt, kwt, v_s, a_s, state_mlstm_C[:, 0])
    y_s = _odd_sample_out(xs, hnum, den_s, xc_s, z_s, ml, g_all, gfin, *w11, 5)

    return (y_p.reshape(bp, tp, D_MODEL), y_s.reshape(ns, 1, D_MODEL),
            ret_p[:, None], hg_p[:, None], c_p[:, None], n_p[:, None], m_p[:, None, :M_HEADS, 0], conv_p[:, None],
            ret_s[:, None], hg_s[:, None], c_s[:, None], n_s[:, None], m_s[:, None], conv_s[:, None])
```

```python
import functools
import math

import jax
import jax.numpy as jnp
from jax import lax
from jax.experimental import pallas as pl
from jax.experimental.pallas import tpu as pltpu

D_MODEL = 1024
PAST_LEN = 16384
R_HEADS = 4
R_DK = 128
R_DV = 128
G_HEADS = 4
G_EXP = 128
G_DV = 128
M_INNER = 2 * D_MODEL
M_HEADS = 4
M_DH = M_INNER // M_HEADS
M_CONV = 4
QKV_BLOCK = 4
D_FF = 2816
EPS = 1e-6
ROPE_BASE = 10000.0
EVEN_IN = 4096
EVEN_OUT = 1024

F32 = jnp.float32
BF16 = jnp.bfloat16

VMEM_LIMIT_BYTES = 56 * 1024 * 1024

HG_CHUNK = 64
HG_SUB = 16
HG_SAFE_LOG_DECAY = -60.0
MXU_TILE = 256
FFN_BOUNDS = (0, 6 * MXU_TILE, D_FF)
BD = MXU_TILE
SUBLANES = 8


def _nt(a, b):
    return lax.dot_general(a, b, (((1,), (1,)), ((), ())), preferred_element_type=F32)


def _tn(a, b):
    return lax.dot_general(a, b, (((0,), (0,)), ((), ())), preferred_element_type=F32)


def _mm(a, b):
    return jnp.dot(a, b, preferred_element_type=F32)


def _sigmoid(x):
    return 1.0 / (1.0 + jnp.exp(-x))


def _silu(x):
    return x * _sigmoid(x)


def _log_sigmoid(x):
    return jnp.minimum(x, 0.0) - jnp.log(1.0 + jnp.exp(-jnp.abs(x)))


def _rms(x, g):
    return x * lax.rsqrt(jnp.mean(x * x, axis=-1, keepdims=True) + EPS) * g


def _head_norm(x, g, center):
    if center:
        x = x - jnp.mean(x, axis=-1, keepdims=True)
    return x * lax.rsqrt(jnp.mean(x * x, axis=-1, keepdims=True) + EPS) * g


def _rotary(x, cos, sin_signed):
    return x * cos + pltpu.roll(x, 64, 1) * sin_signed


def _const_spec(shape):
    n = len(shape)
    return pl.BlockSpec(shape, lambda *_: (0,) * n, pipeline_mode=pl.Buffered(1))


def _params(sem):
    return pltpu.CompilerParams(dimension_semantics=sem, vmem_limit_bytes=VMEM_LIMIT_BYTES)


def _ffn_rows(x, g, wg_ref, wu_ref, wd_ref, gfin):
    h = _rms(x, g).astype(BF16)
    y = jnp.zeros_like(x)
    for lo, hi in zip(FFN_BOUNDS[:-1], FFN_BOUNDS[1:]):
        gt = _mm(h, wg_ref[:, lo:hi])
        ut = _mm(h, wu_ref[:, lo:hi])
        a = (_silu(gt) * ut).astype(BF16)
        y = y + _mm(a, wd_ref[lo:hi, :])
    out = x + 0.5 * y
    if gfin is not None:
        out = _rms(out, gfin)
    return out


def _ffn_kernel(xp_ref, xs_ref, g_ref, gfin_ref, wg_ref, wu_ref, wd_ref, *rest, gi, final, n_cast):
    cast_in = rest[:n_cast]
    op_ref, os_ref = rest[n_cast:n_cast + 2]
    cast_out = rest[n_cast + 2:]
    g = g_ref[gi:gi + 1, :]
    gfin = gfin_ref[...] if final else None
    op_ref[...] = _ffn_rows(xp_ref[...], g, wg_ref, wu_ref, wd_ref, gfin)
    for src, dst in zip(cast_in, cast_out):
        dst[...] = src[...].astype(BF16)

    @pl.when(pl.program_id(0) == pl.num_programs(0) - 1)
    def _():
        os_ref[...] = _ffn_rows(xs_ref[...], g, wg_ref, wu_ref, wd_ref, gfin)


BF16_ROWS = 16


def _cast_specs(arr, lead, steps):
    rows, cols = arr.shape[-2:]
    per = 1 if (rows // steps) % BF16_ROWS == 0 else 2
    br = rows * per // steps
    in_spec = pl.BlockSpec((None,) * len(lead) + (br, cols), lambda i: tuple(lead) + (i // per, 0))
    out_spec = pl.BlockSpec((br, cols), lambda i: (i // per, 0))
    return in_spec, out_spec, jax.ShapeDtypeStruct((rows, cols), BF16)


def _ffn(xp, xs, g_all, gfin, wg, wu, wd, gi, tm, final=False, casts=()):
    n = xp.shape[0]
    ns = xs.shape[0]
    steps = n // tm
    cast_specs = [_cast_specs(arr, lead, steps) for arr, lead in casts]
    outs = pl.pallas_call(
        functools.partial(_ffn_kernel, gi=gi, final=final, n_cast=len(casts)),
        grid=(steps,),
        in_specs=[pl.BlockSpec((tm, D_MODEL), lambda i: (i, 0)),
                  _const_spec((ns, D_MODEL)),
                  _const_spec(g_all.shape),
                  _const_spec((1, D_MODEL)),
                  _const_spec((D_MODEL, D_FF)), _const_spec((D_MODEL, D_FF)), _const_spec((D_FF, D_MODEL))]
                 + [c[0] for c in cast_specs],
        out_specs=[pl.BlockSpec((tm, D_MODEL), lambda i: (i, 0)),
                   pl.BlockSpec((ns, D_MODEL), lambda i: (0, 0))] + [c[1] for c in cast_specs],
        out_shape=[jax.ShapeDtypeStruct((n, D_MODEL), F32), jax.ShapeDtypeStruct((ns, D_MODEL), F32)]
                  + [c[2] for c in cast_specs],
        compiler_params=_params(("arbitrary",)),
        name="ffn_final" if final else "ffn",
    )(xp, xs, g_all, gfin, wg, wu, wd, *[arr for arr, _ in casts])
    return outs[0], outs[1], outs[2:]


def _ret_log_gamma(h):
    return math.log(1.0 - 2.0 ** (-5.0 - h))


def _head_parts(p, mixer, h):
    base = mixer * 4 * R_HEADS * R_DK
    return [p[:, base + (j * R_HEADS + h) * R_DK:base + (j * R_HEADS + h + 1) * R_DK] for j in range(4)]


def _hgrn_gates(gq, gf, lb):
    f = lb + (1.0 - lb) * _sigmoid(gf)
    kk = (1.0 - lb) * _sigmoid(-gf)
    qq = _silu(gq)
    return qq, kk, f


def _split3(x):
    hi = x.astype(BF16)
    r1 = x - hi.astype(F32)
    mid = r1.astype(BF16)
    lo = (r1 - mid.astype(F32)).astype(BF16)
    return hi, mid, lo


def _shift_rows(bases, d):
    base = bases[d % SUBLANES]
    full = (d // SUBLANES) * SUBLANES
    return pltpu.roll(base, full, 0) if full else base


def _hgrn_tile_factorised(qq, kk, vv, b, st):
    tt = qq[0].shape[0]
    heads = range(len(qq))
    qx = [(qq[h] * jnp.exp(b[h])).astype(BF16) for h in heads]
    kx = [kk[h] * jnp.exp(-b[h]) for h in heads]
    kxb = [kx[h].astype(BF16) for h in heads]
    vb = [vv[h].astype(BF16) for h in heads]
    ti = lax.broadcasted_iota(jnp.int32, (HG_CHUNK, HG_CHUNK), 0)
    si = lax.broadcasted_iota(jnp.int32, (HG_CHUNK, HG_CHUNK), 1)
    causal = si <= ti
    st = list(st)
    o_chunks = [[] for _ in heads]
    for c in range(tt // HG_CHUNK):
        rs = slice(c * HG_CHUNK, (c + 1) * HG_CHUNK)
        for h in heads:
            a = jnp.where(causal, _nt(qx[h][rs], kxb[h][rs]), 0.0)
            o_chunks[h].append(_mm(a.astype(BF16), vb[h][rs]) + _nt(qx[h][rs], st[h].astype(BF16)))
            etot = jnp.exp(b[h][(c + 1) * HG_CHUNK - 1:(c + 1) * HG_CHUNK])
            st[h] = st[h] * etot + _tn(vb[h][rs], (kx[h][rs] * etot).astype(BF16))
    return [jnp.concatenate(o_chunks[h], axis=0) for h in heads], st


def _hgrn_tile_guarded(qq_all, kk_all, f_all, vv_all, b_all, st_all):
    tt = qq_all.shape[0]
    hd = G_EXP
    row = lax.broadcasted_iota(jnp.int32, (tt, hd), 0)
    rsub = row % HG_SUB
    sub = (lax.broadcasted_iota(jnp.int32, (HG_CHUNK, hd), 0)) // HG_SUB
    n_sub = HG_CHUNK // HG_SUB
    outs, states = [], []
    for h in range(G_HEADS):
        hs = slice(h * hd, (h + 1) * hd)
        qq = qq_all[:, hs]
        kk = kk_all[:, hs]
        ff = f_all[:, hs]
        bc_all = b_all[:, hs]
        vv = vv_all[:, hs]
        f_sh = [ff] + [pltpu.roll(ff, r, 0) for r in range(1, SUBLANES)]
        k_sh = [kk] + [pltpu.roll(kk, r, 0) for r in range(1, SUBLANES)]
        v_sh = [vv] + [pltpu.roll(vv, r, 0) for r in range(1, SUBLANES)]
        o_band = jnp.sum(qq * kk, axis=-1, keepdims=True) * vv
        decay = None
        for d in range(1, HG_SUB):
            fd = _shift_rows(f_sh, d - 1)
            decay = jnp.where(rsub >= d, fd if decay is None else decay * fd, 0.0)
            term = qq * _shift_rows(k_sh, d) * decay
            o_band = o_band + jnp.sum(term, axis=-1, keepdims=True) * _shift_rows(v_sh, d)
        st = st_all[h]
        o_chunks = []
        for c in range(tt // HG_CHUNK):
            r0 = c * HG_CHUNK
            bc = bc_all[r0:r0 + HG_CHUNK]
            qc = qq[r0:r0 + HG_CHUNK]
            kc = kk[r0:r0 + HG_CHUNK]
            vcb = vv[r0:r0 + HG_CHUNK].astype(BF16)
            refs = [bc[i * HG_SUB - 1:i * HG_SUB] for i in range(1, n_sub)]
            refrow = refs[-1]
            for i in range(n_sub - 2, 0, -1):
                refrow = jnp.where(sub == i, refs[i - 1], refrow)
            qp = qc * jnp.exp(bc - refrow)
            lhs = jnp.concatenate([jnp.where(sub == i, qp, 0.0) for i in range(1, n_sub)], axis=1)
            kcat = jnp.concatenate([jnp.where(sub < i, kc * jnp.exp(refs[i - 1] - bc), 0.0)
                                    for i in range(1, n_sub)], axis=1)
            a = _nt(lhs.astype(BF16), kcat.astype(BF16))
            qb = (qc * jnp.exp(bc)).astype(BF16)
            o_chunks.append(_mm(a.astype(BF16), vcb) + _nt(qb, st.astype(BF16)))
            btot = bc[HG_CHUNK - 1:HG_CHUNK]
            ke = (kc * jnp.exp(btot - bc)).astype(BF16)
            st = st * jnp.exp(btot) + _tn(vcb, ke)
        outs.append(o_band + jnp.concatenate(o_chunks, axis=0))
        states.append(st)
    return jnp.concatenate(outs, axis=1), jnp.stack(states)


def _even_prompt_kernel(x_ref, g_ref, win_ref, wout_ref, cos_ref, sin_ref, retg_ref, hgg_ref, lb_ref,
                        y_ref, sret_ref, shg_ref, dmat_ref, tri_ref, st_ref, ohg_ref, *, tt, gi):
    b_id = pl.program_id(0)
    t_id = pl.program_id(1)
    n_t = pl.num_programs(1)
    hd = 128

    @pl.when(jnp.logical_and(b_id == 0, t_id == 0))
    def _():
        ti = lax.broadcasted_iota(jnp.int32, (tt, tt), 0)
        si = lax.broadcasted_iota(jnp.int32, (tt, tt), 1)
        diff = (ti - si).astype(F32)
        for h in range(R_HEADS):
            dmat_ref[h] = jnp.where(diff >= 0.0, jnp.exp(_ret_log_gamma(h) * jnp.maximum(diff, 0.0)), 0.0)
        same_chunk = (ti // HG_CHUNK) == (si // HG_CHUNK)
        tri_ref[...] = jnp.where(jnp.logical_and(same_chunk, si <= ti), 1.0, 0.0).astype(BF16)

    @pl.when(t_id == 0)
    def _():
        sret_ref[...] = jnp.zeros_like(sret_ref)
        st_ref[...] = jnp.zeros_like(st_ref)

    x = x_ref[0]
    hn = _rms(x, g_ref[gi:gi + 1, :]).astype(BF16)
    cos = cos_ref[...]
    sin = sin_ref[...]
    rowf = lax.broadcasted_iota(jnp.int32, (tt, hd), 0).astype(F32)
    outs = []
    hg = []
    logf_parts = []
    st_old = st_ref[...]

    p = _mm(hn, win_ref[...])
    for i in range(R_HEADS + G_HEADS):
        h = i // 2
        pa, pb, pc, pd = _head_parts(p, i % 2, h)
        if i % 2 == 0:
            lg = _ret_log_gamma(h)
            q = _rotary(pa, cos, sin)
            k = _rotary(pb, cos, sin) * (R_DK ** -0.5)
            v = pc
            rg = pd
            s0 = sret_ref[0, h]
            qb = q.astype(BF16)
            vb = v.astype(BF16)
            inter = _mm(qb, s0.astype(BF16)) * jnp.exp(lg * (rowf + 1.0))
            scores = _nt(qb, k.astype(BF16)) * dmat_ref[h]
            intra = _mm(scores.astype(BF16), vb)
            kd = (k * jnp.exp(lg * (tt - 1.0 - rowf))).astype(BF16)
            sret_ref[0, h] = math.exp(lg * tt) * s0 + _tn(kd, vb)
            outs.append(_head_norm(inter + intra, retg_ref[h:h + 1, :], False) * _silu(rg))
        else:
            qq, kk, ff = _hgrn_gates(pa, pb, lb_ref[:, h * hd:(h + 1) * hd])
            hg.append((qq, kk, ff, pc, _sigmoid(pd)))
            logf_parts.extend(_split3(jnp.log(ff)))

    cs = _mm(tri_ref[...], jnp.concatenate(logf_parts, axis=1))
    b = [cs[:, (3 * h) * hd:(3 * h + 1) * hd] + cs[:, (3 * h + 1) * hd:(3 * h + 2) * hd]
         + cs[:, (3 * h + 2) * hd:(3 * h + 3) * hd] for h in range(G_HEADS)]
    b_all = jnp.concatenate(b, axis=1)
    o_fast, st_fast = _hgrn_tile_factorised([t[0] for t in hg], [t[1] for t in hg], [t[3] for t in hg], b,
                                            [st_old[h] for h in range(G_HEADS)])
    for h in range(G_HEADS):
        ohg_ref[:, h * hd:(h + 1) * hd] = o_fast[h]
        st_ref[h] = st_fast[h]

    @pl.when(jnp.min(b_all) < HG_SAFE_LOG_DECAY)
    def _():
        cat = lambda j: jnp.concatenate([t[j] for t in hg], axis=1)
        o_safe, st_safe = _hgrn_tile_guarded(cat(0), cat(1), cat(2), cat(3), b_all, st_old)
        ohg_ref[...] = o_safe
        st_ref[...] = st_safe

    for h in range(G_HEADS):
        outs.append(_head_norm(ohg_ref[:, h * hd:(h + 1) * hd], hgg_ref[h:h + 1, :], False) * hg[h][4])


    ycat = jnp.concatenate(outs, axis=1).astype(BF16)
    y_ref[0] = x + _mm(ycat, wout_ref[...])

    @pl.when(t_id == n_t - 1)
    def _():
        for h in range(G_HEADS):
            shg_ref[0, h] = st_ref[h].T


def _rope_tables(pos):
    half = R_DK // 2
    inv = ROPE_BASE ** (-jnp.arange(half, dtype=F32) / half)
    ang = pos.astype(F32)[:, None] * inv[None, :]
    cos = jnp.cos(ang)
    sin = jnp.sin(ang)
    return jnp.concatenate([cos, cos], axis=-1), jnp.concatenate([-sin, sin], axis=-1)


def _even_prompt(x, g_all, gi, win, wout, retg, hgg, lb, tt):
    bsz, seq, _ = x.shape
    cos, sin = _rope_tables(jnp.arange(seq, dtype=jnp.int32))
    state_spec = pl.BlockSpec((1, 4, 128, 128), lambda b, t: (b, 0, 0, 0))
    return pl.pallas_call(
        functools.partial(_even_prompt_kernel, tt=tt, gi=gi),
        grid=(bsz, seq // tt),
        in_specs=[pl.BlockSpec((1, tt, D_MODEL), lambda b, t: (b, t, 0)),
                  _const_spec(g_all.shape),
                  _const_spec((D_MODEL, EVEN_IN)),
                  _const_spec((EVEN_OUT, D_MODEL)),
                  pl.BlockSpec((tt, 128), lambda b, t: (t, 0)),
                  pl.BlockSpec((tt, 128), lambda b, t: (t, 0)),
                  _const_spec((4, 128)), _const_spec((4, 128)), _const_spec((1, 512))],
        out_specs=[pl.BlockSpec((1, tt, D_MODEL), lambda b, t: (b, t, 0)), state_spec, state_spec],
        out_shape=[jax.ShapeDtypeStruct(x.shape, F32),
                   jax.ShapeDtypeStruct((bsz, 4, 128, 128), F32),
                   jax.ShapeDtypeStruct((bsz, 4, 128, 128), F32)],
        scratch_shapes=[pltpu.VMEM((4, tt, tt), F32), pltpu.VMEM((tt, tt), BF16), pltpu.VMEM((4, 128, 128), F32),
                        pltpu.VMEM((tt, G_HEADS * G_DV), F32)],
        compiler_params=_params(("arbitrary", "arbitrary")),
        name="even_prompt",
    )(x, g_all, win, wout, cos, sin, retg, hgg, lb)


SB = 8


def _even_sample_kernel(x_ref, g_ref, win_ref, wout_ref, cos_ref, sin_ref, retg_ref, hgg_ref, lb_ref,
                        sret_in, shg_in, y_ref, sret_out, shg_out, p_ref, o_ref, *, gi):
    i = pl.program_id(0)
    n_i = pl.num_programs(0)
    hd = 128

    @pl.when(i == 0)
    def _():
        hn = _rms(x_ref[...], g_ref[gi:gi + 1, :]).astype(BF16)
        p_ref[...] = _mm(hn, win_ref[...])

    r0 = pl.multiple_of(i * SB, SB)
    p = p_ref[pl.ds(r0, SB), :]
    cos = cos_ref[...]
    sin = sin_ref[...]
    row = lax.broadcasted_iota(jnp.int32, (SB, hd), 0)
    outs = []
    for h in range(R_HEADS):
        gamma = math.exp(_ret_log_gamma(h))
        pa, pb, pc, rg = _head_parts(p, 0, h)
        q = _rotary(pa, cos, sin).astype(BF16)
        k = _rotary(pb, cos, sin) * (R_DK ** -0.5)
        vb = pc.astype(BF16)
        o = jnp.zeros((SB, hd), F32)
        for j in range(SB):
            kj = jnp.where(row == j, k, 0.0).astype(BF16)
            s_new = gamma * sret_in[j, h] + _tn(kj, vb)
            sret_out[j, h] = s_new
            o = jnp.where(row == j, _mm(q, s_new.astype(BF16)), o)
        outs.append(_head_norm(o, retg_ref[h:h + 1, :], False) * _silu(rg))
    for h in range(G_HEADS):
        pa, pb, pc, gg = _head_parts(p, 1, h)
        qq, kk, ff = _hgrn_gates(pa, pb, lb_ref[:, h * hd:(h + 1) * hd])
        vb = pc.astype(BF16)
        f_cols = jnp.concatenate([ff, jnp.zeros((hd - SB, hd), F32)], axis=0).T
        qb = qq.astype(BF16)
        o = jnp.zeros((SB, hd), F32)
        for j in range(SB):
            kj = jnp.where(row == j, kk, 0.0).astype(BF16)
            s_new = f_cols[:, j:j + 1] * shg_in[j, h] + _tn(kj, vb)
            shg_out[j, h] = s_new
            o = jnp.where(row == j, _mm(qb, s_new.astype(BF16)), o)
        outs.append(_head_norm(o, hgg_ref[h:h + 1, :], False) * _sigmoid(gg))
    o_ref[pl.ds(r0, SB), :] = jnp.concatenate(outs, axis=1)

    @pl.when(i == n_i - 1)
    def _():
        y_ref[...] = x_ref[...] + _mm(o_ref[...].astype(BF16), wout_ref[...])


def _even_sample(x, g_all, gi, win, wout, retg, hgg, lb, sret, shg):
    n = x.shape[0]
    cos, sin = _rope_tables(jnp.full((1,), PAST_LEN, dtype=jnp.int32))
    state_spec = pl.BlockSpec((SB, 4, 128, 128), lambda i: (i, 0, 0, 0))
    return pl.pallas_call(
        functools.partial(_even_sample_kernel, gi=gi),
        grid=(n // SB,),
        in_specs=[_const_spec((n, D_MODEL)),
                  _const_spec(g_all.shape),
                  _const_spec((D_MODEL, EVEN_IN)),
                  _const_spec((EVEN_OUT, D_MODEL)),
                  _const_spec((1, 128)), _const_spec((1, 128)),
                  _const_spec((4, 128)), _const_spec((4, 128)), _const_spec((1, 512)),
                  state_spec, state_spec],
        out_specs=[pl.BlockSpec((n, D_MODEL), lambda i: (0, 0)), state_spec, state_spec],
        out_shape=[jax.ShapeDtypeStruct((n, D_MODEL), F32),
                   jax.ShapeDtypeStruct(sret.shape, F32),
                   jax.ShapeDtypeStruct(shg.shape, F32)],
        scratch_shapes=[pltpu.VMEM((n, EVEN_IN), F32), pltpu.VMEM((n, EVEN_OUT), F32)],
        compiler_params=_params(("arbitrary",)),
        name="even_sample",
    )(x, g_all, win, wout, cos, sin, retg, hgg, lb, sret, shg)


def _block_diag(w):
    wr = w.reshape(M_INNER // BD, BD, QKV_BLOCK)
    tiled = jnp.tile(wr, (1, 1, BD // QKV_BLOCK))
    rb = lax.broadcasted_iota(jnp.int32, (BD, BD), 0) // QKV_BLOCK
    cb = lax.broadcasted_iota(jnp.int32, (BD, BD), 1) // QKV_BLOCK
    return jnp.where((rb == cb)[None], tiled, 0.0)


def _headwise(xb, w_ref, g0=0):
    return jnp.concatenate([_mm(xb[:, g * BD:(g + 1) * BD], w_ref[g0 + g]) for g in range(xb.shape[1] // BD)], axis=1)


def _gate_weights(w_ig, w_fg):
    w = jnp.concatenate([w_ig, w_fg], axis=1)
    w = jnp.pad(w, ((0, 0), (0, 128 - 2 * M_HEADS)))
    return w.reshape(3, M_INNER, 128)


def _mlstm_out(hs, xc, z, normg_ref, skip_ref, wdown_ref):
    hc = jnp.concatenate([_head_norm(hs[h], normg_ref[:, h * M_DH:(h + 1) * M_DH], True) for h in range(M_HEADS)],
                         axis=1)
    hc = hc + skip_ref[...] * xc
    return _mm((hc * _silu(z)).astype(BF16), wdown_ref[...])


def _odd_prompt_kernel(x_ref, g_ref, win_ref, convw_ref, convb_ref, wq_ref, wk_ref, wv_ref,
                       wgate_t_ref, bgate_t_ref, normg_ref, skip_ref, wdown_ref,
                       y_ref, c_ref, n_ref, m_out_ref, conv_out_ref,
                       carry_ref, m_ref, xc_ref, q_ref, k_ref, v_ref, *, tt, gi):
    t_id = pl.program_id(1)
    k_scale = M_DH ** -0.5

    @pl.when(t_id == 0)
    def _():
        c_ref[...] = jnp.zeros_like(c_ref)
        n_ref[...] = jnp.zeros_like(n_ref)
        m_ref[...] = jnp.zeros_like(m_ref)
        carry_ref[...] = jnp.zeros_like(carry_ref)

    x = x_ref[0]
    hn = _rms(x, g_ref[gi:gi + 1, :]).astype(BF16)

    gates_t = bgate_t_ref[...]
    row8 = lax.broadcasted_iota(jnp.int32, (SUBLANES, M_DH), 0)
    tiles = M_DH // BD
    xm_next = _mm(hn, win_ref[:, :M_DH])
    for h in range(M_HEADS):
        sl = slice(h * M_DH, (h + 1) * M_DH)
        xm = xm_next
        if h + 1 < M_HEADS:
            xm_next = _mm(hn, win_ref[:, (h + 1) * M_DH:(h + 2) * M_DH])
        carry = carry_ref[:, sl]
        conv = convb_ref[:, sl] + convw_ref[M_CONV - 1:M_CONV, sl] * xm
        for j in range(1, M_CONV):
            rolled = pltpu.roll(xm, j, 0)
            head = jnp.where(row8 < j, pltpu.roll(carry, j, 0), rolled[:SUBLANES])
            shifted = jnp.concatenate([head, rolled[SUBLANES:]], axis=0)
            conv = conv + convw_ref[M_CONV - 1 - j:M_CONV - j, sl] * shifted
        carry_ref[:, sl] = xm[tt - SUBLANES:, :]
        conv_out_ref[0, :, sl] = xm[tt - (M_CONV - 1):, :]
        xc = _silu(conv)
        xc_ref[:, sl] = xc
        xcb = xc.astype(BF16)
        qb = _headwise(xcb, wq_ref, h * tiles).astype(BF16)
        kb = _headwise(xcb, wk_ref, h * tiles).astype(BF16)
        vb = _headwise(xm.astype(BF16), wv_ref, h * tiles).astype(BF16)
        q_ref[:, sl] = qb
        k_ref[:, sl] = kb
        v_ref[:, sl] = vb
        gates_t = gates_t + (_nt(wgate_t_ref[0, :, sl], qb) + _nt(wgate_t_ref[1, :, sl], kb)
                             + _nt(wgate_t_ref[2, :, sl], vb))

    lane8 = lax.broadcasted_iota(jnp.int32, (SUBLANES, tt), 1)
    row8t = lax.broadcasted_iota(jnp.int32, (SUBLANES, tt), 0)
    brow_all = _log_sigmoid(gates_t)
    d = 1
    while d < tt:
        brow_all = brow_all + jnp.where(lane8 >= d, pltpu.roll(brow_all, d, 1), 0.0)
        d *= 2
    rows = jnp.where(row8t < M_HEADS, gates_t, brow_all)
    pad = jnp.zeros((128 - SUBLANES, 128), F32)
    cols = jnp.concatenate([jnp.concatenate([rows[:, j * 128:(j + 1) * 128], pad], axis=0).T
                            for j in range(tt // 128)], axis=0)

    ti = lax.broadcasted_iota(jnp.int32, (tt, tt), 0)
    si = lax.broadcasted_iota(jnp.int32, (tt, tt), 1)
    causal = si <= ti
    y = x
    for h in range(M_HEADS):
        sl = slice(h * M_DH, (h + 1) * M_DH)
        qhb = q_ref[:, sl]
        khb = k_ref[:, sl]
        vhb = v_ref[:, sl]
        ig_col = cols[:, h:h + 1]
        b_col = cols[:, M_HEADS + h:M_HEADS + h + 1]
        ig_row = gates_t[h:h + 1, :]
        b_row = brow_all[M_HEADS + h:M_HEADS + h + 1, :]
        m_prev = m_ref[h:h + 1, 0:1]
        c_prev = c_ref[0, h]
        n_prev = n_ref[0, h:h + 1, :]

        dlog = jnp.where(causal, b_col + (ig_row - b_row), -jnp.inf)
        inter_log = b_col + m_prev
        m_row = jnp.maximum(inter_log, jnp.max(dlog, axis=-1, keepdims=True))
        w_inter = jnp.exp(inter_log - m_row)
        qk = _nt(qhb, khb) * (jnp.exp(dlog - m_row) * k_scale)
        num = w_inter * _mm(qhb, c_prev.astype(BF16)) + _mm(qk.astype(BF16), vhb)
        qn = _nt(qhb, jnp.broadcast_to(n_prev, (SUBLANES, M_DH)).astype(BF16))[:, 0:1]
        den = w_inter * qn + jnp.sum(qk, axis=-1, keepdims=True)
        den = jnp.maximum(jnp.abs(den), jnp.exp(-m_row))
        hh = num * (1.0 / den)

        b_end = b_col[tt - 1:tt, :]
        s_log = b_end - b_col + ig_col
        m_new = jnp.maximum(b_end + m_prev, jnp.max(s_log, axis=0, keepdims=True))
        a = jnp.exp(b_end + m_prev - m_new)
        kw = khb.astype(F32) * (jnp.exp(s_log - m_new) * k_scale)
        c_ref[0, h] = a * c_prev + _tn(kw.astype(BF16), vhb)
        n_ref[0, h:h + 1, :] = a * n_prev + jnp.sum(kw, axis=0, keepdims=True)
        m_ref[h:h + 1, :] = jnp.broadcast_to(m_new, (1, 128))

        hc = _head_norm(hh, normg_ref[:, sl], True) + skip_ref[:, sl] * xc_ref[:, sl]
        z = _mm(hn, win_ref[:, M_INNER + h * M_DH:M_INNER + (h + 1) * M_DH])
        y = y + _mm((hc * _silu(z)).astype(BF16), wdown_ref[sl, :])

    y_ref[0] = y
    m_out_ref[0] = m_ref[...]


def _odd_prompt(x, g_all, gi, ml, tt):
    bsz, seq, _ = x.shape
    bt = jnp.broadcast_to(ml['bgate_col'], (SUBLANES, tt))
    return pl.pallas_call(
        functools.partial(_odd_prompt_kernel, tt=tt, gi=gi),
        grid=(bsz, seq // tt),
        in_specs=[pl.BlockSpec((1, tt, D_MODEL), lambda b, t: (b, t, 0)),
                  _const_spec(g_all.shape),
                  _const_spec((D_MODEL, 2 * M_INNER)),
                  _const_spec((M_CONV, M_INNER)),
                  _const_spec((1, M_INNER)),
                  _const_spec((M_INNER // BD, BD, BD)),
                  _const_spec((M_INNER // BD, BD, BD)),
                  _const_spec((M_INNER // BD, BD, BD)),
                  _const_spec((3, SUBLANES, M_INNER)),
                  _const_spec((SUBLANES, tt)),
                  _const_spec((1, M_INNER)),
                  _const_spec((1, M_INNER)),
                  _const_spec((M_INNER, D_MODEL))],
        out_specs=[pl.BlockSpec((1, tt, D_MODEL), lambda b, t: (b, t, 0)),
                   pl.BlockSpec((1, M_HEADS, M_DH, M_DH), lambda b, t: (b, 0, 0, 0)),
                   pl.BlockSpec((1, M_HEADS, M_DH), lambda b, t: (b, 0, 0)),
                   pl.BlockSpec((1, SUBLANES, 128), lambda b, t: (b, 0, 0)),
                   pl.BlockSpec((1, M_CONV - 1, M_INNER), lambda b, t: (b, 0, 0))],
        out_shape=[jax.ShapeDtypeStruct(x.shape, F32),
                   jax.ShapeDtypeStruct((bsz, M_HEADS, M_DH, M_DH), F32),
                   jax.ShapeDtypeStruct((bsz, M_HEADS, M_DH), F32),
                   jax.ShapeDtypeStruct((bsz, SUBLANES, 128), F32),
                   jax.ShapeDtypeStruct((bsz, M_CONV - 1, M_INNER), F32)],
        scratch_shapes=[pltpu.VMEM((SUBLANES, M_INNER), F32), pltpu.VMEM((SUBLANES, 128), F32),
                        pltpu.VMEM((tt, M_INNER), F32), pltpu.VMEM((tt, M_INNER), BF16),
                        pltpu.VMEM((tt, M_INNER), BF16), pltpu.VMEM((tt, M_INNER), BF16)],
        compiler_params=_params(("arbitrary", "arbitrary")),
        name="odd_prompt",
    )(x, g_all, ml['win'], ml['convw'], ml['convb'], ml['wq'], ml['wk'], ml['wv'],
      ml['wgate_t'], bt, ml['normg'], ml['skip'], ml['wdown'])


def _pick_row(ref, b):
    r0 = pl.multiple_of((b // SUBLANES) * SUBLANES, SUBLANES)
    blk = ref[pl.ds(r0, SUBLANES), :]
    row = lax.broadcasted_iota(jnp.int32, blk.shape, 0)
    return jnp.sum(jnp.where(row == b % SUBLANES, blk, 0.0), axis=0, keepdims=True)


def _put_row(ref, b, val):
    r0 = pl.multiple_of((b // SUBLANES) * SUBLANES, SUBLANES)
    blk = ref[pl.ds(r0, SUBLANES), :]
    row = lax.broadcasted_iota(jnp.int32, blk.shape, 0)
    ref[pl.ds(r0, SUBLANES), :] = jnp.where(row == b % SUBLANES, jnp.broadcast_to(val, blk.shape), blk)


def _odd_sample_kernel(x_ref, g_ref, win_ref, convw_ref, convb_ref, cv_ref,
                       wq_ref, wk_ref, wv_ref, wgate_ref, bgate_ref, normg_ref, skip_ref, wdown_ref,
                       m_in_ref, c_in, n_in,
                       y_ref, c_out, n_out, m_out_ref, cv_out_ref,
                       q_ref, k_ref, v_ref, gate_ref, xc_ref, z_ref, h_ref, *, gi):
    b = pl.program_id(0)
    n_b = pl.num_programs(0)

    @pl.when(b == 0)
    def _():
        hn = _rms(x_ref[...], g_ref[gi:gi + 1, :]).astype(BF16)
        p = _mm(hn, win_ref[...])
        xm = p[:, :M_INNER]
        z_ref[...] = p[:, M_INNER:]
        cv0 = cv_ref[:, :M_INNER]
        cv1 = cv_ref[:, M_INNER:2 * M_INNER]
        cv2 = cv_ref[:, 2 * M_INNER:]
        cv_out_ref[:, :M_INNER] = cv1
        cv_out_ref[:, M_INNER:2 * M_INNER] = cv2
        cv_out_ref[:, 2 * M_INNER:] = xm
        conv = (convb_ref[...] + convw_ref[3:4, :] * xm + convw_ref[2:3, :] * cv2
                + convw_ref[1:2, :] * cv1 + convw_ref[0:1, :] * cv0)
        xc = _silu(conv)
        xc_ref[...] = xc
        xcb = xc.astype(BF16)
        q = _headwise(xcb, wq_ref)
        k = _headwise(xcb, wk_ref)
        v = _headwise(xm.astype(BF16), wv_ref)
        q_ref[...] = q
        k_ref[...] = k * (M_DH ** -0.5)
        v_ref[...] = v
        gate_ref[...] = (_mm(q.astype(BF16), wgate_ref[0]) + _mm(k.astype(BF16), wgate_ref[1])
                         + _mm(v.astype(BF16), wgate_ref[2])) + bgate_ref[...]
        m_out_ref[...] = jnp.zeros_like(m_out_ref)
        h_ref[...] = jnp.zeros_like(h_ref)

    r0 = pl.multiple_of((b // SUBLANES) * SUBLANES, SUBLANES)
    row8 = lax.broadcasted_iota(jnp.int32, (SUBLANES, M_INNER), 0)
    sel = row8 == b % SUBLANES
    q8 = jnp.where(sel, q_ref[pl.ds(r0, SUBLANES), :], 0.0)
    k8 = jnp.where(sel, k_ref[pl.ds(r0, SUBLANES), :], 0.0)
    v8 = jnp.where(sel, v_ref[pl.ds(r0, SUBLANES), :], 0.0)
    q8b = q8.astype(BF16)
    k8b = k8.astype(BF16)
    v8b = v8.astype(BF16)
    k_row = jnp.sum(k8, axis=0, keepdims=True)
    q_row = jnp.sum(q8, axis=0, keepdims=True)
    gate = _pick_row(gate_ref, b)
    m_all = _pick_row(m_in_ref, b)
    lane = lax.broadcasted_iota(jnp.int32, (1, 128), 1)
    m_new_all = jnp.zeros((1, 128), F32)
    h_parts = []
    for h in range(M_HEADS):
        sl = slice(h * M_DH, (h + 1) * M_DH)
        ig = gate[:, h:h + 1]
        lf = _log_sigmoid(gate[:, M_HEADS + h:M_HEADS + h + 1])
        m_prev = m_all[:, h:h + 1]
        m_new = jnp.maximum(lf + m_prev, ig)
        a = jnp.exp(lf + m_prev - m_new)
        ws = jnp.exp(ig - m_new)
        c_new = a * c_in[0, h] + ws * _tn(k8b[:, sl], v8b[:, sl])
        c_out[0, h] = c_new
        n_new = a * n_in[0, h:h + 1, :] + ws * k_row[:, sl]
        n_out[0, h:h + 1, :] = n_new
        num = jnp.sum(_mm(q8b[:, sl], c_new.astype(BF16)), axis=0, keepdims=True)
        den = jnp.sum(q_row[:, sl] * n_new, axis=-1, keepdims=True)
        den = jnp.maximum(jnp.abs(den), jnp.exp(-m_new))
        h_parts.append(num / den)
        m_new_all = jnp.where(lane == h, m_new, m_new_all)
    _put_row(h_ref, b, jnp.concatenate(h_parts, axis=1))
    _put_row(m_out_ref, b, m_new_all)

    @pl.when(b == n_b - 1)
    def _():
        hfull = h_ref[...]
        hs = [hfull[:, h * M_DH:(h + 1) * M_DH] for h in range(M_HEADS)]
        y_ref[...] = x_ref[...] + _mlstm_out(hs, xc_ref[...], z_ref[...], normg_ref, skip_ref, wdown_ref)


def _odd_sample(x, g_all, gi, ml, c0, n0, m0, conv0):
    n = x.shape[0]
    m_pad = jnp.pad(m0, ((0, 0), (0, 128 - M_HEADS)))
    cw = (M_CONV - 1) * M_INNER
    full = lambda shape: pl.BlockSpec(shape, lambda b: (0,) * len(shape))
    outs = pl.pallas_call(
        functools.partial(_odd_sample_kernel, gi=gi),
        grid=(n,),
        in_specs=[_const_spec((n, D_MODEL)),
                  _const_spec(g_all.shape),
                  _const_spec((D_MODEL, 2 * M_INNER)),
                  _const_spec((M_CONV, M_INNER)),
                  _const_spec((1, M_INNER)),
                  _const_spec((n, cw)),
                  _const_spec((M_INNER // BD, BD, BD)),
                  _const_spec((M_INNER // BD, BD, BD)),
                  _const_spec((M_INNER // BD, BD, BD)),
                  _const_spec((3, M_INNER, 128)),
                  _const_spec((1, 128)),
                  _const_spec((1, M_INNER)),
                  _const_spec((1, M_INNER)),
                  _const_spec((M_INNER, D_MODEL)),
                  _const_spec((n, 128)),
                  pl.BlockSpec((1, M_HEADS, M_DH, M_DH), lambda b: (b, 0, 0, 0)),
                  pl.BlockSpec((1, M_HEADS, M_DH), lambda b: (b, 0, 0))],
        out_specs=[full((n, D_MODEL)),
                   pl.BlockSpec((1, M_HEADS, M_DH, M_DH), lambda b: (b, 0, 0, 0)),
                   pl.BlockSpec((1, M_HEADS, M_DH), lambda b: (b, 0, 0)),
                   full((n, 128)),
                   full((n, cw))],
        out_shape=[jax.ShapeDtypeStruct((n, D_MODEL), F32),
                   jax.ShapeDtypeStruct(c0.shape, F32),
                   jax.ShapeDtypeStruct(n0.shape, F32),
                   jax.ShapeDtypeStruct((n, 128), F32),
                   jax.ShapeDtypeStruct((n, cw), F32)],
        scratch_shapes=[pltpu.VMEM((n, M_INNER), F32), pltpu.VMEM((n, M_INNER), F32), pltpu.VMEM((n, M_INNER), F32),
                        pltpu.VMEM((n, 128), F32), pltpu.VMEM((n, M_INNER), F32), pltpu.VMEM((n, M_INNER), F32),
                        pltpu.VMEM((n, M_INNER), F32)],
        compiler_params=_params(("arbitrary",)),
        name="odd_sample",
    )(x, g_all, ml['win'], ml['convw'], ml['convb'], conv0.reshape(n, cw),
      ml['wq'], ml['wk'], ml['wv'], ml['wgate'], ml['bgate'], ml['normg'], ml['skip'], ml['wdown'],
      m_pad, c0, n0)
    y, c, nn, m_new, conv_new = outs
    return y, c, nn, m_new[:, :M_HEADS], conv_new.reshape(n, M_CONV - 1, M_INNER)


def _odd_sample_proj_kernel(x_ref, g_ref, win_ref, convw_ref, convb_ref, cv_ref, wq_ref, wk_ref, wv_ref,
                            wgate_ref, bgate_ref, m_in_ref, n_in_ref,
                            qt_ref, kwt_ref, v_ref, a_ref, den_ref, n_out_ref, m_out_ref, cv_out_ref,
                            xc_ref, z_ref, *, gi):
    hn = _rms(x_ref[...], g_ref[gi:gi + 1, :]).astype(BF16)
    p = _mm(hn, win_ref[...])
    xm = p[:, :M_INNER]
    z_ref[...] = p[:, M_INNER:]
    cv0 = cv_ref[:, :M_INNER]
    cv1 = cv_ref[:, M_INNER:2 * M_INNER]
    cv2 = cv_ref[:, 2 * M_INNER:]
    cv_out_ref[:, :M_INNER] = cv1
    cv_out_ref[:, M_INNER:2 * M_INNER] = cv2
    cv_out_ref[:, 2 * M_INNER:] = xm
    conv = (convb_ref[...] + convw_ref[3:4, :] * xm + convw_ref[2:3, :] * cv2
            + convw_ref[1:2, :] * cv1 + convw_ref[0:1, :] * cv0)
    xc = _silu(conv)
    xc_ref[...] = xc
    xcb = xc.astype(BF16)
    q = _headwise(xcb, wq_ref)
    k = _headwise(xcb, wk_ref)
    v = _headwise(xm.astype(BF16), wv_ref)
    v_ref[...] = v
    gate = (_mm(q.astype(BF16), wgate_ref[0]) + _mm(k.astype(BF16), wgate_ref[1])
            + _mm(v.astype(BF16), wgate_ref[2])) + bgate_ref[...]
    m_in = m_in_ref[...]
    lane = lax.broadcasted_iota(jnp.int32, m_in.shape, 1)
    a_all = jnp.zeros_like(m_in)
    den_all = jnp.zeros_like(m_in)
    m_all = jnp.zeros_like(m_in)
    for h in range(M_HEADS):
        sl = slice(h * M_DH, (h + 1) * M_DH)
        ig = gate[:, h:h + 1]
        lf = _log_sigmoid(gate[:, M_HEADS + h:M_HEADS + h + 1])
        m_prev = m_in[:, h:h + 1]
        m_new = jnp.maximum(lf + m_prev, ig)
        a = jnp.exp(lf + m_prev - m_new)
        kw = (jnp.exp(ig - m_new) * (M_DH ** -0.5)) * k[:, sl]
        n_new = a * n_in_ref[:, sl] + kw
        n_out_ref[:, sl] = n_new
        qh = q[:, sl]
        den = jnp.maximum(jnp.abs(jnp.sum(qh * n_new, axis=-1, keepdims=True)), jnp.exp(-m_new))
        a_all = jnp.where(lane == h, a, a_all)
        den_all = jnp.where(lane == h, den, den_all)
        m_all = jnp.where(lane == h, m_new, m_all)
        for c in range(M_DH // 128):
            qt_ref[h, c * 128:(c + 1) * 128, :] = qh[:, c * 128:(c + 1) * 128].T
            kwt_ref[h, c * 128:(c + 1) * 128, :] = kw[:, c * 128:(c + 1) * 128].T
    a_ref[...] = a_all
    den_ref[...] = den_all
    m_out_ref[...] = m_all


def _odd_sample_proj(x, g_all, gi, ml, n0, m0, conv0):
    n = x.shape[0]
    assert n == 128, "the per-head transposes assume one 128-lane tile of sequences"
    m_pad = jnp.pad(m0, ((0, 0), (0, 128 - M_HEADS)))
    cw = (M_CONV - 1) * M_INNER
    f32 = lambda *shape: jax.ShapeDtypeStruct(shape, F32)
    shapes = [f32(M_HEADS, M_DH, n), f32(M_HEADS, M_DH, n), f32(n, M_INNER), f32(n, 128), f32(n, 128),
              f32(n, M_INNER), f32(n, 128), f32(n, cw), f32(n, M_INNER), f32(n, M_INNER)]
    args = (x, g_all, ml['win'], ml['convw'], ml['convb'], conv0.reshape(n, cw), ml['wq'], ml['wk'], ml['wv'],
            ml['wgate'], ml['bgate'], m_pad, n0.reshape(n, M_INNER))
    outs = pl.pallas_call(
        functools.partial(_odd_sample_proj_kernel, gi=gi),
        grid=(1,),
        in_specs=[_const_spec(a.shape) for a in args],
        out_specs=[pl.BlockSpec(s.shape, lambda i, nd=len(s.shape): (0,) * nd) for s in shapes],
        out_shape=shapes,
        compiler_params=_params(("arbitrary",)),
        name="odd_sample_proj",
    )(*args)
    qt, kwt, v, a, den, n_new, m_new, conv_new, xc, z = outs
    return (qt, kwt, v, a, den, n_new.reshape(n, M_HEADS, M_DH), m_new[:, :M_HEADS],
            conv_new.reshape(n, M_CONV - 1, M_INNER), xc, z)


STREAM_IN_BUFS = 6
STREAM_OUT_BUFS = 6
STREAM_ROWS = 32


def _ffn_stream_kernel(xp_ref, g_ref, gfin_ref, wg_ref, wu_ref, wd_ref, qt_ref, kwt_ref, v_ref, a_ref, c_in,
                       op_ref, c_out, hnum_ref, cin_buf, cout_buf, sem_in, sem_out, *, gi, per_step):
    i = pl.program_id(0)
    n_pieces = per_step * M_HEADS

    def seq(p):
        return i * per_step + p // M_HEADS

    def in_copy(p, ahead=0):
        slot = p % STREAM_IN_BUFS
        s = seq(p) if not ahead else jnp.minimum(seq(p) + per_step, pl.num_programs(0) * per_step - 1)
        return pltpu.make_async_copy(c_in.at[s, p % M_HEADS], cin_buf.at[slot], sem_in.at[slot])

    def out_copy(p):
        slot = p % STREAM_OUT_BUFS
        return pltpu.make_async_copy(cout_buf.at[slot], c_out.at[seq(p), p % M_HEADS], sem_out.at[slot])

    @pl.when(i == 0)
    def _():
        hnum_ref[...] = jnp.zeros_like(hnum_ref)
        for p in range(STREAM_IN_BUFS):
            in_copy(p).start()

    shift = (128 - (i * per_step) % 128) % 128
    rolled = {}

    def columns(ref, h):
        if (id(ref), h) not in rolled:
            rolled[(id(ref), h)] = pltpu.roll(ref[h], shift, 1)
        return rolled[(id(ref), h)]

    rows = {}

    def seq_rows(j):
        if j not in rows:
            s = i * per_step + j
            rows[j] = (_pick_row(a_ref, s), _pick_row(v_ref, s))
        return rows[j]

    h_parts = {}

    def update(p):
        j, h = divmod(p, M_HEADS)
        a_row, v_row = seq_rows(j)
        kw_col = columns(kwt_ref, h)[:, j:j + 1]
        q_col = columns(qt_ref, h)[:, j:j + 1]
        a = a_row[:, h:h + 1]
        v_h = v_row[:, h * M_DH:(h + 1) * M_DH]
        acc = jnp.zeros((STREAM_ROWS, M_DH), F32)
        for r0 in range(0, M_DH, STREAM_ROWS):
            rs = slice(r0, r0 + STREAM_ROWS)
            c_new = a * cin_buf[p % STREAM_IN_BUFS, rs, :] + kw_col[rs] * v_h
            cout_buf[p % STREAM_OUT_BUFS, rs, :] = c_new
            acc = acc + q_col[rs] * c_new
        h_parts[(j, h)] = jnp.sum(acc, axis=0, keepdims=True)
        if h == M_HEADS - 1:
            _put_row(hnum_ref, i * per_step + j, jnp.concatenate([h_parts[(j, hh)] for hh in range(M_HEADS)], axis=1))

    def region(first, matmul):
        ps = (first, first + 1)
        for p in ps:
            in_copy(p).wait()
            if p >= STREAM_OUT_BUFS:
                out_copy(p - STREAM_OUT_BUFS).wait()
        out = matmul()
        for p in ps:
            update(p)
        for p in ps:
            out_copy(p).start()
            if p + STREAM_IN_BUFS < n_pieces:
                in_copy(p + STREAM_IN_BUFS).start()
            else:
                in_copy(p % STREAM_IN_BUFS, ahead=1).start()
        return out

    x = xp_ref[...]
    hb = _rms(x, g_ref[gi:gi + 1, :]).astype(BF16)
    y = None
    first = 0
    for lo, hi in zip(FFN_BOUNDS[:-1], FFN_BOUNDS[1:]):
        half = lo + (hi - lo + MXU_TILE) // (2 * MXU_TILE) * MXU_TILE
        parts = []
        for w_ref in (wg_ref, wu_ref):
            for c0, c1 in ((lo, half), (half, hi)):
                parts.append(region(first, lambda w_ref=w_ref, c0=c0, c1=c1: _mm(hb, w_ref[:, c0:c1])))
                first += 2
        gt = jnp.concatenate(parts[:2], axis=1)
        ut = jnp.concatenate(parts[2:], axis=1)
        d = _mm((_silu(gt) * ut).astype(BF16), wd_ref[lo:hi, :])
        y = d if y is None else y + d
    assert first == n_pieces
    op_ref[...] = _rms(x + 0.5 * y, gfin_ref[...])
    for p in range(n_pieces - STREAM_OUT_BUFS, n_pieces):
        out_copy(p).wait()

    @pl.when(i == pl.num_programs(0) - 1)
    def _():
        for p in range(STREAM_IN_BUFS):
            in_copy(p, ahead=1).wait()


def _ffn_stream(xp, g_all, gfin, wg, wu, wd, gi, tm, qt, kwt, v, a, c0):
    n = xp.shape[0]
    ns = v.shape[0]
    steps = n // tm
    per_step = ns // steps
    assert per_step * steps == ns and 128 % per_step == 0
    any_spec = pl.BlockSpec(memory_space=pl.ANY)
    return pl.pallas_call(
        functools.partial(_ffn_stream_kernel, gi=gi, per_step=per_step),
        grid=(steps,),
        in_specs=[pl.BlockSpec((tm, D_MODEL), lambda i: (i, 0)),
                  _const_spec(g_all.shape),
                  _const_spec((1, D_MODEL)),
                  _const_spec((D_MODEL, D_FF)), _const_spec((D_MODEL, D_FF)), _const_spec((D_FF, D_MODEL)),
                  _const_spec(qt.shape), _const_spec(kwt.shape), _const_spec(v.shape), _const_spec(a.shape),
                  any_spec],
        out_specs=[pl.BlockSpec((tm, D_MODEL), lambda i: (i, 0)),
                   any_spec,
                   pl.BlockSpec((ns, M_INNER), lambda i: (0, 0))],
        out_shape=[jax.ShapeDtypeStruct((n, D_MODEL), F32),
                   jax.ShapeDtypeStruct(c0.shape, F32),
                   jax.ShapeDtypeStruct((ns, M_INNER), F32)],
        scratch_shapes=[pltpu.VMEM((STREAM_IN_BUFS, M_DH, M_DH), F32), pltpu.VMEM((STREAM_OUT_BUFS, M_DH, M_DH), F32),
                        pltpu.SemaphoreType.DMA((STREAM_IN_BUFS,)), pltpu.SemaphoreType.DMA((STREAM_OUT_BUFS,))],
        compiler_params=_params(("arbitrary",)),
        name="ffn_final_stream",
    )(xp, g_all, gfin, wg, wu, wd, qt, kwt, v, a, c0)


def _odd_sample_out_kernel(x_ref, hnum_ref, den_ref, xc_ref, z_ref, normg_ref, skip_ref, wdown_ref,
                           g_ref, gfin_ref, wg_ref, wu_ref, wd_ref, y_ref, *, gi):
    hnum = hnum_ref[...]
    den = den_ref[...]
    hs = [hnum[:, h * M_DH:(h + 1) * M_DH] * (1.0 / den[:, h:h + 1]) for h in range(M_HEADS)]
    xs = x_ref[...] + _mlstm_out(hs, xc_ref[...], z_ref[...], normg_ref, skip_ref, wdown_ref)
    y_ref[...] = _ffn_rows(xs, g_ref[gi:gi + 1, :], wg_ref, wu_ref, wd_ref, gfin_ref[...])


def _odd_sample_out(x, hnum, den, xc, z, ml, g_all, gfin, wg, wu, wd, gi):
    args = (x, hnum, den, xc, z, ml['normg'], ml['skip'], ml['wdown'], g_all, gfin, wg, wu, wd)
    return pl.pallas_call(
        functools.partial(_odd_sample_out_kernel, gi=gi),
        grid=(1,),
        in_specs=[_const_spec(a.shape) for a in args],
        out_specs=pl.BlockSpec(x.shape, lambda i: (0, 0)),
        out_shape=jax.ShapeDtypeStruct(x.shape, F32),
        compiler_params=_params(("arbitrary",)),
        name="odd_sample_out",
    )(*args)


TM_FFN = 512
TT_EVEN = 256
TT_ODD = 512


def kernel(x_prompt, x_sample, state_ret, state_hgrn, state_mlstm_C, state_mlstm_n, state_mlstm_m, state_mlstm_conv,
           norm_g, final_norm_g, ffn_w_gate, ffn_w_up, ffn_w_down, ev_w_in, ev_w_out, ret_norm_g, hg_norm_g,
           hg_lb_logits, ml_w_in, ml_conv_w, ml_conv_b, ml_w_q, ml_w_k, ml_w_v, ml_w_ig, ml_b_ig, ml_w_fg,
           ml_b_fg, ml_norm_g, ml_skip, ml_w_down):
    bp, tp, _ = x_prompt.shape
    ns = x_sample.shape[0]

    g_all = norm_g.reshape(-1, D_MODEL)
    gfin = final_norm_g.reshape(1, D_MODEL)
    lb_all = jnp.cumsum(jax.nn.softmax(hg_lb_logits.astype(F32), axis=0), axis=0)
    lb = lb_all[0].reshape(1, G_HEADS * G_EXP)
    retg = ret_norm_g[0]
    hgg = hg_norm_g[0]
    wgate = _gate_weights(ml_w_ig[0], ml_w_fg[0])
    bgate = jnp.pad(jnp.concatenate([ml_b_ig[0], ml_b_fg[0]]), (0, 128 - 2 * M_HEADS)).reshape(1, 128)
    ml = {
        'convw': ml_conv_w[0],
        'convb': ml_conv_b[0].reshape(1, M_INNER),
        'wq': _block_diag(ml_w_q[0]).astype(BF16),
        'wk': _block_diag(ml_w_k[0]).astype(BF16),
        'wv': _block_diag(ml_w_v[0]).astype(BF16),
        'wgate': wgate.astype(BF16),
        'wgate_t': jnp.swapaxes(wgate[:, :, :SUBLANES], 1, 2).astype(BF16),
        'bgate': bgate,
        'bgate_col': bgate[0, :SUBLANES].reshape(SUBLANES, 1),
        'normg': ml_norm_g[0].reshape(1, M_INNER),
        'skip': ml_skip[0].reshape(1, M_INNER),
    }

    xp = x_prompt.reshape(bp * tp, D_MODEL)
    xs = x_sample.reshape(ns, D_MODEL)
    ffn_w = (ffn_w_gate, ffn_w_up, ffn_w_down)
    next_ffn = lambda layer, idx: [(w, (layer, idx)) for w in ffn_w]

    w00 = [w[0, 0].astype(BF16) for w in ffn_w]
    xp, xs, cast = _ffn(xp, xs, g_all, gfin, *w00, 0, TM_FFN,
                        casts=next_ffn(0, 1) + [(ev_w_in, (0,)), (ev_w_out, (0,))])
    w01, (ev_in, ev_out) = cast[:3], cast[3:]
    xp, ret_p, hg_p = _even_prompt(xp.reshape(bp, tp, D_MODEL), g_all, 1, ev_in, ev_out, retg, hgg, lb, TT_EVEN)
    xs, ret_s, hg_s = _even_sample(xs, g_all, 1, ev_in, ev_out, retg, hgg, lb, state_ret[:, 0], state_hgrn[:, 0])
    xp, xs, cast = _ffn(xp.reshape(bp * tp, D_MODEL), xs, g_all, gfin, *w01, 2, TM_FFN,
                        casts=next_ffn(1, 0) + [(ml_w_in, (0,)), (ml_w_down, (0,))])
    w10, (ml['win'], ml['wdown']) = cast[:3], cast[3:]
    xp, xs, w11 = _ffn(xp, xs, g_all, gfin, *w10, 3, TM_FFN, casts=next_ffn(1, 1))
    xp, c_p, n_p, m_p, conv_p = _odd_prompt(xp.reshape(bp, tp, D_MODEL), g_all, 4, ml, TT_ODD)
    qt, kwt, v_s, a_s, den_s, n_s, m_s, conv_s, xc_s, z_s = _odd_sample_proj(
        xs, g_all, 4, ml, state_mlstm_n[:, 0], state_mlstm_m[:, 0], state_mlstm_conv[:, 0])
    y_p, c_s, hnum = _ffn_stream(xp.reshape(bp * tp, D_MODEL), g_all, gfin, *w11, 5, TM_FFN,
                                 qt, kwt, v_s, a_s, state_mlstm_C[:, 0])
    y_s = _odd_sample_out(xs, hnum, den_s, xc_s, z_s, ml, g_all, gfin, *w11, 5)

    return (y_p.reshape(bp, tp, D_MODEL), y_s.reshape(ns, 1, D_MODEL),
            ret_p[:, None], hg_p[:, None], c_p[:, None], n_p[:, None], m_p[:, None, :M_HEADS, 0], conv_p[:, None],
            ret_s[:, None], hg_s[:, None], c_s[:, None], n_s[:, None], m_s[:, None], conv_s[:, None])
```

```python
import functools
import math

import jax
import jax.numpy as jnp
from jax import lax
from jax.experimental import pallas as pl
from jax.experimental.pallas import tpu as pltpu

D_MODEL = 1024
PAST_LEN = 16384
R_HEADS = 4
R_DK = 128
R_DV = 128
G_HEADS = 4
G_EXP = 128
G_DV = 128
M_INNER = 2 * D_MODEL
M_HEADS = 4
M_DH = M_INNER // M_HEADS
M_CONV = 4
QKV_BLOCK = 4
D_FF = 2816
EPS = 1e-6
ROPE_BASE = 10000.0
EVEN_IN = 4096
EVEN_OUT = 1024

F32 = jnp.float32
BF16 = jnp.bfloat16

VMEM_LIMIT_BYTES = 56 * 1024 * 1024

HG_CHUNK = 64
HG_SUB = 16
HG_SAFE_LOG_DECAY = -60.0
MXU_TILE = 256
FFN_BOUNDS = (0, 6 * MXU_TILE, D_FF)
BD = MXU_TILE
SUBLANES = 8


def _nt(a, b):
    return lax.dot_general(a, b, (((1,), (1,)), ((), ())), preferred_element_type=F32)


def _tn(a, b):
    return lax.dot_general(a, b, (((0,), (0,)), ((), ())), preferred_element_type=F32)


def _mm(a, b):
    return jnp.dot(a, b, preferred_element_type=F32)


def _sigmoid(x):
    return 1.0 / (1.0 + jnp.exp(-x))


def _silu(x):
    return x * _sigmoid(x)


def _log_sigmoid(x):
    return jnp.minimum(x, 0.0) - jnp.log(1.0 + jnp.exp(-jnp.abs(x)))


def _rms(x, g):
    return x * lax.rsqrt(jnp.mean(x * x, axis=-1, keepdims=True) + EPS) * g


def _head_norm(x, g, center):
    if center:
        x = x - jnp.mean(x, axis=-1, keepdims=True)
    return x * lax.rsqrt(jnp.mean(x * x, axis=-1, keepdims=True) + EPS) * g


def _rotary(x, cos, sin_signed):
    return x * cos + pltpu.roll(x, 64, 1) * sin_signed


def _const_spec(shape):
    n = len(shape)
    return pl.BlockSpec(shape, lambda *_: (0,) * n, pipeline_mode=pl.Buffered(1))


def _params(sem):
    return pltpu.CompilerParams(dimension_semantics=sem, vmem_limit_bytes=VMEM_LIMIT_BYTES)


def _ffn_rows(x, g, wg_ref, wu_ref, wd_ref, gfin):
    h = _rms(x, g).astype(BF16)
    y = jnp.zeros_like(x)
    for lo, hi in zip(FFN_BOUNDS[:-1], FFN_BOUNDS[1:]):
        gt = _mm(h, wg_ref[:, lo:hi])
        ut = _mm(h, wu_ref[:, lo:hi])
        a = (_silu(gt) * ut).astype(BF16)
        y = y + _mm(a, wd_ref[lo:hi, :])
    out = x + 0.5 * y
    if gfin is not None:
        out = _rms(out, gfin)
    return out


def _ffn_kernel(xp_ref, xs_ref, g_ref, gfin_ref, wg_ref, wu_ref, wd_ref, *rest, gi, final, n_cast):
    cast_in = rest[:n_cast]
    op_ref, os_ref = rest[n_cast:n_cast + 2]
    cast_out = rest[n_cast + 2:]
    g = g_ref[gi:gi + 1, :]
    gfin = gfin_ref[...] if final else None
    op_ref[...] = _ffn_rows(xp_ref[...], g, wg_ref, wu_ref, wd_ref, gfin)
    for src, dst in zip(cast_in, cast_out):
        dst[...] = src[...].astype(BF16)

    @pl.when(pl.program_id(0) == pl.num_programs(0) - 1)
    def _():
        os_ref[...] = _ffn_rows(xs_ref[...], g, wg_ref, wu_ref, wd_ref, gfin)


BF16_ROWS = 16


def _cast_specs(arr, lead, steps):
    rows, cols = arr.shape[-2:]
    per = 1 if (rows // steps) % BF16_ROWS == 0 else 2
    br = rows * per // steps
    in_spec = pl.BlockSpec((None,) * len(lead) + (br, cols), lambda i: tuple(lead) + (i // per, 0))
    out_spec = pl.BlockSpec((br, cols), lambda i: (i // per, 0))
    return in_spec, out_spec, jax.ShapeDtypeStruct((rows, cols), BF16)


def _ffn(xp, xs, g_all, gfin, wg, wu, wd, gi, tm, final=False, casts=()):
    n = xp.shape[0]
    ns = xs.shape[0]
    steps = n // tm
    cast_specs = [_cast_specs(arr, lead, steps) for arr, lead in casts]
    outs = pl.pallas_call(
        functools.partial(_ffn_kernel, gi=gi, final=final, n_cast=len(casts)),
        grid=(steps,),
        in_specs=[pl.BlockSpec((tm, D_MODEL), lambda i: (i, 0)),
                  _const_spec((ns, D_MODEL)),
                  _const_spec(g_all.shape),
                  _const_spec((1, D_MODEL)),
                  _const_spec((D_MODEL, D_FF)), _const_spec((D_MODEL, D_FF)), _const_spec((D_FF, D_MODEL))]
                 + [c[0] for c in cast_specs],
        out_specs=[pl.BlockSpec((tm, D_MODEL), lambda i: (i, 0)),
                   pl.BlockSpec((ns, D_MODEL), lambda i: (0, 0))] + [c[1] for c in cast_specs],
        out_shape=[jax.ShapeDtypeStruct((n, D_MODEL), F32), jax.ShapeDtypeStruct((ns, D_MODEL), F32)]
                  + [c[2] for c in cast_specs],
        compiler_params=_params(("arbitrary",)),
        name="ffn_final" if final else "ffn",
    )(xp, xs, g_all, gfin, wg, wu, wd, *[arr for arr, _ in casts])
    return outs[0], outs[1], outs[2:]


def _ret_log_gamma(h):
    return math.log(1.0 - 2.0 ** (-5.0 - h))


def _head_parts(p, mixer, h):
    base = mixer * 4 * R_HEADS * R_DK
    return [p[:, base + (j * R_HEADS + h) * R_DK:base + (j * R_HEADS + h + 1) * R_DK] for j in range(4)]


def _hgrn_gates(gq, gf, lb):
    f = lb + (1.0 - lb) * _sigmoid(gf)
    kk = (1.0 - lb) * _sigmoid(-gf)
    qq = _silu(gq)
    return qq, kk, f


def _split3(x):
    hi = x.astype(BF16)
    r1 = x - hi.astype(F32)
    mid = r1.astype(BF16)
    lo = (r1 - mid.astype(F32)).astype(BF16)
    return hi, mid, lo


def _shift_rows(bases, d):
    base = bases[d % SUBLANES]
    full = (d // SUBLANES) * SUBLANES
    return pltpu.roll(base, full, 0) if full else base


def _hgrn_tile_factorised(qq, kk, vv, b, st):
    tt = qq[0].shape[0]
    heads = range(len(qq))
    qx = [(qq[h] * jnp.exp(b[h])).astype(BF16) for h in heads]
    kx = [kk[h] * jnp.exp(-b[h]) for h in heads]
    kxb = [kx[h].astype(BF16) for h in heads]
    vb = [vv[h].astype(BF16) for h in heads]
    ti = lax.broadcasted_iota(jnp.int32, (HG_CHUNK, HG_CHUNK), 0)
    si = lax.broadcasted_iota(jnp.int32, (HG_CHUNK, HG_CHUNK), 1)
    causal = si <= ti
    st = list(st)
    o_chunks = [[] for _ in heads]
    for c in range(tt // HG_CHUNK):
        rs = slice(c * HG_CHUNK, (c + 1) * HG_CHUNK)
        for h in heads:
            a = jnp.where(causal, _nt(qx[h][rs], kxb[h][rs]), 0.0)
            o_chunks[h].append(_mm(a.astype(BF16), vb[h][rs]) + _nt(qx[h][rs], st[h].astype(BF16)))
            etot = jnp.exp(b[h][(c + 1) * HG_CHUNK - 1:(c + 1) * HG_CHUNK])
            st[h] = st[h] * etot + _tn(vb[h][rs], (kx[h][rs] * etot).astype(BF16))
    return [jnp.concatenate(o_chunks[h], axis=0) for h in heads], st


def _hgrn_tile_guarded(qq_all, kk_all, f_all, vv_all, b_all, st_all):
    tt = qq_all.shape[0]
    hd = G_EXP
    row = lax.broadcasted_iota(jnp.int32, (tt, hd), 0)
    rsub = row % HG_SUB
    sub = (lax.broadcasted_iota(jnp.int32, (HG_CHUNK, hd), 0)) // HG_SUB
    n_sub = HG_CHUNK // HG_SUB
    outs, states = [], []
    for h in range(G_HEADS):
        hs = slice(h * hd, (h + 1) * hd)
        qq = qq_all[:, hs]
        kk = kk_all[:, hs]
        ff = f_all[:, hs]
        bc_all = b_all[:, hs]
        vv = vv_all[:, hs]
        f_sh = [ff] + [pltpu.roll(ff, r, 0) for r in range(1, SUBLANES)]
        k_sh = [kk] + [pltpu.roll(kk, r, 0) for r in range(1, SUBLANES)]
        v_sh = [vv] + [pltpu.roll(vv, r, 0) for r in range(1, SUBLANES)]
        o_band = jnp.sum(qq * kk, axis=-1, keepdims=True) * vv
        decay = None
        for d in range(1, HG_SUB):
            fd = _shift_rows(f_sh, d - 1)
            decay = jnp.where(rsub >= d, fd if decay is None else decay * fd, 0.0)
            term = qq * _shift_rows(k_sh, d) * decay
            o_band = o_band + jnp.sum(term, axis=-1, keepdims=True) * _shift_rows(v_sh, d)
        st = st_all[h]
        o_chunks = []
        for c in range(tt // HG_CHUNK):
            r0 = c * HG_CHUNK
            bc = bc_all[r0:r0 + HG_CHUNK]
            qc = qq[r0:r0 + HG_CHUNK]
            kc = kk[r0:r0 + HG_CHUNK]
            vcb = vv[r0:r0 + HG_CHUNK].astype(BF16)
            refs = [bc[i * HG_SUB - 1:i * HG_SUB] for i in range(1, n_sub)]
            refrow = refs[-1]
            for i in range(n_sub - 2, 0, -1):
                refrow = jnp.where(sub == i, refs[i - 1], refrow)
            qp = qc * jnp.exp(bc - refrow)
            lhs = jnp.concatenate([jnp.where(sub == i, qp, 0.0) for i in range(1, n_sub)], axis=1)
            kcat = jnp.concatenate([jnp.where(sub < i, kc * jnp.exp(refs[i - 1] - bc), 0.0)
                                    for i in range(1, n_sub)], axis=1)
            a = _nt(lhs.astype(BF16), kcat.astype(BF16))
            qb = (qc * jnp.exp(bc)).astype(BF16)
            o_chunks.append(_mm(a.astype(BF16), vcb) + _nt(qb, st.astype(BF16)))
            btot = bc[HG_CHUNK - 1:HG_CHUNK]
            ke = (kc * jnp.exp(btot - bc)).astype(BF16)
            st = st * jnp.exp(btot) + _tn(vcb, ke)
        outs.append(o_band + jnp.concatenate(o_chunks, axis=0))
        states.append(st)
    return jnp.concatenate(outs, axis=1), jnp.stack(states)


def _even_prompt_kernel(x_ref, g_ref, win_ref, wout_ref, cos_ref, sin_ref, retg_ref, hgg_ref, lb_ref,
                        y_ref, sret_ref, shg_ref, dmat_ref, tri_ref, st_ref, ohg_ref, *, tt, gi):
    b_id = pl.program_id(0)
    t_id = pl.program_id(1)
    n_t = pl.num_programs(1)
    hd = 128

    @pl.when(jnp.logical_and(b_id == 0, t_id == 0))
    def _():
        ti = lax.broadcasted_iota(jnp.int32, (tt, tt), 0)
        si = lax.broadcasted_iota(jnp.int32, (tt, tt), 1)
        diff = (ti - si).astype(F32)
        for h in range(R_HEADS):
            dmat_ref[h] = jnp.where(diff >= 0.0, jnp.exp(_ret_log_gamma(h) * jnp.maximum(diff, 0.0)), 0.0)
        same_chunk = (ti // HG_CHUNK) == (si // HG_CHUNK)
        tri_ref[...] = jnp.where(jnp.logical_and(same_chunk, si <= ti), 1.0, 0.0).astype(BF16)

    @pl.when(t_id == 0)
    def _():
        sret_ref[...] = jnp.zeros_like(sret_ref)
        st_ref[...] = jnp.zeros_like(st_ref)

    x = x_ref[0]
    hn = _rms(x, g_ref[gi:gi + 1, :]).astype(BF16)
    cos = cos_ref[...]
    sin = sin_ref[...]
    rowf = lax.broadcasted_iota(jnp.int32, (tt, hd), 0).astype(F32)
    outs = []
    hg = []
    logf_parts = []
    st_old = st_ref[...]

    p = _mm(hn, win_ref[...])
    for i in range(R_HEADS + G_HEADS):
        h = i // 2
        pa, pb, pc, pd = _head_parts(p, i % 2, h)
        if i % 2 == 0:
            lg = _ret_log_gamma(h)
            q = _rotary(pa, cos, sin)
            k = _rotary(pb, cos, sin) * (R_DK ** -0.5)
            v = pc
            rg = pd
            s0 = sret_ref[0, h]
            qb = q.astype(BF16)
            vb = v.astype(BF16)
            inter = _mm(qb, s0.astype(BF16)) * jnp.exp(lg * (rowf + 1.0))
            scores = _nt(qb, k.astype(BF16)) * dmat_ref[h]
            intra = _mm(scores.astype(BF16), vb)
            kd = (k * jnp.exp(lg * (tt - 1.0 - rowf))).astype(BF16)
            sret_ref[0, h] = math.exp(lg * tt) * s0 + _tn(kd, vb)
            outs.append(_head_norm(inter + intra, retg_ref[h:h + 1, :], False) * _silu(rg))
        else:
            qq, kk, ff = _hgrn_gates(pa, pb, lb_ref[:, h * hd:(h + 1) * hd])
            hg.append((qq, kk, ff, pc, _sigmoid(pd)))
            logf_parts.extend(_split3(jnp.log(ff)))

    cs = _mm(tri_ref[...], jnp.concatenate(logf_parts, axis=1))
    b = [cs[:, (3 * h) * hd:(3 * h + 1) * hd] + cs[:, (3 * h + 1) * hd:(3 * h + 2) * hd]
         + cs[:, (3 * h + 2) * hd:(3 * h + 3) * hd] for h in range(G_HEADS)]
    b_all = jnp.concatenate(b, axis=1)
    o_fast, st_fast = _hgrn_tile_factorised([t[0] for t in hg], [t[1] for t in hg], [t[3] for t in hg], b,
                                            [st_old[h] for h in range(G_HEADS)])
    for h in range(G_HEADS):
        ohg_ref[:, h * hd:(h + 1) * hd] = o_fast[h]
        st_ref[h] = st_fast[h]

    @pl.when(jnp.min(b_all) < HG_SAFE_LOG_DECAY)
    def _():
        cat = lambda j: jnp.concatenate([t[j] for t in hg], axis=1)
        o_safe, st_safe = _hgrn_tile_guarded(cat(0), cat(1), cat(2), cat(3), b_all, st_old)
        ohg_ref[...] = o_safe
        st_ref[...] = st_safe

    for h in range(G_HEADS):
        outs.append(_head_norm(ohg_ref[:, h * hd:(h + 1) * hd], hgg_ref[h:h + 1, :], False) * hg[h][4])


    ycat = jnp.concatenate(outs, axis=1).astype(BF16)
    y_ref[0] = x + _mm(ycat, wout_ref[...])

    @pl.when(t_id == n_t - 1)
    def _():
        for h in range(G_HEADS):
            shg_ref[0, h] = st_ref[h].T


def _rope_tables(pos):
    half = R_DK // 2
    inv = ROPE_BASE ** (-jnp.arange(half, dtype=F32) / half)
    ang = pos.astype(F32)[:, None] * inv[None, :]
    cos = jnp.cos(ang)
    sin = jnp.sin(ang)
    return jnp.concatenate([cos, cos], axis=-1), jnp.concatenate([-sin, sin], axis=-1)


def _even_prompt(x, g_all, gi, win, wout, retg, hgg, lb, tt):
    bsz, seq, _ = x.shape
    cos, sin = _rope_tables(jnp.arange(seq, dtype=jnp.int32))
    state_spec = pl.BlockSpec((1, 4, 128, 128), lambda b, t: (b, 0, 0, 0))
    return pl.pallas_call(
        functools.partial(_even_prompt_kernel, tt=tt, gi=gi),
        grid=(bsz, seq // tt),
        in_specs=[pl.BlockSpec((1, tt, D_MODEL), lambda b, t: (b, t, 0)),
                  _const_spec(g_all.shape),
                  _const_spec((D_MODEL, EVEN_IN)),
                  _const_spec((EVEN_OUT, D_MODEL)),
                  pl.BlockSpec((tt, 128), lambda b, t: (t, 0)),
                  pl.BlockSpec((tt, 128), lambda b, t: (t, 0)),
                  _const_spec((4, 128)), _const_spec((4, 128)), _const_spec((1, 512))],
        out_specs=[pl.BlockSpec((1, tt, D_MODEL), lambda b, t: (b, t, 0)), state_spec, state_spec],
        out_shape=[jax.ShapeDtypeStruct(x.shape, F32),
                   jax.ShapeDtypeStruct((bsz, 4, 128, 128), F32),
                   jax.ShapeDtypeStruct((bsz, 4, 128, 128), F32)],
        scratch_shapes=[pltpu.VMEM((4, tt, tt), F32), pltpu.VMEM((tt, tt), BF16), pltpu.VMEM((4, 128, 128), F32),
                        pltpu.VMEM((tt, G_HEADS * G_DV), F32)],
        compiler_params=_params(("arbitrary", "arbitrary")),
        name="even_prompt",
    )(x, g_all, win, wout, cos, sin, retg, hgg, lb)


SB = 8


def _even_sample_kernel(x_ref, g_ref, win_ref, wout_ref, cos_ref, sin_ref, retg_ref, hgg_ref, lb_ref,
                        sret_in, shg_in, y_ref, sret_out, shg_out, p_ref, o_ref, *, gi):
    i = pl.program_id(0)
    n_i = pl.num_programs(0)
    hd = 128

    @pl.when(i == 0)
    def _():
        hn = _rms(x_ref[...], g_ref[gi:gi + 1, :]).astype(BF16)
        p_ref[...] = _mm(hn, win_ref[...])

    r0 = pl.multiple_of(i * SB, SB)
    p = p_ref[pl.ds(r0, SB), :]
    cos = cos_ref[...]
    sin = sin_ref[...]
    row = lax.broadcasted_iota(jnp.int32, (SB, hd), 0)
    outs = []
    for h in range(R_HEADS):
        gamma = math.exp(_ret_log_gamma(h))
        pa, pb, pc, rg = _head_parts(p, 0, h)
        q = _rotary(pa, cos, sin).astype(BF16)
        k = _rotary(pb, cos, sin) * (R_DK ** -0.5)
        vb = pc.astype(BF16)
        o = jnp.zeros((SB, hd), F32)
        for j in range(SB):
            kj = jnp.where(row == j, k, 0.0).astype(BF16)
            s_new = gamma * sret_in[j, h] + _tn(kj, vb)
            sret_out[j, h] = s_new
            o = jnp.where(row == j, _mm(q, s_new.astype(BF16)), o)
        outs.append(_head_norm(o, retg_ref[h:h + 1, :], False) * _silu(rg))
    for h in range(G_HEADS):
        pa, pb, pc, gg = _head_parts(p, 1, h)
        qq, kk, ff = _hgrn_gates(pa, pb, lb_ref[:, h * hd:(h + 1) * hd])
        vb = pc.astype(BF16)
        f_cols = jnp.concatenate([ff, jnp.zeros((hd - SB, hd), F32)], axis=0).T
        qb = qq.astype(BF16)
        o = jnp.zeros((SB, hd), F32)
        for j in range(SB):
            kj = jnp.where(row == j, kk, 0.0).astype(BF16)
            s_new = f_cols[:, j:j + 1] * shg_in[j, h] + _tn(kj, vb)
            shg_out[j, h] = s_new
            o = jnp.where(row == j, _mm(qb, s_new.astype(BF16)), o)
        outs.append(_head_norm(o, hgg_ref[h:h + 1, :], False) * _sigmoid(gg))
    o_ref[pl.ds(r0, SB), :] = jnp.concatenate(outs, axis=1)

    @pl.when(i == n_i - 1)
    def _():
        y_ref[...] = x_ref[...] + _mm(o_ref[...].astype(BF16), wout_ref[...])


def _even_sample(x, g_all, gi, win, wout, retg, hgg, lb, sret, shg):
    n = x.shape[0]
    cos, sin = _rope_tables(jnp.full((1,), PAST_LEN, dtype=jnp.int32))
    state_spec = pl.BlockSpec((SB, 4, 128, 128), lambda i: (i, 0, 0, 0))
    return pl.pallas_call(
        functools.partial(_even_sample_kernel, gi=gi),
        grid=(n // SB,),
        in_specs=[_const_spec((n, D_MODEL)),
                  _const_spec(g_all.shape),
                  _const_spec((D_MODEL, EVEN_IN)),
                  _const_spec((EVEN_OUT, D_MODEL)),
                  _const_spec((1, 128)), _const_spec((1, 128)),
                  _const_spec((4, 128)), _const_spec((4, 128)), _const_spec((1, 512)),
                  state_spec, state_spec],
        out_specs=[pl.BlockSpec((n, D_MODEL), lambda i: (0, 0)), state_spec, state_spec],
        out_shape=[jax.ShapeDtypeStruct((n, D_MODEL), F32),
                   jax.ShapeDtypeStruct(sret.shape, F32),
                   jax.ShapeDtypeStruct(shg.shape, F32)],
        scratch_shapes=[pltpu.VMEM((n, EVEN_IN), F32), pltpu.VMEM((n, EVEN_OUT), F32)],
        compiler_params=_params(("arbitrary",)),
        name="even_sample",
    )(x, g_all, win, wout, cos, sin, retg, hgg, lb, sret, shg)


def _block_diag(w):
    wr = w.reshape(M_INNER // BD, BD, QKV_BLOCK)
    tiled = jnp.tile(wr, (1, 1, BD // QKV_BLOCK))
    rb = lax.broadcasted_iota(jnp.int32, (BD, BD), 0) // QKV_BLOCK
    cb = lax.broadcasted_iota(jnp.int32, (BD, BD), 1) // QKV_BLOCK
    return jnp.where((rb == cb)[None], tiled, 0.0)


def _headwise(xb, w_ref, g0=0):
    return jnp.concatenate([_mm(xb[:, g * BD:(g + 1) * BD], w_ref[g0 + g]) for g in range(xb.shape[1] // BD)], axis=1)


def _gate_weights(w_ig, w_fg):
    w = jnp.concatenate([w_ig, w_fg], axis=1)
    w = jnp.pad(w, ((0, 0), (0, 128 - 2 * M_HEADS)))
    return w.reshape(3, M_INNER, 128)


def _mlstm_out(hs, xc, z, normg_ref, skip_ref, wdown_ref):
    hc = jnp.concatenate([_head_norm(hs[h], normg_ref[:, h * M_DH:(h + 1) * M_DH], True) for h in range(M_HEADS)],
                         axis=1)
    hc = hc + skip_ref[...] * xc
    return _mm((hc * _silu(z)).astype(BF16), wdown_ref[...])


def _odd_prompt_kernel(x_ref, g_ref, win_ref, convw_ref, convb_ref, wq_ref, wk_ref, wv_ref,
                       wgate_t_ref, bgate_t_ref, normg_ref, skip_ref, wdown_ref,
                       y_ref, c_ref, n_ref, m_out_ref, conv_out_ref,
                       carry_ref, m_ref, xc_ref, q_ref, k_ref, v_ref, *, tt, gi):
    t_id = pl.program_id(1)
    k_scale = M_DH ** -0.5

    @pl.when(t_id == 0)
    def _():
        c_ref[...] = jnp.zeros_like(c_ref)
        n_ref[...] = jnp.zeros_like(n_ref)
        m_ref[...] = jnp.zeros_like(m_ref)
        carry_ref[...] = jnp.zeros_like(carry_ref)

    x = x_ref[0]
    hn = _rms(x, g_ref[gi:gi + 1, :]).astype(BF16)

    gates_t = bgate_t_ref[...]
    row8 = lax.broadcasted_iota(jnp.int32, (SUBLANES, M_DH), 0)
    tiles = M_DH // BD
    xm_next = _mm(hn, win_ref[:, :M_DH])
    for h in range(M_HEADS):
        sl = slice(h * M_DH, (h + 1) * M_DH)
        xm = xm_next
        if h + 1 < M_HEADS:
            xm_next = _mm(hn, win_ref[:, (h + 1) * M_DH:(h + 2) * M_DH])
        carry = carry_ref[:, sl]
        conv = convb_ref[:, sl] + convw_ref[M_CONV - 1:M_CONV, sl] * xm
        for j in range(1, M_CONV):
            rolled = pltpu.roll(xm, j, 0)
            head = jnp.where(row8 < j, pltpu.roll(carry, j, 0), rolled[:SUBLANES])
            shifted = jnp.concatenate([head, rolled[SUBLANES:]], axis=0)
            conv = conv + convw_ref[M_CONV - 1 - j:M_CONV - j, sl] * shifted
        carry_ref[:, sl] = xm[tt - SUBLANES:, :]
        conv_out_ref[0, :, sl] = xm[tt - (M_CONV - 1):, :]
        xc = _silu(conv)
        xc_ref[:, sl] = xc
        xcb = xc.astype(BF16)
        qb = _headwise(xcb, wq_ref, h * tiles).astype(BF16)
        kb = _headwise(xcb, wk_ref, h * tiles).astype(BF16)
        vb = _headwise(xm.astype(BF16), wv_ref, h * tiles).astype(BF16)
        q_ref[:, sl] = qb
        k_ref[:, sl] = kb
        v_ref[:, sl] = vb
        gates_t = gates_t + (_nt(wgate_t_ref[0, :, sl], qb) + _nt(wgate_t_ref[1, :, sl], kb)
                             + _nt(wgate_t_ref[2, :, sl], vb))

    lane8 = lax.broadcasted_iota(jnp.int32, (SUBLANES, tt), 1)
    row8t = lax.broadcasted_iota(jnp.int32, (SUBLANES, tt), 0)
    brow_all = _log_sigmoid(gates_t)
    d = 1
    while d < tt:
        brow_all = brow_all + jnp.where(lane8 >= d, pltpu.roll(brow_all, d, 1), 0.0)
        d *= 2
    rows = jnp.where(row8t < M_HEADS, gates_t, brow_all)
    pad = jnp.zeros((128 - SUBLANES, 128), F32)
    cols = jnp.concatenate([jnp.concatenate([rows[:, j * 128:(j + 1) * 128], pad], axis=0).T
                            for j in range(tt // 128)], axis=0)

    ti = lax.broadcasted_iota(jnp.int32, (tt, tt), 0)
    si = lax.broadcasted_iota(jnp.int32, (tt, tt), 1)
    causal = si <= ti
    y = x
    for h in range(M_HEADS):
        sl = slice(h * M_DH, (h + 1) * M_DH)
        qhb = q_ref[:, sl]
        khb = k_ref[:, sl]
        vhb = v_ref[:, sl]
        ig_col = cols[:, h:h + 1]
        b_col = cols[:, M_HEADS + h:M_HEADS + h + 1]
        ig_row = gates_t[h:h + 1, :]
        b_row = brow_all[M_HEADS + h:M_HEADS + h + 1, :]
        m_prev = m_ref[h:h + 1, 0:1]
        c_prev = c_ref[0, h]
        n_prev = n_ref[0, h:h + 1, :]

        dlog = jnp.where(causal, b_col + (ig_row - b_row), -jnp.inf)
        inter_log = b_col + m_prev
        m_row = jnp.maximum(inter_log, jnp.max(dlog, axis=-1, keepdims=True))
        w_inter = jnp.exp(inter_log - m_row)
        qk = _nt(qhb, khb) * (jnp.exp(dlog - m_row) * k_scale)
        num = w_inter * _mm(qhb, c_prev.astype(BF16)) + _mm(qk.astype(BF16), vhb)
        qn = _nt(qhb, jnp.broadcast_to(n_prev, (SUBLANES, M_DH)).astype(BF16))[:, 0:1]
        den = w_inter * qn + jnp.sum(qk, axis=-1, keepdims=True)
        den = jnp.maximum(jnp.abs(den), jnp.exp(-m_row))
        hh = num * (1.0 / den)

        b_end = b_col[tt - 1:tt, :]
        s_log = b_end - b_col + ig_col
        m_new = jnp.maximum(b_end + m_prev, jnp.max(s_log, axis=0, keepdims=True))
        a = jnp.exp(b_end + m_prev - m_new)
        kw = khb.astype(F32) * (jnp.exp(s_log - m_new) * k_scale)
        c_ref[0, h] = a * c_prev + _tn(kw.astype(BF16), vhb)
        n_ref[0, h:h + 1, :] = a * n_prev + jnp.sum(kw, axis=0, keepdims=True)
        m_ref[h:h + 1, :] = jnp.broadcast_to(m_new, (1, 128))

        hc = _head_norm(hh, normg_ref[:, sl], True) + skip_ref[:, sl] * xc_ref[:, sl]
        z = _mm(hn, win_ref[:, M_INNER + h * M_DH:M_INNER + (h + 1) * M_DH])
        y = y + _mm((hc * _silu(z)).astype(BF16), wdown_ref[sl, :])

    y_ref[0] = y
    m_out_ref[0] = m_ref[...]


def _odd_prompt(x, g_all, gi, ml, tt):
    bsz, seq, _ = x.shape
    bt = jnp.broadcast_to(ml['bgate_col'], (SUBLANES, tt))
    return pl.pallas_call(
        functools.partial(_odd_prompt_kernel, tt=tt, gi=gi),
        grid=(bsz, seq // tt),
        in_specs=[pl.BlockSpec((1, tt, D_MODEL), lambda b, t: (b, t, 0)),
                  _const_spec(g_all.shape),
                  _const_spec((D_MODEL, 2 * M_INNER)),
                  _const_spec((M_CONV, M_INNER)),
                  _const_spec((1, M_INNER)),
                  _const_spec((M_INNER // BD, BD, BD)),
                  _const_spec((M_INNER // BD, BD, BD)),
                  _const_spec((M_INNER // BD, BD, BD)),
                  _const_spec((3, SUBLANES, M_INNER)),
                  _const_spec((SUBLANES, tt)),
                  _const_spec((1, M_INNER)),
                  _const_spec((1, M_INNER)),
                  _const_spec((M_INNER, D_MODEL))],
        out_specs=[pl.BlockSpec((1, tt, D_MODEL), lambda b, t: (b, t, 0)),
                   pl.BlockSpec((1, M_HEADS, M_DH, M_DH), lambda b, t: (b, 0, 0, 0)),
                   pl.BlockSpec((1, M_HEADS, M_DH), lambda b, t: (b, 0, 0)),
                   pl.BlockSpec((1, SUBLANES, 128), lambda b, t: (b, 0, 0)),
                   pl.BlockSpec((1, M_CONV - 1, M_INNER), lambda b, t: (b, 0, 0))],
        out_shape=[jax.ShapeDtypeStruct(x.shape, F32),
                   jax.ShapeDtypeStruct((bsz, M_HEADS, M_DH, M_DH), F32),
                   jax.ShapeDtypeStruct((bsz, M_HEADS, M_DH), F32),
                   jax.ShapeDtypeStruct((bsz, SUBLANES, 128), F32),
                   jax.ShapeDtypeStruct((bsz, M_CONV - 1, M_INNER), F32)],
        scratch_shapes=[pltpu.VMEM((SUBLANES, M_INNER), F32), pltpu.VMEM((SUBLANES, 128), F32),
                        pltpu.VMEM((tt, M_INNER), F32), pltpu.VMEM((tt, M_INNER), BF16),
                        pltpu.VMEM((tt, M_INNER), BF16), pltpu.VMEM((tt, M_INNER), BF16)],
        compiler_params=_params(("arbitrary", "arbitrary")),
        name="odd_prompt",
    )(x, g_all, ml['win'], ml['convw'], ml['convb'], ml['wq'], ml['wk'], ml['wv'],
      ml['wgate_t'], bt, ml['normg'], ml['skip'], ml['wdown'])


def _pick_row(ref, b):
    r0 = pl.multiple_of((b // SUBLANES) * SUBLANES, SUBLANES)
    blk = ref[pl.ds(r0, SUBLANES), :]
    row = lax.broadcasted_iota(jnp.int32, blk.shape, 0)
    return jnp.sum(jnp.where(row == b % SUBLANES, blk, 0.0), axis=0, keepdims=True)


def _put_row(ref, b, val):
    r0 = pl.multiple_of((b // SUBLANES) * SUBLANES, SUBLANES)
    blk = ref[pl.ds(r0, SUBLANES), :]
    row = lax.broadcasted_iota(jnp.int32, blk.shape, 0)
    ref[pl.ds(r0, SUBLANES), :] = jnp.where(row == b % SUBLANES, jnp.broadcast_to(val, blk.shape), blk)


def _odd_sample_kernel(x_ref, g_ref, win_ref, convw_ref, convb_ref, cv_ref,
                       wq_ref, wk_ref, wv_ref, wgate_ref, bgate_ref, normg_ref, skip_ref, wdown_ref,
                       m_in_ref, c_in, n_in,
                       y_ref, c_out, n_out, m_out_ref, cv_out_ref,
                       q_ref, k_ref, v_ref, gate_ref, xc_ref, z_ref, h_ref, *, gi):
    b = pl.program_id(0)
    n_b = pl.num_programs(0)

    @pl.when(b == 0)
    def _():
        hn = _rms(x_ref[...], g_ref[gi:gi + 1, :]).astype(BF16)
        p = _mm(hn, win_ref[...])
        xm = p[:, :M_INNER]
        z_ref[...] = p[:, M_INNER:]
        cv0 = cv_ref[:, :M_INNER]
        cv1 = cv_ref[:, M_INNER:2 * M_INNER]
        cv2 = cv_ref[:, 2 * M_INNER:]
        cv_out_ref[:, :M_INNER] = cv1
        cv_out_ref[:, M_INNER:2 * M_INNER] = cv2
        cv_out_ref[:, 2 * M_INNER:] = xm
        conv = (convb_ref[...] + convw_ref[3:4, :] * xm + convw_ref[2:3, :] * cv2
                + convw_ref[1:2, :] * cv1 + convw_ref[0:1, :] * cv0)
        xc = _silu(conv)
        xc_ref[...] = xc
        xcb = xc.astype(BF16)
        q = _headwise(xcb, wq_ref)
        k = _headwise(xcb, wk_ref)
        v = _headwise(xm.astype(BF16), wv_ref)
        q_ref[...] = q
        k_ref[...] = k * (M_DH ** -0.5)
        v_ref[...] = v
        gate_ref[...] = (_mm(q.astype(BF16), wgate_ref[0]) + _mm(k.astype(BF16), wgate_ref[1])
                         + _mm(v.astype(BF16), wgate_ref[2])) + bgate_ref[...]
        m_out_ref[...] = jnp.zeros_like(m_out_ref)
        h_ref[...] = jnp.zeros_like(h_ref)

    r0 = pl.multiple_of((b // SUBLANES) * SUBLANES, SUBLANES)
    row8 = lax.broadcasted_iota(jnp.int32, (SUBLANES, M_INNER), 0)
    sel = row8 == b % SUBLANES
    q8 = jnp.where(sel, q_ref[pl.ds(r0, SUBLANES), :], 0.0)
    k8 = jnp.where(sel, k_ref[pl.ds(r0, SUBLANES), :], 0.0)
    v8 = jnp.where(sel, v_ref[pl.ds(r0, SUBLANES), :], 0.0)
    q8b = q8.astype(BF16)
    k8b = k8.astype(BF16)
    v8b = v8.astype(BF16)
    k_row = jnp.sum(k8, axis=0, keepdims=True)
    q_row = jnp.sum(q8, axis=0, keepdims=True)
    gate = _pick_row(gate_ref, b)
    m_all = _pick_row(m_in_ref, b)
    lane = lax.broadcasted_iota(jnp.int32, (1, 128), 1)
    m_new_all = jnp.zeros((1, 128), F32)
    h_parts = []
    for h in range(M_HEADS):
        sl = slice(h * M_DH, (h + 1) * M_DH)
        ig = gate[:, h:h + 1]
        lf = _log_sigmoid(gate[:, M_HEADS + h:M_HEADS + h + 1])
        m_prev = m_all[:, h:h + 1]
        m_new = jnp.maximum(lf + m_prev, ig)
        a = jnp.exp(lf + m_prev - m_new)
        ws = jnp.exp(ig - m_new)
        c_new = a * c_in[0, h] + ws * _tn(k8b[:, sl], v8b[:, sl])
        c_out[0, h] = c_new
        n_new = a * n_in[0, h:h + 1, :] + ws * k_row[:, sl]
        n_out[0, h:h + 1, :] = n_new
        num = jnp.sum(_mm(q8b[:, sl], c_new.astype(BF16)), axis=0, keepdims=True)
        den = jnp.sum(q_row[:, sl] * n_new, axis=-1, keepdims=True)
        den = jnp.maximum(jnp.abs(den), jnp.exp(-m_new))
        h_parts.append(num / den)
        m_new_all = jnp.where(lane == h, m_new, m_new_all)
    _put_row(h_ref, b, jnp.concatenate(h_parts, axis=1))
    _put_row(m_out_ref, b, m_new_all)

    @pl.when(b == n_b - 1)
    def _():
        hfull = h_ref[...]
        hs = [hfull[:, h * M_DH:(h + 1) * M_DH] for h in range(M_HEADS)]
        y_ref[...] = x_ref[...] + _mlstm_out(hs, xc_ref[...], z_ref[...], normg_ref, skip_ref, wdown_ref)


def _odd_sample(x, g_all, gi, ml, c0, n0, m0, conv0):
    n = x.shape[0]
    m_pad = jnp.pad(m0, ((0, 0), (0, 128 - M_HEADS)))
    cw = (M_CONV - 1) * M_INNER
    full = lambda shape: pl.BlockSpec(shape, lambda b: (0,) * len(shape))
    outs = pl.pallas_call(
        functools.partial(_odd_sample_kernel, gi=gi),
        grid=(n,),
        in_specs=[_const_spec((n, D_MODEL)),
                  _const_spec(g_all.shape),
                  _const_spec((D_MODEL, 2 * M_INNER)),
                  _const_spec((M_CONV, M_INNER)),
                  _const_spec((1, M_INNER)),
                  _const_spec((n, cw)),
                  _const_spec((M_INNER // BD, BD, BD)),
                  _const_spec((M_INNER // BD, BD, BD)),
                  _const_spec((M_INNER // BD, BD, BD)),
                  _const_spec((3, M_INNER, 128)),
                  _const_spec((1, 128)),
                  _const_spec((1, M_INNER)),
                  _const_spec((1, M_INNER)),
                  _const_spec((M_INNER, D_MODEL)),
                  _const_spec((n, 128)),
                  pl.BlockSpec((1, M_HEADS, M_DH, M_DH), lambda b: (b, 0, 0, 0)),
                  pl.BlockSpec((1, M_HEADS, M_DH), lambda b: (b, 0, 0))],
        out_specs=[full((n, D_MODEL)),
                   pl.BlockSpec((1, M_HEADS, M_DH, M_DH), lambda b: (b, 0, 0, 0)),
                   pl.BlockSpec((1, M_HEADS, M_DH), lambda b: (b, 0, 0)),
                   full((n, 128)),
                   full((n, cw))],
        out_shape=[jax.ShapeDtypeStruct((n, D_MODEL), F32),
                   jax.ShapeDtypeStruct(c0.shape, F32),
                   jax.ShapeDtypeStruct(n0.shape, F32),
                   jax.ShapeDtypeStruct((n, 128), F32),
                   jax.ShapeDtypeStruct((n, cw), F32)],
        scratch_shapes=[pltpu.VMEM((n, M_INNER), F32), pltpu.VMEM((n, M_INNER), F32), pltpu.VMEM((n, M_INNER), F32),
                        pltpu.VMEM((n, 128), F32), pltpu.VMEM((n, M_INNER), F32), pltpu.VMEM((n, M_INNER), F32),
                        pltpu.VMEM((n, M_INNER), F32)],
        compiler_params=_params(("arbitrary",)),
        name="odd_sample",
    )(x, g_all, ml['win'], ml['convw'], ml['convb'], conv0.reshape(n, cw),
      ml['wq'], ml['wk'], ml['wv'], ml['wgate'], ml['bgate'], ml['normg'], ml['skip'], ml['wdown'],
      m_pad, c0, n0)
    y, c, nn, m_new, conv_new = outs
    return y, c, nn, m_new[:, :M_HEADS], conv_new.reshape(n, M_CONV - 1, M_INNER)


def _odd_sample_proj_kernel(x_ref, g_ref, win_ref, convw_ref, convb_ref, cv_ref, wq_ref, wk_ref, wv_ref,
                            wgate_ref, bgate_ref, m_in_ref, n_in_ref,
                            qt_ref, kwt_ref, v_ref, a_ref, den_ref, n_out_ref, m_out_ref, cv_out_ref,
                            xc_ref, z_ref, *, gi):
    hn = _rms(x_ref[...], g_ref[gi:gi + 1, :]).astype(BF16)
    p = _mm(hn, win_ref[...])
    xm = p[:, :M_INNER]
    z_ref[...] = p[:, M_INNER:]
    cv0 = cv_ref[:, :M_INNER]
    cv1 = cv_ref[:, M_INNER:2 * M_INNER]
    cv2 = cv_ref[:, 2 * M_INNER:]
    cv_out_ref[:, :M_INNER] = cv1
    cv_out_ref[:, M_INNER:2 * M_INNER] = cv2
    cv_out_ref[:, 2 * M_INNER:] = xm
    conv = (convb_ref[...] + convw_ref[3:4, :] * xm + convw_ref[2:3, :] * cv2
            + convw_ref[1:2, :] * cv1 + convw_ref[0:1, :] * cv0)
    xc = _silu(conv)
    xc_ref[...] = xc
    xcb = xc.astype(BF16)
    q = _headwise(xcb, wq_ref)
    k = _headwise(xcb, wk_ref)
    v = _headwise(xm.astype(BF16), wv_ref)
    v_ref[...] = v
    gate = (_mm(q.astype(BF16), wgate_ref[0]) + _mm(k.astype(BF16), wgate_ref[1])
            + _mm(v.astype(BF16), wgate_ref[2])) + bgate_ref[...]
    m_in = m_in_ref[...]
    lane = lax.broadcasted_iota(jnp.int32, m_in.shape, 1)
    a_all = jnp.zeros_like(m_in)
    den_all = jnp.zeros_like(m_in)
    m_all = jnp.zeros_like(m_in)
    for h in range(M_HEADS):
        sl = slice(h * M_DH, (h + 1) * M_DH)
        ig = gate[:, h:h + 1]
        lf = _log_sigmoid(gate[:, M_HEADS + h:M_HEADS + h + 1])
        m_prev = m_in[:, h:h + 1]
        m_new = jnp.maximum(lf + m_prev, ig)
        a = jnp.exp(lf + m_prev - m_new)
        kw = (jnp.exp(ig - m_new) * (M_DH ** -0.5)) * k[:, sl]
        n_new = a * n_in_ref[:, sl] + kw
        n_out_ref[:, sl] = n_new
        qh = q[:, sl]
        den = jnp.maximum(jnp.abs(jnp.sum(qh * n_new, axis=-1, keepdims=True)), jnp.exp(-m_new))
        a_all = jnp.where(lane == h, a, a_all)
        den_all = jnp.where(lane == h, den, den_all)
        m_all = jnp.where(lane == h, m_new, m_all)
        for c in range(M_DH // 128):
            qt_ref[h, c * 128:(c + 1) * 128, :] = qh[:, c * 128:(c + 1) * 128].T
            kwt_ref[h, c * 128:(c + 1) * 128, :] = kw[:, c * 128:(c + 1) * 128].T
    a_ref[...] = a_all
    den_ref[...] = den_all
    m_out_ref[...] = m_all


def _odd_sample_proj(x, g_all, gi, ml, n0, m0, conv0):
    n = x.shape[0]
    assert n == 128, "the per-head transposes assume one 128-lane tile of sequences"
    m_pad = jnp.pad(m0, ((0, 0), (0, 128 - M_HEADS)))
    cw = (M_CONV - 1) * M_INNER
    f32 = lambda *shape: jax.ShapeDtypeStruct(shape, F32)
    shapes = [f32(M_HEADS, M_DH, n), f32(M_HEADS, M_DH, n), f32(n, M_INNER), f32(n, 128), f32(n, 128),
              f32(n, M_INNER), f32(n, 128), f32(n, cw), f32(n, M_INNER), f32(n, M_INNER)]
    args = (x, g_all, ml['win'], ml['convw'], ml['convb'], conv0.reshape(n, cw), ml['wq'], ml['wk'], ml['wv'],
            ml['wgate'], ml['bgate'], m_pad, n0.reshape(n, M_INNER))
    outs = pl.pallas_call(
        functools.partial(_odd_sample_proj_kernel, gi=gi),
        grid=(1,),
        in_specs=[_const_spec(a.shape) for a in args],
        out_specs=[pl.BlockSpec(s.shape, lambda i, nd=len(s.shape): (0,) * nd) for s in shapes],
        out_shape=shapes,
        compiler_params=_params(("arbitrary",)),
        name="odd_sample_proj",
    )(*args)
    qt, kwt, v, a, den, n_new, m_new, conv_new, xc, z = outs
    return (qt, kwt, v, a, den, n_new.reshape(n, M_HEADS, M_DH), m_new[:, :M_HEADS],
            conv_new.reshape(n, M_CONV - 1, M_INNER), xc, z)


STREAM_IN_BUFS = 6
STREAM_OUT_BUFS = 6
STREAM_ROWS = 32


def _ffn_stream_kernel(xp_ref, g_ref, gfin_ref, wg_ref, wu_ref, wd_ref, qt_ref, kwt_ref, v_ref, a_ref, c_in,
                       op_ref, c_out, hnum_ref, cin_buf, cout_buf, sem_in, sem_out, *, gi, per_step):
    i = pl.program_id(0)
    n_pieces = per_step * M_HEADS

    def seq(p):
        return i * per_step + p // M_HEADS

    def in_copy(p, ahead=0):
        slot = p % STREAM_IN_BUFS
        s = seq(p) if not ahead else jnp.minimum(seq(p) + per_step, pl.num_programs(0) * per_step - 1)
        return pltpu.make_async_copy(c_in.at[s, p % M_HEADS], cin_buf.at[slot], sem_in.at[slot])

    def out_copy(p):
        slot = p % STREAM_OUT_BUFS
        return pltpu.make_async_copy(cout_buf.at[slot], c_out.at[seq(p), p % M_HEADS], sem_out.at[slot])

    @pl.when(i == 0)
    def _():
        hnum_ref[...] = jnp.zeros_like(hnum_ref)
        for p in range(STREAM_IN_BUFS):
            in_copy(p).start()

    shift = (128 - (i * per_step) % 128) % 128
    rolled = {}

    def columns(ref, h):
        if (id(ref), h) not in rolled:
            rolled[(id(ref), h)] = pltpu.roll(ref[h], shift, 1)
        return rolled[(id(ref), h)]

    rows = {}

    def seq_rows(j):
        if j not in rows:
            s = i * per_step + j
            rows[j] = (_pick_row(a_ref, s), _pick_row(v_ref, s))
        return rows[j]

    h_parts = {}

    def update(p):
        j, h = divmod(p, M_HEADS)
        a_row, v_row = seq_rows(j)
        kw_col = columns(kwt_ref, h)[:, j:j + 1]
        q_col = columns(qt_ref, h)[:, j:j + 1]
        a = a_row[:, h:h + 1]
        v_h = v_row[:, h * M_DH:(h + 1) * M_DH]
        acc = jnp.zeros((STREAM_ROWS, M_DH), F32)
        for r0 in range(0, M_DH, STREAM_ROWS):
            rs = slice(r0, r0 + STREAM_ROWS)
            c_new = a * cin_buf[p % STREAM_IN_BUFS, rs, :] + kw_col[rs] * v_h
            cout_buf[p % STREAM_OUT_BUFS, rs, :] = c_new
            acc = acc + q_col[rs] * c_new
        h_parts[(j, h)] = jnp.sum(acc, axis=0, keepdims=True)
        if h == M_HEADS - 1:
            _put_row(hnum_ref, i * per_step + j, jnp.concatenate([h_parts[(j, hh)] for hh in range(M_HEADS)], axis=1))

    def region(first, matmul):
        ps = (first, first + 1)
        for p in ps:
            in_copy(p).wait()
            if p >= STREAM_OUT_BUFS:
                out_copy(p - STREAM_OUT_BUFS).wait()
            else:
                pl.when(i > 0)(out_copy(p).wait)
        out = matmul()
        for p in ps:
            update(p)
        for p in ps:
            out_copy(p).start()
            if p + STREAM_IN_BUFS < n_pieces:
                in_copy(p + STREAM_IN_BUFS).start()
            else:
                in_copy(p % STREAM_IN_BUFS, ahead=1).start()
        return out

    x = xp_ref[...]
    hb = _rms(x, g_ref[gi:gi + 1, :]).astype(BF16)
    y = None
    first = 0
    for lo, hi in zip(FFN_BOUNDS[:-1], FFN_BOUNDS[1:]):
        half = lo + (hi - lo + MXU_TILE) // (2 * MXU_TILE) * MXU_TILE
        parts = []
        for w_ref in (wg_ref, wu_ref):
            for c0, c1 in ((lo, half), (half, hi)):
                parts.append(region(first, lambda w_ref=w_ref, c0=c0, c1=c1: _mm(hb, w_ref[:, c0:c1])))
                first += 2
        gt = jnp.concatenate(parts[:2], axis=1)
        ut = jnp.concatenate(parts[2:], axis=1)
        d = _mm((_silu(gt) * ut).astype(BF16), wd_ref[lo:hi, :])
        y = d if y is None else y + d
    assert first == n_pieces
    op_ref[...] = _rms(x + 0.5 * y, gfin_ref[...])
    @pl.when(i == pl.num_programs(0) - 1)
    def _():
        for p in range(n_pieces - STREAM_OUT_BUFS, n_pieces):
            out_copy(p).wait()
        for p in range(STREAM_IN_BUFS):
            in_copy(p, ahead=1).wait()


def _ffn_stream(xp, g_all, gfin, wg, wu, wd, gi, tm, qt, kwt, v, a, c0):
    n = xp.shape[0]
    ns = v.shape[0]
    steps = n // tm
    per_step = ns // steps
    assert per_step * steps == ns and 128 % per_step == 0
    any_spec = pl.BlockSpec(memory_space=pl.ANY)
    return pl.pallas_call(
        functools.partial(_ffn_stream_kernel, gi=gi, per_step=per_step),
        grid=(steps,),
        in_specs=[pl.BlockSpec((tm, D_MODEL), lambda i: (i, 0)),
                  _const_spec(g_all.shape),
                  _const_spec((1, D_MODEL)),
                  _const_spec((D_MODEL, D_FF)), _const_spec((D_MODEL, D_FF)), _const_spec((D_FF, D_MODEL)),
                  _const_spec(qt.shape), _const_spec(kwt.shape), _const_spec(v.shape), _const_spec(a.shape),
                  any_spec],
        out_specs=[pl.BlockSpec((tm, D_MODEL), lambda i: (i, 0)),
                   any_spec,
                   pl.BlockSpec((ns, M_INNER), lambda i: (0, 0))],
        out_shape=[jax.ShapeDtypeStruct((n, D_MODEL), F32),
                   jax.ShapeDtypeStruct(c0.shape, F32),
                   jax.ShapeDtypeStruct((ns, M_INNER), F32)],
        scratch_shapes=[pltpu.VMEM((STREAM_IN_BUFS, M_DH, M_DH), F32), pltpu.VMEM((STREAM_OUT_BUFS, M_DH, M_DH), F32),
                        pltpu.SemaphoreType.DMA((STREAM_IN_BUFS,)), pltpu.SemaphoreType.DMA((STREAM_OUT_BUFS,))],
        compiler_params=_params(("arbitrary",)),
        name="ffn_final_stream",
    )(xp, g_all, gfin, wg, wu, wd, qt, kwt, v, a, c0)


def _odd_sample_out_kernel(x_ref, hnum_ref, den_ref, xc_ref, z_ref, normg_ref, skip_ref, wdown_ref,
                           g_ref, gfin_ref, wg_ref, wu_ref, wd_ref, y_ref, *, gi):
    hnum = hnum_ref[...]
    den = den_ref[...]
    hs = [hnum[:, h * M_DH:(h + 1) * M_DH] * (1.0 / den[:, h:h + 1]) for h in range(M_HEADS)]
    xs = x_ref[...] + _mlstm_out(hs, xc_ref[...], z_ref[...], normg_ref, skip_ref, wdown_ref)
    y_ref[...] = _ffn_rows(xs, g_ref[gi:gi + 1, :], wg_ref, wu_ref, wd_ref, gfin_ref[...])


def _odd_sample_out(x, hnum, den, xc, z, ml, g_all, gfin, wg, wu, wd, gi):
    args = (x, hnum, den, xc, z, ml['normg'], ml['skip'], ml['wdown'], g_all, gfin, wg, wu, wd)
    return pl.pallas_call(
        functools.partial(_odd_sample_out_kernel, gi=gi),
        grid=(1,),
        in_specs=[_const_spec(a.shape) for a in args],
        out_specs=pl.BlockSpec(x.shape, lambda i: (0, 0)),
        out_shape=jax.ShapeDtypeStruct(x.shape, F32),
        compiler_params=_params(("arbitrary",)),
        name="odd_sample_out",
    )(*args)


TM_FFN = 512
TT_EVEN = 256
TT_ODD = 512


def kernel(x_prompt, x_sample, state_ret, state_hgrn, state_mlstm_C, state_mlstm_n, state_mlstm_m, state_mlstm_conv,
           norm_g, final_norm_g, ffn_w_gate, ffn_w_up, ffn_w_down, ev_w_in, ev_w_out, ret_norm_g, hg_norm_g,
           hg_lb_logits, ml_w_in, ml_conv_w, ml_conv_b, ml_w_q, ml_w_k, ml_w_v, ml_w_ig, ml_b_ig, ml_w_fg,
           ml_b_fg, ml_norm_g, ml_skip, ml_w_down):
    bp, tp, _ = x_prompt.shape
    ns = x_sample.shape[0]

    g_all = norm_g.reshape(-1, D_MODEL)
    gfin = final_norm_g.reshape(1, D_MODEL)
    lb_all = jnp.cumsum(jax.nn.softmax(hg_lb_logits.astype(F32), axis=0), axis=0)
    lb = lb_all[0].reshape(1, G_HEADS * G_EXP)
    retg = ret_norm_g[0]
    hgg = hg_norm_g[0]
    wgate = _gate_weights(ml_w_ig[0], ml_w_fg[0])
    bgate = jnp.pad(jnp.concatenate([ml_b_ig[0], ml_b_fg[0]]), (0, 128 - 2 * M_HEADS)).reshape(1, 128)
    ml = {
        'convw': ml_conv_w[0],
        'convb': ml_conv_b[0].reshape(1, M_INNER),
        'wq': _block_diag(ml_w_q[0]).astype(BF16),
        'wk': _block_diag(ml_w_k[0]).astype(BF16),
        'wv': _block_diag(ml_w_v[0]).astype(BF16),
        'wgate': wgate.astype(BF16),
        'wgate_t': jnp.swapaxes(wgate[:, :, :SUBLANES], 1, 2).astype(BF16),
        'bgate': bgate,
        'bgate_col': bgate[0, :SUBLANES].reshape(SUBLANES, 1),
        'normg': ml_norm_g[0].reshape(1, M_INNER),
        'skip': ml_skip[0].reshape(1, M_INNER),
    }

    xp = x_prompt.reshape(bp * tp, D_MODEL)
    xs = x_sample.reshape(ns, D_MODEL)
    ffn_w = (ffn_w_gate, ffn_w_up, ffn_w_down)
    next_ffn = lambda layer, idx: [(w, (layer, idx)) for w in ffn_w]

    w00 = [w[0, 0].astype(BF16) for w in ffn_w]
    xp, xs, cast = _ffn(xp, xs, g_all, gfin, *w00, 0, TM_FFN,
                        casts=next_ffn(0, 1) + [(ev_w_in, (0,)), (ev_w_out, (0,))])
    w01, (ev_in, ev_out) = cast[:3], cast[3:]
    xp, ret_p, hg_p = _even_prompt(xp.reshape(bp, tp, D_MODEL), g_all, 1, ev_in, ev_out, retg, hgg, lb, TT_EVEN)
    xs, ret_s, hg_s = _even_sample(xs, g_all, 1, ev_in, ev_out, retg, hgg, lb, state_ret[:, 0], state_hgrn[:, 0])
    xp, xs, cast = _ffn(xp.reshape(bp * tp, D_MODEL), xs, g_all, gfin, *w01, 2, TM_FFN,
                        casts=next_ffn(1, 0) + [(ml_w_in, (0,)), (ml_w_down, (0,))])
    w10, (ml['win'], ml['wdown']) = cast[:3], cast[3:]
    xp, xs, w11 = _ffn(xp, xs, g_all, gfin, *w10, 3, TM_FFN, casts=next_ffn(1, 1))
    xp, c_p, n_p, m_p, conv_p = _odd_prompt(xp.reshape(bp, tp, D_MODEL), g_all, 4, ml, TT_ODD)
    qt, kwt, v_s, a_s, den_s, n_s, m_s, conv_s, xc_s, z_s = _odd_sample_proj(
        xs, g_all, 4, ml, state_mlstm_n[:, 0], state_mlstm_m[:, 0], state_mlstm_conv[:, 0])
    y_p, c_s, hnum = _ffn_stream(xp.reshape(bp * tp, D_MODEL), g_all, gfin, *w11, 5, TM_FFN,
                                 qt, kwt, v_s, a_s, state_mlstm_C[:, 0])
    y_s = _odd_sample_out(xs, hnum, den_s, xc_s, z_s, ml, g_all, gfin, *w11, 5)

    return (y_p.reshape(bp, tp, D_MODEL), y_s.reshape(ns, 1, D_MODEL),
            ret_p[:, None], hg_p[:, None], c_p[:, None], n_p[:, None], m_p[:, None, :M_HEADS, 0], conv_p[:, None],
            ret_s[:, None], hg_s[:, None], c_s[:, None], n_s[:, None], m_s[:, None], conv_s[:, None])
```

```python
import functools
import math

import jax
import jax.numpy as jnp
from jax import lax
from jax.experimental import pallas as pl
from jax.experimental.pallas import tpu as pltpu

D_MODEL = 1024
PAST_LEN = 16384
R_HEADS = 4
R_DK = 128
R_DV = 128
G_HEADS = 4
G_EXP = 128
G_DV = 128
M_INNER = 2 * D_MODEL
M_HEADS = 4
M_DH = M_INNER // M_HEADS
M_CONV = 4
QKV_BLOCK = 4
D_FF = 2816
EPS = 1e-6
ROPE_BASE = 10000.0
EVEN_IN = 4096
EVEN_OUT = 1024

F32 = jnp.float32
BF16 = jnp.bfloat16

VMEM_LIMIT_BYTES = 56 * 1024 * 1024

HG_CHUNK = 64
HG_SUB = 16
HG_SAFE_LOG_DECAY = -60.0
MXU_TILE = 256
FFN_BOUNDS = (0, 6 * MXU_TILE, D_FF)
BD = MXU_TILE
SUBLANES = 8


def _nt(a, b):
    return lax.dot_general(a, b, (((1,), (1,)), ((), ())), preferred_element_type=F32)


def _tn(a, b):
    return lax.dot_general(a, b, (((0,), (0,)), ((), ())), preferred_element_type=F32)


def _mm(a, b):
    return jnp.dot(a, b, preferred_element_type=F32)


def _sigmoid(x):
    return 1.0 / (1.0 + jnp.exp(-x))


def _silu(x):
    return x * _sigmoid(x)


def _log_sigmoid(x):
    return jnp.minimum(x, 0.0) - jnp.log(1.0 + jnp.exp(-jnp.abs(x)))


def _rms(x, g):
    return x * lax.rsqrt(jnp.mean(x * x, axis=-1, keepdims=True) + EPS) * g


def _head_norm(x, g, center):
    if center:
        x = x - jnp.mean(x, axis=-1, keepdims=True)
    return x * lax.rsqrt(jnp.mean(x * x, axis=-1, keepdims=True) + EPS) * g


def _rotary(x, cos, sin_signed):
    return x * cos + pltpu.roll(x, 64, 1) * sin_signed


def _const_spec(shape):
    n = len(shape)
    return pl.BlockSpec(shape, lambda *_: (0,) * n, pipeline_mode=pl.Buffered(1))


def _params(sem):
    return pltpu.CompilerParams(dimension_semantics=sem, vmem_limit_bytes=VMEM_LIMIT_BYTES)


def _ffn_rows(x, g, wg_ref, wu_ref, wd_ref, gfin):
    h = _rms(x, g).astype(BF16)
    y = jnp.zeros_like(x)
    for lo, hi in zip(FFN_BOUNDS[:-1], FFN_BOUNDS[1:]):
        gt = _mm(h, wg_ref[:, lo:hi])
        ut = _mm(h, wu_ref[:, lo:hi])
        a = (_silu(gt) * ut).astype(BF16)
        y = y + _mm(a, wd_ref[lo:hi, :])
    out = x + 0.5 * y
    if gfin is not None:
        out = _rms(out, gfin)
    return out


def _ffn_kernel(xp_ref, xs_ref, g_ref, gfin_ref, wg_ref, wu_ref, wd_ref, *rest, gi, final, n_cast):
    cast_in = rest[:n_cast]
    op_ref, os_ref = rest[n_cast:n_cast + 2]
    cast_out = rest[n_cast + 2:]
    g = g_ref[gi:gi + 1, :]
    gfin = gfin_ref[...] if final else None
    op_ref[...] = _ffn_rows(xp_ref[...], g, wg_ref, wu_ref, wd_ref, gfin)
    for src, dst in zip(cast_in, cast_out):
        dst[...] = src[...].astype(BF16)

    @pl.when(pl.program_id(0) == pl.num_programs(0) - 1)
    def _():
        os_ref[...] = _ffn_rows(xs_ref[...], g, wg_ref, wu_ref, wd_ref, gfin)


BF16_ROWS = 16


def _cast_specs(arr, lead, steps):
    rows, cols = arr.shape[-2:]
    per = 1 if (rows // steps) % BF16_ROWS == 0 else 2
    br = rows * per // steps
    in_spec = pl.BlockSpec((None,) * len(lead) + (br, cols), lambda i: tuple(lead) + (i // per, 0))
    out_spec = pl.BlockSpec((br, cols), lambda i: (i // per, 0))
    return in_spec, out_spec, jax.ShapeDtypeStruct((rows, cols), BF16)


def _ffn(xp, xs, g_all, gfin, wg, wu, wd, gi, tm, final=False, casts=()):
    n = xp.shape[0]
    ns = xs.shape[0]
    steps = n // tm
    cast_specs = [_cast_specs(arr, lead, steps) for arr, lead in casts]
    outs = pl.pallas_call(
        functools.partial(_ffn_kernel, gi=gi, final=final, n_cast=len(casts)),
        grid=(steps,),
        in_specs=[pl.BlockSpec((tm, D_MODEL), lambda i: (i, 0)),
                  _const_spec((ns, D_MODEL)),
                  _const_spec(g_all.shape),
                  _const_spec((1, D_MODEL)),
                  _const_spec((D_MODEL, D_FF)), _const_spec((D_MODEL, D_FF)), _const_spec((D_FF, D_MODEL))]
                 + [c[0] for c in cast_specs],
        out_specs=[pl.BlockSpec((tm, D_MODEL), lambda i: (i, 0)),
                   pl.BlockSpec((ns, D_MODEL), lambda i: (0, 0))] + [c[1] for c in cast_specs],
        out_shape=[jax.ShapeDtypeStruct((n, D_MODEL), F32), jax.ShapeDtypeStruct((ns, D_MODEL), F32)]
                  + [c[2] for c in cast_specs],
        compiler_params=_params(("arbitrary",)),
        name="ffn_final" if final else "ffn",
    )(xp, xs, g_all, gfin, wg, wu, wd, *[arr for arr, _ in casts])
    return outs[0], outs[1], outs[2:]


def _ret_log_gamma(h):
    return math.log(1.0 - 2.0 ** (-5.0 - h))


def _head_parts(p, mixer, h):
    base = mixer * 4 * R_HEADS * R_DK
    return [p[:, base + (j * R_HEADS + h) * R_DK:base + (j * R_HEADS + h + 1) * R_DK] for j in range(4)]


def _hgrn_gates(gq, gf, lb):
    f = lb + (1.0 - lb) * _sigmoid(gf)
    kk = (1.0 - lb) * _sigmoid(-gf)
    qq = _silu(gq)
    return qq, kk, f


def _split3(x):
    hi = x.astype(BF16)
    r1 = x - hi.astype(F32)
    mid = r1.astype(BF16)
    lo = (r1 - mid.astype(F32)).astype(BF16)
    return hi, mid, lo


def _shift_rows(bases, d):
    base = bases[d % SUBLANES]
    full = (d // SUBLANES) * SUBLANES
    return pltpu.roll(base, full, 0) if full else base


def _hgrn_tile_factorised(qq, kk, vv, b, st):
    tt = qq[0].shape[0]
    heads = range(len(qq))
    qx = [(qq[h] * jnp.exp(b[h])).astype(BF16) for h in heads]
    kx = [kk[h] * jnp.exp(-b[h]) for h in heads]
    kxb = [kx[h].astype(BF16) for h in heads]
    vb = [vv[h].astype(BF16) for h in heads]
    ti = lax.broadcasted_iota(jnp.int32, (HG_CHUNK, HG_CHUNK), 0)
    si = lax.broadcasted_iota(jnp.int32, (HG_CHUNK, HG_CHUNK), 1)
    causal = si <= ti
    st = list(st)
    o_chunks = [[] for _ in heads]
    for c in range(tt // HG_CHUNK):
        rs = slice(c * HG_CHUNK, (c + 1) * HG_CHUNK)
        for h in heads:
            a = jnp.where(causal, _nt(qx[h][rs], kxb[h][rs]), 0.0)
            o_chunks[h].append(_mm(a.astype(BF16), vb[h][rs]) + _nt(qx[h][rs], st[h].astype(BF16)))
            etot = jnp.exp(b[h][(c + 1) * HG_CHUNK - 1:(c + 1) * HG_CHUNK])
            st[h] = st[h] * etot + _tn(vb[h][rs], (kx[h][rs] * etot).astype(BF16))
    return [jnp.concatenate(o_chunks[h], axis=0) for h in heads], st


def _hgrn_tile_guarded(qq_all, kk_all, f_all, vv_all, b_all, st_all):
    tt = qq_all.shape[0]
    hd = G_EXP
    row = lax.broadcasted_iota(jnp.int32, (tt, hd), 0)
    rsub = row % HG_SUB
    sub = (lax.broadcasted_iota(jnp.int32, (HG_CHUNK, hd), 0)) // HG_SUB
    n_sub = HG_CHUNK // HG_SUB
    outs, states = [], []
    for h in range(G_HEADS):
        hs = slice(h * hd, (h + 1) * hd)
        qq = qq_all[:, hs]
        kk = kk_all[:, hs]
        ff = f_all[:, hs]
        bc_all = b_all[:, hs]
        vv = vv_all[:, hs]
        f_sh = [ff] + [pltpu.roll(ff, r, 0) for r in range(1, SUBLANES)]
        k_sh = [kk] + [pltpu.roll(kk, r, 0) for r in range(1, SUBLANES)]
        v_sh = [vv] + [pltpu.roll(vv, r, 0) for r in range(1, SUBLANES)]
        o_band = jnp.sum(qq * kk, axis=-1, keepdims=True) * vv
        decay = None
        for d in range(1, HG_SUB):
            fd = _shift_rows(f_sh, d - 1)
            decay = jnp.where(rsub >= d, fd if decay is None else decay * fd, 0.0)
            term = qq * _shift_rows(k_sh, d) * decay
            o_band = o_band + jnp.sum(term, axis=-1, keepdims=True) * _shift_rows(v_sh, d)
        st = st_all[h]
        o_chunks = []
        for c in range(tt // HG_CHUNK):
            r0 = c * HG_CHUNK
            bc = bc_all[r0:r0 + HG_CHUNK]
            qc = qq[r0:r0 + HG_CHUNK]
            kc = kk[r0:r0 + HG_CHUNK]
            vcb = vv[r0:r0 + HG_CHUNK].astype(BF16)
            refs = [bc[i * HG_SUB - 1:i * HG_SUB] for i in range(1, n_sub)]
            refrow = refs[-1]
            for i in range(n_sub - 2, 0, -1):
                refrow = jnp.where(sub == i, refs[i - 1], refrow)
            qp = qc * jnp.exp(bc - refrow)
            lhs = jnp.concatenate([jnp.where(sub == i, qp, 0.0) for i in range(1, n_sub)], axis=1)
            kcat = jnp.concatenate([jnp.where(sub < i, kc * jnp.exp(refs[i - 1] - bc), 0.0)
                                    for i in range(1, n_sub)], axis=1)
            a = _nt(lhs.astype(BF16), kcat.astype(BF16))
            qb = (qc * jnp.exp(bc)).astype(BF16)
            o_chunks.append(_mm(a.astype(BF16), vcb) + _nt(qb, st.astype(BF16)))
            btot = bc[HG_CHUNK - 1:HG_CHUNK]
            ke = (kc * jnp.exp(btot - bc)).astype(BF16)
            st = st * jnp.exp(btot) + _tn(vcb, ke)
        outs.append(o_band + jnp.concatenate(o_chunks, axis=0))
        states.append(st)
    return jnp.concatenate(outs, axis=1), jnp.stack(states)


def _even_prompt_kernel(x_ref, g_ref, win_ref, wout_ref, cos_ref, sin_ref, retg_ref, hgg_ref, lb_ref,
                        y_ref, sret_ref, shg_ref, dmat_ref, tri_ref, st_ref, ohg_ref, *, tt, gi):
    b_id = pl.program_id(0)
    t_id = pl.program_id(1)
    n_t = pl.num_programs(1)
    hd = 128

    @pl.when(jnp.logical_and(b_id == 0, t_id == 0))
    def _():
        ti = lax.broadcasted_iota(jnp.int32, (tt, tt), 0)
        si = lax.broadcasted_iota(jnp.int32, (tt, tt), 1)
        diff = (ti - si).astype(F32)
        for h in range(R_HEADS):
            dmat_ref[h] = jnp.where(diff >= 0.0, jnp.exp(_ret_log_gamma(h) * jnp.maximum(diff, 0.0)), 0.0)
        same_chunk = (ti // HG_CHUNK) == (si // HG_CHUNK)
        tri_ref[...] = jnp.where(jnp.logical_and(same_chunk, si <= ti), 1.0, 0.0).astype(BF16)

    @pl.when(t_id == 0)
    def _():
        sret_ref[...] = jnp.zeros_like(sret_ref)
        st_ref[...] = jnp.zeros_like(st_ref)

    x = x_ref[0]
    hn = _rms(x, g_ref[gi:gi + 1, :]).astype(BF16)
    cos = cos_ref[...]
    sin = sin_ref[...]
    rowf = lax.broadcasted_iota(jnp.int32, (tt, hd), 0).astype(F32)
    outs = []
    hg = []
    logf_parts = []
    st_old = st_ref[...]

    p = _mm(hn, win_ref[...])
    for i in range(R_HEADS + G_HEADS):
        h = i // 2
        pa, pb, pc, pd = _head_parts(p, i % 2, h)
        if i % 2 == 0:
            lg = _ret_log_gamma(h)
            q = _rotary(pa, cos, sin)
            k = _rotary(pb, cos, sin) * (R_DK ** -0.5)
            v = pc
            rg = pd
            s0 = sret_ref[0, h]
            qb = q.astype(BF16)
            vb = v.astype(BF16)
            inter = _mm(qb, s0.astype(BF16)) * jnp.exp(lg * (rowf + 1.0))
            scores = _nt(qb, k.astype(BF16)) * dmat_ref[h]
            intra = _mm(scores.astype(BF16), vb)
            kd = (k * jnp.exp(lg * (tt - 1.0 - rowf))).astype(BF16)
            sret_ref[0, h] = math.exp(lg * tt) * s0 + _tn(kd, vb)
            outs.append(_head_norm(inter + intra, retg_ref[h:h + 1, :], False) * _silu(rg))
        else:
            qq, kk, ff = _hgrn_gates(pa, pb, lb_ref[:, h * hd:(h + 1) * hd])
            hg.append((qq, kk, ff, pc, _sigmoid(pd)))
            logf_parts.extend(_split3(jnp.log(ff)))

    cs = _mm(tri_ref[...], jnp.concatenate(logf_parts, axis=1))
    b = [cs[:, (3 * h) * hd:(3 * h + 1) * hd] + cs[:, (3 * h + 1) * hd:(3 * h + 2) * hd]
         + cs[:, (3 * h + 2) * hd:(3 * h + 3) * hd] for h in range(G_HEADS)]
    b_all = jnp.concatenate(b, axis=1)
    o_fast, st_fast = _hgrn_tile_factorised([t[0] for t in hg], [t[1] for t in hg], [t[3] for t in hg], b,
                                            [st_old[h] for h in range(G_HEADS)])
    for h in range(G_HEADS):
        ohg_ref[:, h * hd:(h + 1) * hd] = o_fast[h]
        st_ref[h] = st_fast[h]

    @pl.when(jnp.min(b_all) < HG_SAFE_LOG_DECAY)
    def _():
        cat = lambda j: jnp.concatenate([t[j] for t in hg], axis=1)
        o_safe, st_safe = _hgrn_tile_guarded(cat(0), cat(1), cat(2), cat(3), b_all, st_old)
        ohg_ref[...] = o_safe
        st_ref[...] = st_safe

    for h in range(G_HEADS):
        outs.append(_head_norm(ohg_ref[:, h * hd:(h + 1) * hd], hgg_ref[h:h + 1, :], False) * hg[h][4])


    ycat = jnp.concatenate(outs, axis=1).astype(BF16)
    y_ref[0] = x + _mm(ycat, wout_ref[...])

    @pl.when(t_id == n_t - 1)
    def _():
        for h in range(G_HEADS):
            shg_ref[0, h] = st_ref[h].T


def _rope_tables(pos):
    half = R_DK // 2
    inv = ROPE_BASE ** (-jnp.arange(half, dtype=F32) / half)
    ang = pos.astype(F32)[:, None] * inv[None, :]
    cos = jnp.cos(ang)
    sin = jnp.sin(ang)
    return jnp.concatenate([cos, cos], axis=-1), jnp.concatenate([-sin, sin], axis=-1)


def _even_prompt(x, g_all, gi, win, wout, retg, hgg, lb, tt):
    bsz, seq, _ = x.shape
    cos, sin = _rope_tables(jnp.arange(seq, dtype=jnp.int32))
    state_spec = pl.BlockSpec((1, 4, 128, 128), lambda b, t: (b, 0, 0, 0))
    return pl.pallas_call(
        functools.partial(_even_prompt_kernel, tt=tt, gi=gi),
        grid=(bsz, seq // tt),
        in_specs=[pl.BlockSpec((1, tt, D_MODEL), lambda b, t: (b, t, 0)),
                  _const_spec(g_all.shape),
                  _const_spec((D_MODEL, EVEN_IN)),
                  _const_spec((EVEN_OUT, D_MODEL)),
                  pl.BlockSpec((tt, 128), lambda b, t: (t, 0)),
                  pl.BlockSpec((tt, 128), lambda b, t: (t, 0)),
                  _const_spec((4, 128)), _const_spec((4, 128)), _const_spec((1, 512))],
        out_specs=[pl.BlockSpec((1, tt, D_MODEL), lambda b, t: (b, t, 0)), state_spec, state_spec],
        out_shape=[jax.ShapeDtypeStruct(x.shape, F32),
                   jax.ShapeDtypeStruct((bsz, 4, 128, 128), F32),
                   jax.ShapeDtypeStruct((bsz, 4, 128, 128), F32)],
        scratch_shapes=[pltpu.VMEM((4, tt, tt), F32), pltpu.VMEM((tt, tt), BF16), pltpu.VMEM((4, 128, 128), F32),
                        pltpu.VMEM((tt, G_HEADS * G_DV), F32)],
        compiler_params=_params(("arbitrary", "arbitrary")),
        name="even_prompt",
    )(x, g_all, win, wout, cos, sin, retg, hgg, lb)


SB = 8


def _even_sample_kernel(x_ref, g_ref, win_ref, wout_ref, cos_ref, sin_ref, retg_ref, hgg_ref, lb_ref,
                        sret_in, shg_in, y_ref, sret_out, shg_out, p_ref, o_ref, *, gi):
    i = pl.program_id(0)
    n_i = pl.num_programs(0)
    hd = 128

    @pl.when(i == 0)
    def _():
        hn = _rms(x_ref[...], g_ref[gi:gi + 1, :]).astype(BF16)
        p_ref[...] = _mm(hn, win_ref[...])

    r0 = pl.multiple_of(i * SB, SB)
    p = p_ref[pl.ds(r0, SB), :]
    cos = cos_ref[...]
    sin = sin_ref[...]
    row = lax.broadcasted_iota(jnp.int32, (SB, hd), 0)
    outs = []
    for h in range(R_HEADS):
        gamma = math.exp(_ret_log_gamma(h))
        pa, pb, pc, rg = _head_parts(p, 0, h)
        q = _rotary(pa, cos, sin).astype(BF16)
        k = _rotary(pb, cos, sin) * (R_DK ** -0.5)
        vb = pc.astype(BF16)
        o = jnp.zeros((SB, hd), F32)
        for j in range(SB):
            kj = jnp.where(row == j, k, 0.0).astype(BF16)
            s_new = gamma * sret_in[j, h] + _tn(kj, vb)
            sret_out[j, h] = s_new
            o = jnp.where(row == j, _mm(q, s_new.astype(BF16)), o)
        outs.append(_head_norm(o, retg_ref[h:h + 1, :], False) * _silu(rg))
    for h in range(G_HEADS):
        pa, pb, pc, gg = _head_parts(p, 1, h)
        qq, kk, ff = _hgrn_gates(pa, pb, lb_ref[:, h * hd:(h + 1) * hd])
        vb = pc.astype(BF16)
        f_cols = jnp.concatenate([ff, jnp.zeros((hd - SB, hd), F32)], axis=0).T
        qb = qq.astype(BF16)
        o = jnp.zeros((SB, hd), F32)
        for j in range(SB):
            kj = jnp.where(row == j, kk, 0.0).astype(BF16)
            s_new = f_cols[:, j:j + 1] * shg_in[j, h] + _tn(kj, vb)
            shg_out[j, h] = s_new
            o = jnp.where(row == j, _mm(qb, s_new.astype(BF16)), o)
        outs.append(_head_norm(o, hgg_ref[h:h + 1, :], False) * _sigmoid(gg))
    o_ref[pl.ds(r0, SB), :] = jnp.concatenate(outs, axis=1)

    @pl.when(i == n_i - 1)
    def _():
        y_ref[...] = x_ref[...] + _mm(o_ref[...].astype(BF16), wout_ref[...])


def _even_sample(x, g_all, gi, win, wout, retg, hgg, lb, sret, shg):
    n = x.shape[0]
    cos, sin = _rope_tables(jnp.full((1,), PAST_LEN, dtype=jnp.int32))
    state_spec = pl.BlockSpec((SB, 4, 128, 128), lambda i: (i, 0, 0, 0))
    return pl.pallas_call(
        functools.partial(_even_sample_kernel, gi=gi),
        grid=(n // SB,),
        in_specs=[_const_spec((n, D_MODEL)),
                  _const_spec(g_all.shape),
                  _const_spec((D_MODEL, EVEN_IN)),
                  _const_spec((EVEN_OUT, D_MODEL)),
                  _const_spec((1, 128)), _const_spec((1, 128)),
                  _const_spec((4, 128)), _const_spec((4, 128)), _const_spec((1, 512)),
                  state_spec, state_spec],
        out_specs=[pl.BlockSpec((n, D_MODEL), lambda i: (0, 0)), state_spec, state_spec],
        out_shape=[jax.ShapeDtypeStruct((n, D_MODEL), F32),
                   jax.ShapeDtypeStruct(sret.shape, F32),
                   jax.ShapeDtypeStruct(shg.shape, F32)],
        scratch_shapes=[pltpu.VMEM((n, EVEN_IN), F32), pltpu.VMEM((n, EVEN_OUT), F32)],
        compiler_params=_params(("arbitrary",)),
        name="even_sample",
    )(x, g_all, win, wout, cos, sin, retg, hgg, lb, sret, shg)


def _block_diag(w):
    wr = w.reshape(M_INNER // BD, BD, QKV_BLOCK)
    tiled = jnp.tile(wr, (1, 1, BD // QKV_BLOCK))
    rb = lax.broadcasted_iota(jnp.int32, (BD, BD), 0) // QKV_BLOCK
    cb = lax.broadcasted_iota(jnp.int32, (BD, BD), 1) // QKV_BLOCK
    return jnp.where((rb == cb)[None], tiled, 0.0)


def _headwise(xb, w_ref, g0=0):
    return jnp.concatenate([_mm(xb[:, g * BD:(g + 1) * BD], w_ref[g0 + g]) for g in range(xb.shape[1] // BD)], axis=1)


def _gate_weights(w_ig, w_fg):
    w = jnp.concatenate([w_ig, w_fg], axis=1)
    w = jnp.pad(w, ((0, 0), (0, 128 - 2 * M_HEADS)))
    return w.reshape(3, M_INNER, 128)


def _mlstm_out(hs, xc, z, normg_ref, skip_ref, wdown_ref):
    hc = jnp.concatenate([_head_norm(hs[h], normg_ref[:, h * M_DH:(h + 1) * M_DH], True) for h in range(M_HEADS)],
                         axis=1)
    hc = hc + skip_ref[...] * xc
    return _mm((hc * _silu(z)).astype(BF16), wdown_ref[...])


def _odd_prompt_kernel(x_ref, g_ref, win_ref, convw_ref, convb_ref, wq_ref, wk_ref, wv_ref,
                       wgate_t_ref, bgate_t_ref, normg_ref, skip_ref, wdown_ref,
                       y_ref, c_ref, n_ref, m_out_ref, conv_out_ref,
                       carry_ref, m_ref, xc_ref, q_ref, k_ref, v_ref, *, tt, gi):
    t_id = pl.program_id(1)
    k_scale = M_DH ** -0.5

    @pl.when(t_id == 0)
    def _():
        c_ref[...] = jnp.zeros_like(c_ref)
        n_ref[...] = jnp.zeros_like(n_ref)
        m_ref[...] = jnp.zeros_like(m_ref)
        carry_ref[...] = jnp.zeros_like(carry_ref)

    x = x_ref[0]
    hn = _rms(x, g_ref[gi:gi + 1, :]).astype(BF16)

    gates_t = bgate_t_ref[...]
    row8 = lax.broadcasted_iota(jnp.int32, (SUBLANES, M_DH), 0)
    tiles = M_DH // BD
    xm_next = _mm(hn, win_ref[:, :M_DH])
    for h in range(M_HEADS):
        sl = slice(h * M_DH, (h + 1) * M_DH)
        xm = xm_next
        if h + 1 < M_HEADS:
            xm_next = _mm(hn, win_ref[:, (h + 1) * M_DH:(h + 2) * M_DH])
        carry = carry_ref[:, sl]
        conv = convb_ref[:, sl] + convw_ref[M_CONV - 1:M_CONV, sl] * xm
        for j in range(1, M_CONV):
            rolled = pltpu.roll(xm, j, 0)
            head = jnp.where(row8 < j, pltpu.roll(carry, j, 0), rolled[:SUBLANES])
            shifted = jnp.concatenate([head, rolled[SUBLANES:]], axis=0)
            conv = conv + convw_ref[M_CONV - 1 - j:M_CONV - j, sl] * shifted
        carry_ref[:, sl] = xm[tt - SUBLANES:, :]
        conv_out_ref[0, :, sl] = xm[tt - (M_CONV - 1):, :]
        xc = _silu(conv)
        xc_ref[:, sl] = xc
        xcb = xc.astype(BF16)
        qb = _headwise(xcb, wq_ref, h * tiles).astype(BF16)
        kb = _headwise(xcb, wk_ref, h * tiles).astype(BF16)
        vb = _headwise(xm.astype(BF16), wv_ref, h * tiles).astype(BF16)
        q_ref[:, sl] = qb
        k_ref[:, sl] = kb
        v_ref[:, sl] = vb
        gates_t = gates_t + (_nt(wgate_t_ref[0, :, sl], qb) + _nt(wgate_t_ref[1, :, sl], kb)
                             + _nt(wgate_t_ref[2, :, sl], vb))

    lane8 = lax.broadcasted_iota(jnp.int32, (SUBLANES, tt), 1)
    row8t = lax.broadcasted_iota(jnp.int32, (SUBLANES, tt), 0)
    brow_all = _log_sigmoid(gates_t)
    d = 1
    while d < tt:
        brow_all = brow_all + jnp.where(lane8 >= d, pltpu.roll(brow_all, d, 1), 0.0)
        d *= 2
    rows = jnp.where(row8t < M_HEADS, gates_t, brow_all)
    pad = jnp.zeros((128 - SUBLANES, 128), F32)
    cols = jnp.concatenate([jnp.concatenate([rows[:, j * 128:(j + 1) * 128], pad], axis=0).T
                            for j in range(tt // 128)], axis=0)

    ti = lax.broadcasted_iota(jnp.int32, (tt, tt), 0)
    si = lax.broadcasted_iota(jnp.int32, (tt, tt), 1)
    causal = si <= ti
    y = x
    for h in range(M_HEADS):
        sl = slice(h * M_DH, (h + 1) * M_DH)
        qhb = q_ref[:, sl]
        khb = k_ref[:, sl]
        vhb = v_ref[:, sl]
        ig_col = cols[:, h:h + 1]
        b_col = cols[:, M_HEADS + h:M_HEADS + h + 1]
        ig_row = gates_t[h:h + 1, :]
        b_row = brow_all[M_HEADS + h:M_HEADS + h + 1, :]
        m_prev = m_ref[h:h + 1, 0:1]
        c_prev = c_ref[0, h]
        n_prev = n_ref[0, h:h + 1, :]

        dlog = jnp.where(causal, b_col + (ig_row - b_row), -jnp.inf)
        inter_log = b_col + m_prev
        m_row = jnp.maximum(inter_log, jnp.max(dlog, axis=-1, keepdims=True))
        w_inter = jnp.exp(inter_log - m_row)
        qk = _nt(qhb, khb) * (jnp.exp(dlog - m_row) * k_scale)
        num = w_inter * _mm(qhb, c_prev.astype(BF16)) + _mm(qk.astype(BF16), vhb)
        qn = _nt(qhb, jnp.broadcast_to(n_prev, (SUBLANES, M_DH)).astype(BF16))[:, 0:1]
        den = w_inter * qn + jnp.sum(qk, axis=-1, keepdims=True)
        den = jnp.maximum(jnp.abs(den), jnp.exp(-m_row))
        hh = num * (1.0 / den)

        b_end = b_col[tt - 1:tt, :]
        s_log = b_end - b_col + ig_col
        m_new = jnp.maximum(b_end + m_prev, jnp.max(s_log, axis=0, keepdims=True))
        a = jnp.exp(b_end + m_prev - m_new)
        kw = khb.astype(F32) * (jnp.exp(s_log - m_new) * k_scale)
        c_ref[0, h] = a * c_prev + _tn(kw.astype(BF16), vhb)
        n_ref[0, h:h + 1, :] = a * n_prev + jnp.sum(kw, axis=0, keepdims=True)
        m_ref[h:h + 1, :] = jnp.broadcast_to(m_new, (1, 128))

        hc = _head_norm(hh, normg_ref[:, sl], True) + skip_ref[:, sl] * xc_ref[:, sl]
        z = _mm(hn, win_ref[:, M_INNER + h * M_DH:M_INNER + (h + 1) * M_DH])
        y = y + _mm((hc * _silu(z)).astype(BF16), wdown_ref[sl, :])

    y_ref[0] = y
    m_out_ref[0] = m_ref[...]


def _odd_prompt(x, g_all, gi, ml, tt):
    bsz, seq, _ = x.shape
    bt = jnp.broadcast_to(ml['bgate_col'], (SUBLANES, tt))
    return pl.pallas_call(
        functools.partial(_odd_prompt_kernel, tt=tt, gi=gi),
        grid=(bsz, seq // tt),
        in_specs=[pl.BlockSpec((1, tt, D_MODEL), lambda b, t: (b, t, 0)),
                  _const_spec(g_all.shape),
                  _const_spec((D_MODEL, 2 * M_INNER)),
                  _const_spec((M_CONV, M_INNER)),
                  _const_spec((1, M_INNER)),
                  _const_spec((M_INNER // BD, BD, BD)),
                  _const_spec((M_INNER // BD, BD, BD)),
                  _const_spec((M_INNER // BD, BD, BD)),
                  _const_spec((3, SUBLANES, M_INNER)),
                  _const_spec((SUBLANES, tt)),
                  _const_spec((1, M_INNER)),
                  _const_spec((1, M_INNER)),
                  _const_spec((M_INNER, D_MODEL))],
        out_specs=[pl.BlockSpec((1, tt, D_MODEL), lambda b, t: (b, t, 0)),
                   pl.BlockSpec((1, M_HEADS, M_DH, M_DH), lambda b, t: (b, 0, 0, 0)),
                   pl.BlockSpec((1, M_HEADS, M_DH), lambda b, t: (b, 0, 0)),
                   pl.BlockSpec((1, SUBLANES, 128), lambda b, t: (b, 0, 0)),
                   pl.BlockSpec((1, M_CONV - 1, M_INNER), lambda b, t: (b, 0, 0))],
        out_shape=[jax.ShapeDtypeStruct(x.shape, F32),
                   jax.ShapeDtypeStruct((bsz, M_HEADS, M_DH, M_DH), F32),
                   jax.ShapeDtypeStruct((bsz, M_HEADS, M_DH), F32),
                   jax.ShapeDtypeStruct((bsz, SUBLANES, 128), F32),
                   jax.ShapeDtypeStruct((bsz, M_CONV - 1, M_INNER), F32)],
        scratch_shapes=[pltpu.VMEM((SUBLANES, M_INNER), F32), pltpu.VMEM((SUBLANES, 128), F32),
                        pltpu.VMEM((tt, M_INNER), F32), pltpu.VMEM((tt, M_INNER), BF16),
                        pltpu.VMEM((tt, M_INNER), BF16), pltpu.VMEM((tt, M_INNER), BF16)],
        compiler_params=_params(("arbitrary", "arbitrary")),
        name="odd_prompt",
    )(x, g_all, ml['win'], ml['convw'], ml['convb'], ml['wq'], ml['wk'], ml['wv'],
      ml['wgate_t'], bt, ml['normg'], ml['skip'], ml['wdown'])


def _pick_row(ref, b):
    r0 = pl.multiple_of((b // SUBLANES) * SUBLANES, SUBLANES)
    blk = ref[pl.ds(r0, SUBLANES), :]
    row = lax.broadcasted_iota(jnp.int32, blk.shape, 0)
    return jnp.sum(jnp.where(row == b % SUBLANES, blk, 0.0), axis=0, keepdims=True)


def _put_row(ref, b, val):
    r0 = pl.multiple_of((b // SUBLANES) * SUBLANES, SUBLANES)
    blk = ref[pl.ds(r0, SUBLANES), :]
    row = lax.broadcasted_iota(jnp.int32, blk.shape, 0)
    ref[pl.ds(r0, SUBLANES), :] = jnp.where(row == b % SUBLANES, jnp.broadcast_to(val, blk.shape), blk)


def _odd_sample_proj_kernel(x_ref, g_ref, win_ref, convw_ref, convb_ref, cv_ref, wq_ref, wk_ref, wv_ref,
                            wgate_ref, bgate_ref, m_in_ref, n_in_ref,
                            qt_ref, kwt_ref, v_ref, a_ref, den_ref, n_out_ref, m_out_ref, cv_out_ref,
                            xc_ref, z_ref, *, gi):
    hn = _rms(x_ref[...], g_ref[gi:gi + 1, :]).astype(BF16)
    p = _mm(hn, win_ref[...])
    xm = p[:, :M_INNER]
    z_ref[...] = p[:, M_INNER:]
    cv0 = cv_ref[:, :M_INNER]
    cv1 = cv_ref[:, M_INNER:2 * M_INNER]
    cv2 = cv_ref[:, 2 * M_INNER:]
    cv_out_ref[:, :M_INNER] = cv1
    cv_out_ref[:, M_INNER:2 * M_INNER] = cv2
    cv_out_ref[:, 2 * M_INNER:] = xm
    conv = (convb_ref[...] + convw_ref[3:4, :] * xm + convw_ref[2:3, :] * cv2
            + convw_ref[1:2, :] * cv1 + convw_ref[0:1, :] * cv0)
    xc = _silu(conv)
    xc_ref[...] = xc
    xcb = xc.astype(BF16)
    q = _headwise(xcb, wq_ref)
    k = _headwise(xcb, wk_ref)
    v = _headwise(xm.astype(BF16), wv_ref)
    v_ref[...] = v
    gate = (_mm(q.astype(BF16), wgate_ref[0]) + _mm(k.astype(BF16), wgate_ref[1])
            + _mm(v.astype(BF16), wgate_ref[2])) + bgate_ref[...]
    m_in = m_in_ref[...]
    lane = lax.broadcasted_iota(jnp.int32, m_in.shape, 1)
    a_all = jnp.zeros_like(m_in)
    den_all = jnp.zeros_like(m_in)
    m_all = jnp.zeros_like(m_in)
    for h in range(M_HEADS):
        sl = slice(h * M_DH, (h + 1) * M_DH)
        ig = gate[:, h:h + 1]
        lf = _log_sigmoid(gate[:, M_HEADS + h:M_HEADS + h + 1])
        m_prev = m_in[:, h:h + 1]
        m_new = jnp.maximum(lf + m_prev, ig)
        a = jnp.exp(lf + m_prev - m_new)
        kw = (jnp.exp(ig - m_new) * (M_DH ** -0.5)) * k[:, sl]
        n_new = a * n_in_ref[:, sl] + kw
        n_out_ref[:, sl] = n_new
        qh = q[:, sl]
        den = jnp.maximum(jnp.abs(jnp.sum(qh * n_new, axis=-1, keepdims=True)), jnp.exp(-m_new))
        a_all = jnp.where(lane == h, a, a_all)
        den_all = jnp.where(lane == h, den, den_all)
        m_all = jnp.where(lane == h, m_new, m_all)
        for c in range(M_DH // 128):
            qt_ref[h, c * 128:(c + 1) * 128, :] = qh[:, c * 128:(c + 1) * 128].T
            kwt_ref[h, c * 128:(c + 1) * 128, :] = kw[:, c * 128:(c + 1) * 128].T
    a_ref[...] = a_all
    den_ref[...] = den_all
    m_out_ref[...] = m_all


def _odd_sample_proj(x, g_all, gi, ml, n0, m0, conv0):
    n = x.shape[0]
    assert n == 128, "the per-head transposes assume one 128-lane tile of sequences"
    m_pad = jnp.pad(m0, ((0, 0), (0, 128 - M_HEADS)))
    cw = (M_CONV - 1) * M_INNER
    f32 = lambda *shape: jax.ShapeDtypeStruct(shape, F32)
    shapes = [f32(M_HEADS, M_DH, n), f32(M_HEADS, M_DH, n), f32(n, M_INNER), f32(n, 128), f32(n, 128),
              f32(n, M_INNER), f32(n, 128), f32(n, cw), f32(n, M_INNER), f32(n, M_INNER)]
    args = (x, g_all, ml['win'], ml['convw'], ml['convb'], conv0.reshape(n, cw), ml['wq'], ml['wk'], ml['wv'],
            ml['wgate'], ml['bgate'], m_pad, n0.reshape(n, M_INNER))
    outs = pl.pallas_call(
        functools.partial(_odd_sample_proj_kernel, gi=gi),
        grid=(1,),
        in_specs=[_const_spec(a.shape) for a in args],
        out_specs=[pl.BlockSpec(s.shape, lambda i, nd=len(s.shape): (0,) * nd) for s in shapes],
        out_shape=shapes,
        compiler_params=_params(("arbitrary",)),
        name="odd_sample_proj",
    )(*args)
    qt, kwt, v, a, den, n_new, m_new, conv_new, xc, z = outs
    return (qt, kwt, v, a, den, n_new.reshape(n, M_HEADS, M_DH), m_new[:, :M_HEADS],
            conv_new.reshape(n, M_CONV - 1, M_INNER), xc, z)


STREAM_IN_BUFS = 6
STREAM_OUT_BUFS = 6
STREAM_ROWS = 32


def _ffn_stream_kernel(xp_ref, g_ref, gfin_ref, wg_ref, wu_ref, wd_ref, qt_ref, kwt_ref, v_ref, a_ref, c_in,
                       op_ref, c_out, hnum_ref, cin_buf, cout_buf, sem_in, sem_out, *, gi, per_step):
    i = pl.program_id(0)
    n_pieces = per_step * M_HEADS

    def seq(p):
        return i * per_step + p // M_HEADS

    def in_copy(p, ahead=0):
        slot = p % STREAM_IN_BUFS
        s = seq(p) if not ahead else jnp.minimum(seq(p) + per_step, pl.num_programs(0) * per_step - 1)
        return pltpu.make_async_copy(c_in.at[s, p % M_HEADS], cin_buf.at[slot], sem_in.at[slot])

    def out_copy(p):
        slot = p % STREAM_OUT_BUFS
        return pltpu.make_async_copy(cout_buf.at[slot], c_out.at[seq(p), p % M_HEADS], sem_out.at[slot])

    @pl.when(i == 0)
    def _():
        hnum_ref[...] = jnp.zeros_like(hnum_ref)
        for p in range(STREAM_IN_BUFS):
            in_copy(p).start()

    shift = (128 - (i * per_step) % 128) % 128
    rolled = {}

    def columns(ref, h):
        if (id(ref), h) not in rolled:
            rolled[(id(ref), h)] = pltpu.roll(ref[h], shift, 1)
        return rolled[(id(ref), h)]

    rows = {}

    def seq_rows(j):
        if j not in rows:
            s = i * per_step + j
            rows[j] = (_pick_row(a_ref, s), _pick_row(v_ref, s))
        return rows[j]

    h_parts = {}

    def update(p):
        j, h = divmod(p, M_HEADS)
        a_row, v_row = seq_rows(j)
        kw_col = columns(kwt_ref, h)[:, j:j + 1]
        q_col = columns(qt_ref, h)[:, j:j + 1]
        a = a_row[:, h:h + 1]
        v_h = v_row[:, h * M_DH:(h + 1) * M_DH]
        acc = jnp.zeros((STREAM_ROWS, M_DH), F32)
        for r0 in range(0, M_DH, STREAM_ROWS):
            rs = slice(r0, r0 + STREAM_ROWS)
            c_new = a * cin_buf[p % STREAM_IN_BUFS, rs, :] + kw_col[rs] * v_h
            cout_buf[p % STREAM_OUT_BUFS, rs, :] = c_new
            acc = acc + q_col[rs] * c_new
        h_parts[(j, h)] = jnp.sum(acc, axis=0, keepdims=True)
        if h == M_HEADS - 1:
            _put_row(hnum_ref, i * per_step + j, jnp.concatenate([h_parts[(j, hh)] for hh in range(M_HEADS)], axis=1))

    def region(first, matmul):
        ps = (first, first + 1)
        for p in ps:
            in_copy(p).wait()
            if p >= STREAM_OUT_BUFS:
                out_copy(p - STREAM_OUT_BUFS).wait()
            else:
                pl.when(i > 0)(out_copy(p).wait)
        out = matmul()
        for p in ps:
            update(p)
        for p in ps:
            out_copy(p).start()
            if p + STREAM_IN_BUFS < n_pieces:
                in_copy(p + STREAM_IN_BUFS).start()
            else:
                in_copy(p % STREAM_IN_BUFS, ahead=1).start()
        return out

    x = xp_ref[...]
    hb = _rms(x, g_ref[gi:gi + 1, :]).astype(BF16)
    y = None
    first = 0
    for lo, hi in zip(FFN_BOUNDS[:-1], FFN_BOUNDS[1:]):
        half = lo + (hi - lo + MXU_TILE) // (2 * MXU_TILE) * MXU_TILE
        parts = []
        for w_ref in (wg_ref, wu_ref):
            for c0, c1 in ((lo, half), (half, hi)):
                parts.append(region(first, lambda w_ref=w_ref, c0=c0, c1=c1: _mm(hb, w_ref[:, c0:c1])))
                first += 2
        gt = jnp.concatenate(parts[:2], axis=1)
        ut = jnp.concatenate(parts[2:], axis=1)
        d = _mm((_silu(gt) * ut).astype(BF16), wd_ref[lo:hi, :])
        y = d if y is None else y + d
    assert first == n_pieces
    op_ref[...] = _rms(x + 0.5 * y, gfin_ref[...])
    @pl.when(i == pl.num_programs(0) - 1)
    def _():
        for p in range(n_pieces - STREAM_OUT_BUFS, n_pieces):
            out_copy(p).wait()
        for p in range(STREAM_IN_BUFS):
            in_copy(p, ahead=1).wait()


def _ffn_stream(xp, g_all, gfin, wg, wu, wd, gi, tm, qt, kwt, v, a, c0):
    n = xp.shape[0]
    ns = v.shape[0]
    steps = n // tm
    per_step = ns // steps
    assert per_step * steps == ns and 128 % per_step == 0
    any_spec = pl.BlockSpec(memory_space=pl.ANY)
    return pl.pallas_call(
        functools.partial(_ffn_stream_kernel, gi=gi, per_step=per_step),
        grid=(steps,),
        in_specs=[pl.BlockSpec((tm, D_MODEL), lambda i: (i, 0)),
                  _const_spec(g_all.shape),
                  _const_spec((1, D_MODEL)),
                  _const_spec((D_MODEL, D_FF)), _const_spec((D_MODEL, D_FF)), _const_spec((D_FF, D_MODEL)),
                  _const_spec(qt.shape), _const_spec(kwt.shape), _const_spec(v.shape), _const_spec(a.shape),
                  any_spec],
        out_specs=[pl.BlockSpec((tm, D_MODEL), lambda i: (i, 0)),
                   any_spec,
                   pl.BlockSpec((ns, M_INNER), lambda i: (0, 0))],
        out_shape=[jax.ShapeDtypeStruct((n, D_MODEL), F32),
                   jax.ShapeDtypeStruct(c0.shape, F32),
                   jax.ShapeDtypeStruct((ns, M_INNER), F32)],
        scratch_shapes=[pltpu.VMEM((STREAM_IN_BUFS, M_DH, M_DH), F32), pltpu.VMEM((STREAM_OUT_BUFS, M_DH, M_DH), F32),
                        pltpu.SemaphoreType.DMA((STREAM_IN_BUFS,)), pltpu.SemaphoreType.DMA((STREAM_OUT_BUFS,))],
        compiler_params=_params(("arbitrary",)),
        name="ffn_final_stream",
    )(xp, g_all, gfin, wg, wu, wd, qt, kwt, v, a, c0)


def _odd_sample_out_kernel(x_ref, hnum_ref, den_ref, xc_ref, z_ref, normg_ref, skip_ref, wdown_ref,
                           g_ref, gfin_ref, wg_ref, wu_ref, wd_ref, y_ref, *, gi):
    hnum = hnum_ref[...]
    den = den_ref[...]
    hs = [hnum[:, h * M_DH:(h + 1) * M_DH] * (1.0 / den[:, h:h + 1]) for h in range(M_HEADS)]
    xs = x_ref[...] + _mlstm_out(hs, xc_ref[...], z_ref[...], normg_ref, skip_ref, wdown_ref)
    y_ref[...] = _ffn_rows(xs, g_ref[gi:gi + 1, :], wg_ref, wu_ref, wd_ref, gfin_ref[...])


def _odd_sample_out(x, hnum, den, xc, z, ml, g_all, gfin, wg, wu, wd, gi):
    args = (x, hnum, den, xc, z, ml['normg'], ml['skip'], ml['wdown'], g_all, gfin, wg, wu, wd)
    return pl.pallas_call(
        functools.partial(_odd_sample_out_kernel, gi=gi),
        grid=(1,),
        in_specs=[_const_spec(a.shape) for a in args],
        out_specs=pl.BlockSpec(x.shape, lambda i: (0, 0)),
        out_shape=jax.ShapeDtypeStruct(x.shape, F32),
        compiler_params=_params(("arbitrary",)),
        name="odd_sample_out",
    )(*args)


TM_FFN = 512
TT_EVEN = 256
TT_ODD = 512


def kernel(x_prompt, x_sample, state_ret, state_hgrn, state_mlstm_C, state_mlstm_n, state_mlstm_m, state_mlstm_conv,
           norm_g, final_norm_g, ffn_w_gate, ffn_w_up, ffn_w_down, ev_w_in, ev_w_out, ret_norm_g, hg_norm_g,
           hg_lb_logits, ml_w_in, ml_conv_w, ml_conv_b, ml_w_q, ml_w_k, ml_w_v, ml_w_ig, ml_b_ig, ml_w_fg,
           ml_b_fg, ml_norm_g, ml_skip, ml_w_down):
    bp, tp, _ = x_prompt.shape
    ns = x_sample.shape[0]

    g_all = norm_g.reshape(-1, D_MODEL)
    gfin = final_norm_g.reshape(1, D_MODEL)
    lb_all = jnp.cumsum(jax.nn.softmax(hg_lb_logits.astype(F32), axis=0), axis=0)
    lb = lb_all[0].reshape(1, G_HEADS * G_EXP)
    retg = ret_norm_g[0]
    hgg = hg_norm_g[0]
    wgate = _gate_weights(ml_w_ig[0], ml_w_fg[0])
    bgate = jnp.pad(jnp.concatenate([ml_b_ig[0], ml_b_fg[0]]), (0, 128 - 2 * M_HEADS)).reshape(1, 128)
    ml = {
        'convw': ml_conv_w[0],
        'convb': ml_conv_b[0].reshape(1, M_INNER),
        'wq': _block_diag(ml_w_q[0]).astype(BF16),
        'wk': _block_diag(ml_w_k[0]).astype(BF16),
        'wv': _block_diag(ml_w_v[0]).astype(BF16),
        'wgate': wgate.astype(BF16),
        'wgate_t': jnp.swapaxes(wgate[:, :, :SUBLANES], 1, 2).astype(BF16),
        'bgate': bgate,
        'bgate_col': bgate[0, :SUBLANES].reshape(SUBLANES, 1),
        'normg': ml_norm_g[0].reshape(1, M_INNER),
        'skip': ml_skip[0].reshape(1, M_INNER),
    }

    xp = x_prompt.reshape(bp * tp, D_MODEL)
    xs = x_sample.reshape(ns, D_MODEL)
    ffn_w = (ffn_w_gate, ffn_w_up, ffn_w_down)
    next_ffn = lambda layer, idx: [(w, (layer, idx)) for w in ffn_w]

    w00 = [w[0, 0].astype(BF16) for w in ffn_w]
    xp, xs, cast = _ffn(xp, xs, g_all, gfin, *w00, 0, TM_FFN,
                        casts=next_ffn(0, 1) + [(ev_w_in, (0,)), (ev_w_out, (0,))])
    w01, (ev_in, ev_out) = cast[:3], cast[3:]
    xp, ret_p, hg_p = _even_prompt(xp.reshape(bp, tp, D_MODEL), g_all, 1, ev_in, ev_out, retg, hgg, lb, TT_EVEN)
    xs, ret_s, hg_s = _even_sample(xs, g_all, 1, ev_in, ev_out, retg, hgg, lb, state_ret[:, 0], state_hgrn[:, 0])
    xp, xs, cast = _ffn(xp.reshape(bp * tp, D_MODEL), xs, g_all, gfin, *w01, 2, TM_FFN,
                        casts=next_ffn(1, 0) + [(ml_w_in, (0,)), (ml_w_down, (0,))])
    w10, (ml['win'], ml['wdown']) = cast[:3], cast[3:]
    xp, xs, w11 = _ffn(xp, xs, g_all, gfin, *w10, 3, TM_FFN, casts=next_ffn(1, 1))
    xp, c_p, n_p, m_p, conv_p = _odd_prompt(xp.reshape(bp, tp, D_MODEL), g_all, 4, ml, TT_ODD)
    qt, kwt, v_s, a_s, den_s, n_s, m_s, conv_s, xc_s, z_s = _odd_sample_proj(
        xs, g_all, 4, ml, state_mlstm_n[:, 0], state_mlstm_m[:, 0], state_mlstm_conv[:, 0])
    y_p, c_s, hnum = _ffn_stream(xp.reshape(bp * tp, D_MODEL), g_all, gfin, *w11, 5, TM_FFN,
                                 qt, kwt, v_s, a_s, state_mlstm_C[:, 0])
    y_s = _odd_sample_out(xs, hnum, den_s, xc_s, z_s, ml, g_all, gfin, *w11, 5)

    return (y_p.reshape(bp, tp, D_MODEL), y_s.reshape(ns, 1, D_MODEL),
            ret_p[:, None], hg_p[:, None], c_p[:, None], n_p[:, None], m_p[:, None, :M_HEADS, 0], conv_p[:, None],
            ret_s[:, None], hg_s[:, None], c_s[:, None], n_s[:, None], m_s[:, None], conv_s[:, None])
```

```python
import functools
import math

import jax
import jax.numpy as jnp
from jax import lax
from jax.experimental import pallas as pl
from jax.experimental.pallas import tpu as pltpu

D_MODEL = 1024
PAST_LEN = 16384
R_HEADS = 4
R_DK = 128
R_DV = 128
G_HEADS = 4
G_EXP = 128
G_DV = 128
M_INNER = 2 * D_MODEL
M_HEADS = 4
M_DH = M_INNER // M_HEADS
M_CONV = 4
QKV_BLOCK = 4
D_FF = 2816
EPS = 1e-6
ROPE_BASE = 10000.0
EVEN_IN = 4096
EVEN_OUT = 1024

F32 = jnp.float32
BF16 = jnp.bfloat16

VMEM_LIMIT_BYTES = 56 * 1024 * 1024

HG_CHUNK = 64
HG_SUB = 16
HG_SAFE_LOG_DECAY = -60.0
MXU_TILE = 256
FFN_BOUNDS = (0, 6 * MXU_TILE, D_FF)
BD = MXU_TILE
SUBLANES = 8


def _nt(a, b):
    return lax.dot_general(a, b, (((1,), (1,)), ((), ())), preferred_element_type=F32)


def _tn(a, b):
    return lax.dot_general(a, b, (((0,), (0,)), ((), ())), preferred_element_type=F32)


def _mm(a, b):
    return jnp.dot(a, b, preferred_element_type=F32)


def _sigmoid(x):
    return 1.0 / (1.0 + jnp.exp(-x))


def _silu(x):
    return x * _sigmoid(x)


def _log_sigmoid(x):
    return jnp.minimum(x, 0.0) - jnp.log(1.0 + jnp.exp(-jnp.abs(x)))


def _rms(x, g):
    return x * lax.rsqrt(jnp.mean(x * x, axis=-1, keepdims=True) + EPS) * g


def _head_norm(x, g, center):
    if center:
        x = x - jnp.mean(x, axis=-1, keepdims=True)
    return x * lax.rsqrt(jnp.mean(x * x, axis=-1, keepdims=True) + EPS) * g


def _rotary(x, cos, sin_signed):
    return x * cos + pltpu.roll(x, 64, 1) * sin_signed


def _const_spec(shape):
    n = len(shape)
    return pl.BlockSpec(shape, lambda *_: (0,) * n, pipeline_mode=pl.Buffered(1))


def _params(sem):
    return pltpu.CompilerParams(dimension_semantics=sem, vmem_limit_bytes=VMEM_LIMIT_BYTES)


def _ffn_rows(x, g, wg_ref, wu_ref, wd_ref, gfin):
    h = _rms(x, g).astype(BF16)
    y = jnp.zeros_like(x)
    for lo, hi in zip(FFN_BOUNDS[:-1], FFN_BOUNDS[1:]):
        gt = _mm(h, wg_ref[:, lo:hi])
        ut = _mm(h, wu_ref[:, lo:hi])
        a = (_silu(gt) * ut).astype(BF16)
        y = y + _mm(a, wd_ref[lo:hi, :])
    out = x + 0.5 * y
    if gfin is not None:
        out = _rms(out, gfin)
    return out


def _ffn_kernel(xp_ref, xs_ref, g_ref, gfin_ref, wg_ref, wu_ref, wd_ref, *rest, gi, final, n_cast):
    cast_in = rest[:n_cast]
    op_ref, os_ref = rest[n_cast:n_cast + 2]
    cast_out = rest[n_cast + 2:]
    g = g_ref[gi:gi + 1, :]
    gfin = gfin_ref[...] if final else None
    op_ref[...] = _ffn_rows(xp_ref[...], g, wg_ref, wu_ref, wd_ref, gfin)
    for src, dst in zip(cast_in, cast_out):
        dst[...] = src[...].astype(BF16)

    @pl.when(pl.program_id(0) == pl.num_programs(0) - 1)
    def _():
        os_ref[...] = _ffn_rows(xs_ref[...], g, wg_ref, wu_ref, wd_ref, gfin)


BF16_ROWS = 16


def _cast_specs(arr, lead, steps):
    rows, cols = arr.shape[-2:]
    per = 1 if (rows // steps) % BF16_ROWS == 0 else 2
    br = rows * per // steps
    in_spec = pl.BlockSpec((None,) * len(lead) + (br, cols), lambda i: tuple(lead) + (i // per, 0))
    out_spec = pl.BlockSpec((br, cols), lambda i: (i // per, 0))
    return in_spec, out_spec, jax.ShapeDtypeStruct((rows, cols), BF16)


def _ffn(xp, xs, g_all, gfin, wg, wu, wd, gi, tm, final=False, casts=()):
    n = xp.shape[0]
    ns = xs.shape[0]
    steps = n // tm
    cast_specs = [_cast_specs(arr, lead, steps) for arr, lead in casts]
    outs = pl.pallas_call(
        functools.partial(_ffn_kernel, gi=gi, final=final, n_cast=len(casts)),
        grid=(steps,),
        in_specs=[pl.BlockSpec((tm, D_MODEL), lambda i: (i, 0)),
                  _const_spec((ns, D_MODEL)),
                  _const_spec(g_all.shape),
                  _const_spec((1, D_MODEL)),
                  _const_spec((D_MODEL, D_FF)), _const_spec((D_MODEL, D_FF)), _const_spec((D_FF, D_MODEL))]
                 + [c[0] for c in cast_specs],
        out_specs=[pl.BlockSpec((tm, D_MODEL), lambda i: (i, 0)),
                   pl.BlockSpec((ns, D_MODEL), lambda i: (0, 0))] + [c[1] for c in cast_specs],
        out_shape=[jax.ShapeDtypeStruct((n, D_MODEL), F32), jax.ShapeDtypeStruct((ns, D_MODEL), F32)]
                  + [c[2] for c in cast_specs],
        compiler_params=_params(("arbitrary",)),
        name="ffn_final" if final else "ffn",
    )(xp, xs, g_all, gfin, wg, wu, wd, *[arr for arr, _ in casts])
    return outs[0], outs[1], outs[2:]


def _ret_log_gamma(h):
    return math.log(1.0 - 2.0 ** (-5.0 - h))


def _head_parts(p, mixer, h):
    base = mixer * 4 * R_HEADS * R_DK
    return [p[:, base + (j * R_HEADS + h) * R_DK:base + (j * R_HEADS + h + 1) * R_DK] for j in range(4)]


def _hgrn_gates(gq, gf, lb):
    f = lb + (1.0 - lb) * _sigmoid(gf)
    kk = (1.0 - lb) * _sigmoid(-gf)
    qq = _silu(gq)
    return qq, kk, f


def _split3(x):
    hi = x.astype(BF16)
    r1 = x - hi.astype(F32)
    mid = r1.astype(BF16)
    lo = (r1 - mid.astype(F32)).astype(BF16)
    return hi, mid, lo


def _shift_rows(bases, d):
    base = bases[d % SUBLANES]
    full = (d // SUBLANES) * SUBLANES
    return pltpu.roll(base, full, 0) if full else base


def _hgrn_tile_factorised(qq, kk, vv, b, st):
    tt = qq[0].shape[0]
    heads = range(len(qq))
    qx = [(qq[h] * jnp.exp(b[h])).astype(BF16) for h in heads]
    kx = [kk[h] * jnp.exp(-b[h]) for h in heads]
    kxb = [kx[h].astype(BF16) for h in heads]
    vb = [vv[h].astype(BF16) for h in heads]
    ti = lax.broadcasted_iota(jnp.int32, (HG_CHUNK, HG_CHUNK), 0)
    si = lax.broadcasted_iota(jnp.int32, (HG_CHUNK, HG_CHUNK), 1)
    causal = si <= ti
    st = list(st)
    o_chunks = [[] for _ in heads]
    for c in range(tt // HG_CHUNK):
        rs = slice(c * HG_CHUNK, (c + 1) * HG_CHUNK)
        for h in heads:
            a = jnp.where(causal, _nt(qx[h][rs], kxb[h][rs]), 0.0)
            o_chunks[h].append(_mm(a.astype(BF16), vb[h][rs]) + _nt(qx[h][rs], st[h].astype(BF16)))
            etot = jnp.exp(b[h][(c + 1) * HG_CHUNK - 1:(c + 1) * HG_CHUNK])
            st[h] = st[h] * etot + _tn(vb[h][rs], (kx[h][rs] * etot).astype(BF16))
    return [jnp.concatenate(o_chunks[h], axis=0) for h in heads], st


def _hgrn_tile_guarded(qq_all, kk_all, f_all, vv_all, b_all, st_all):
    tt = qq_all.shape[0]
    hd = G_EXP
    row = lax.broadcasted_iota(jnp.int32, (tt, hd), 0)
    rsub = row % HG_SUB
    sub = (lax.broadcasted_iota(jnp.int32, (HG_CHUNK, hd), 0)) // HG_SUB
    n_sub = HG_CHUNK // HG_SUB
    outs, states = [], []
    for h in range(G_HEADS):
        hs = slice(h * hd, (h + 1) * hd)
        qq = qq_all[:, hs]
        kk = kk_all[:, hs]
        ff = f_all[:, hs]
        bc_all = b_all[:, hs]
        vv = vv_all[:, hs]
        f_sh = [ff] + [pltpu.roll(ff, r, 0) for r in range(1, SUBLANES)]
        k_sh = [kk] + [pltpu.roll(kk, r, 0) for r in range(1, SUBLANES)]
        v_sh = [vv] + [pltpu.roll(vv, r, 0) for r in range(1, SUBLANES)]
        o_band = jnp.sum(qq * kk, axis=-1, keepdims=True) * vv
        decay = None
        for d in range(1, HG_SUB):
            fd = _shift_rows(f_sh, d - 1)
            decay = jnp.where(rsub >= d, fd if decay is None else decay * fd, 0.0)
            term = qq * _shift_rows(k_sh, d) * decay
            o_band = o_band + jnp.sum(term, axis=-1, keepdims=True) * _shift_rows(v_sh, d)
        st = st_all[h]
        o_chunks = []
        for c in range(tt // HG_CHUNK):
            r0 = c * HG_CHUNK
            bc = bc_all[r0:r0 + HG_CHUNK]
            qc = qq[r0:r0 + HG_CHUNK]
            kc = kk[r0:r0 + HG_CHUNK]
            vcb = vv[r0:r0 + HG_CHUNK].astype(BF16)
            refs = [bc[i * HG_SUB - 1:i * HG_SUB] for i in range(1, n_sub)]
            refrow = refs[-1]
            for i in range(n_sub - 2, 0, -1):
                refrow = jnp.where(sub == i, refs[i - 1], refrow)
            qp = qc * jnp.exp(bc - refrow)
            lhs = jnp.concatenate([jnp.where(sub == i, qp, 0.0) for i in range(1, n_sub)], axis=1)
            kcat = jnp.concatenate([jnp.where(sub < i, kc * jnp.exp(refs[i - 1] - bc), 0.0)
                                    for i in range(1, n_sub)], axis=1)
            a = _nt(lhs.astype(BF16), kcat.astype(BF16))
            qb = (qc * jnp.exp(bc)).astype(BF16)
            o_chunks.append(_mm(a.astype(BF16), vcb) + _nt(qb, st.astype(BF16)))
            btot = bc[HG_CHUNK - 1:HG_CHUNK]
            ke = (kc * jnp.exp(btot - bc)).astype(BF16)
            st = st * jnp.exp(btot) + _tn(vcb, ke)
        outs.append(o_band + jnp.concatenate(o_chunks, axis=0))
        states.append(st)
    return jnp.concatenate(outs, axis=1), jnp.stack(states)


def _even_prompt_kernel(x_ref, g_ref, win_ref, wout_ref, cos_ref, sin_ref, retg_ref, hgg_ref, lb_ref,
                        y_ref, sret_ref, shg_ref, dmat_ref, tri_ref, st_ref, ohg_ref, *, tt, gi):
    b_id = pl.program_id(0)
    t_id = pl.program_id(1)
    n_t = pl.num_programs(1)
    hd = 128

    @pl.when(jnp.logical_and(b_id == 0, t_id == 0))
    def _():
        ti = lax.broadcasted_iota(jnp.int32, (tt, tt), 0)
        si = lax.broadcasted_iota(jnp.int32, (tt, tt), 1)
        diff = (ti - si).astype(F32)
        for h in range(R_HEADS):
            dmat_ref[h] = jnp.where(diff >= 0.0, jnp.exp(_ret_log_gamma(h) * jnp.maximum(diff, 0.0)), 0.0)
        same_chunk = (ti // HG_CHUNK) == (si // HG_CHUNK)
        tri_ref[...] = jnp.where(jnp.logical_and(same_chunk, si <= ti), 1.0, 0.0).astype(BF16)

    @pl.when(t_id == 0)
    def _():
        sret_ref[...] = jnp.zeros_like(sret_ref)
        st_ref[...] = jnp.zeros_like(st_ref)

    x = x_ref[0]
    hn = _rms(x, g_ref[gi:gi + 1, :]).astype(BF16)
    cos = cos_ref[...]
    sin = sin_ref[...]
    rowf = lax.broadcasted_iota(jnp.int32, (tt, hd), 0).astype(F32)
    outs = []
    hg = []
    logf_parts = []
    st_old = st_ref[...]

    p = _mm(hn, win_ref[...])
    for i in range(R_HEADS + G_HEADS):
        h = i // 2
        pa, pb, pc, pd = _head_parts(p, i % 2, h)
        if i % 2 == 0:
            lg = _ret_log_gamma(h)
            q = _rotary(pa, cos, sin)
            k = _rotary(pb, cos, sin) * (R_DK ** -0.5)
            v = pc
            rg = pd
            s0 = sret_ref[0, h]
            qb = q.astype(BF16)
            vb = v.astype(BF16)
            inter = _mm(qb, s0.astype(BF16)) * jnp.exp(lg * (rowf + 1.0))
            scores = _nt(qb, k.astype(BF16)) * dmat_ref[h]
            intra = _mm(scores.astype(BF16), vb)
            kd = (k * jnp.exp(lg * (tt - 1.0 - rowf))).astype(BF16)
            sret_ref[0, h] = math.exp(lg * tt) * s0 + _tn(kd, vb)
            outs.append(_head_norm(inter + intra, retg_ref[h:h + 1, :], False) * _silu(rg))
        else:
            qq, kk, ff = _hgrn_gates(pa, pb, lb_ref[:, h * hd:(h + 1) * hd])
            hg.append((qq, kk, ff, pc, _sigmoid(pd)))
            logf_parts.extend(_split3(jnp.log(ff)))

    cs = _mm(tri_ref[...], jnp.concatenate(logf_parts, axis=1))
    b = [cs[:, (3 * h) * hd:(3 * h + 1) * hd] + cs[:, (3 * h + 1) * hd:(3 * h + 2) * hd]
         + cs[:, (3 * h + 2) * hd:(3 * h + 3) * hd] for h in range(G_HEADS)]
    b_all = jnp.concatenate(b, axis=1)
    o_fast, st_fast = _hgrn_tile_factorised([t[0] for t in hg], [t[1] for t in hg], [t[3] for t in hg], b,
                                            [st_old[h] for h in range(G_HEADS)])
    for h in range(G_HEADS):
        ohg_ref[:, h * hd:(h + 1) * hd] = o_fast[h]
        st_ref[h] = st_fast[h]

    @pl.when(jnp.min(b_all) < HG_SAFE_LOG_DECAY)
    def _():
        cat = lambda j: jnp.concatenate([t[j] for t in hg], axis=1)
        o_safe, st_safe = _hgrn_tile_guarded(cat(0), cat(1), cat(2), cat(3), b_all, st_old)
        ohg_ref[...] = o_safe
        st_ref[...] = st_safe

    for h in range(G_HEADS):
        outs.append(_head_norm(ohg_ref[:, h * hd:(h + 1) * hd], hgg_ref[h:h + 1, :], False) * hg[h][4])


    ycat = jnp.concatenate(outs, axis=1).astype(BF16)
    y_ref[0] = x + _mm(ycat, wout_ref[...])

    @pl.when(t_id == n_t - 1)
    def _():
        for h in range(G_HEADS):
            shg_ref[0, h] = st_ref[h].T


def _rope_tables(pos):
    half = R_DK // 2
    inv = ROPE_BASE ** (-jnp.arange(half, dtype=F32) / half)
    ang = pos.astype(F32)[:, None] * inv[None, :]
    cos = jnp.cos(ang)
    sin = jnp.sin(ang)
    return jnp.concatenate([cos, cos], axis=-1), jnp.concatenate([-sin, sin], axis=-1)


def _even_prompt(x, g_all, gi, win, wout, retg, hgg, lb, tt):
    bsz, seq, _ = x.shape
    cos, sin = _rope_tables(jnp.arange(seq, dtype=jnp.int32))
    state_spec = pl.BlockSpec((1, 4, 128, 128), lambda b, t: (b, 0, 0, 0))
    return pl.pallas_call(
        functools.partial(_even_prompt_kernel, tt=tt, gi=gi),
        grid=(bsz, seq // tt),
        in_specs=[pl.BlockSpec((1, tt, D_MODEL), lambda b, t: (b, t, 0)),
                  _const_spec(g_all.shape),
                  _const_spec((D_MODEL, EVEN_IN)),
                  _const_spec((EVEN_OUT, D_MODEL)),
                  pl.BlockSpec((tt, 128), lambda b, t: (t, 0)),
                  pl.BlockSpec((tt, 128), lambda b, t: (t, 0)),
                  _const_spec((4, 128)), _const_spec((4, 128)), _const_spec((1, 512))],
        out_specs=[pl.BlockSpec((1, tt, D_MODEL), lambda b, t: (b, t, 0)), state_spec, state_spec],
        out_shape=[jax.ShapeDtypeStruct(x.shape, F32),
                   jax.ShapeDtypeStruct((bsz, 4, 128, 128), F32),
                   jax.ShapeDtypeStruct((bsz, 4, 128, 128), F32)],
        scratch_shapes=[pltpu.VMEM((4, tt, tt), F32), pltpu.VMEM((tt, tt), BF16), pltpu.VMEM((4, 128, 128), F32),
                        pltpu.VMEM((tt, G_HEADS * G_DV), F32)],
        compiler_params=_params(("arbitrary", "arbitrary")),
        name="even_prompt",
    )(x, g_all, win, wout, cos, sin, retg, hgg, lb)


SB = 8


def _even_sample_kernel(x_ref, g_ref, win_ref, wout_ref, cos_ref, sin_ref, retg_ref, hgg_ref, lb_ref,
                        sret_in, shg_in, y_ref, sret_out, shg_out, p_ref, o_ref, *, gi):
    i = pl.program_id(0)
    n_i = pl.num_programs(0)
    hd = 128

    @pl.when(i == 0)
    def _():
        hn = _rms(x_ref[...], g_ref[gi:gi + 1, :]).astype(BF16)
        p_ref[...] = _mm(hn, win_ref[...])

    r0 = pl.multiple_of(i * SB, SB)
    p = p_ref[pl.ds(r0, SB), :]
    cos = cos_ref[...]
    sin = sin_ref[...]
    row = lax.broadcasted_iota(jnp.int32, (SB, hd), 0)
    outs = []
    for h in range(R_HEADS):
        gamma = math.exp(_ret_log_gamma(h))
        pa, pb, pc, rg = _head_parts(p, 0, h)
        q = _rotary(pa, cos, sin).astype(BF16)
        k = _rotary(pb, cos, sin) * (R_DK ** -0.5)
        vb = pc.astype(BF16)
        o = jnp.zeros((SB, hd), F32)
        for j in range(SB):
            kj = jnp.where(row == j, k, 0.0).astype(BF16)
            s_new = gamma * sret_in[j, h] + _tn(kj, vb)
            sret_out[j, h] = s_new
            o = jnp.where(row == j, _mm(q, s_new.astype(BF16)), o)
        outs.append(_head_norm(o, retg_ref[h:h + 1, :], False) * _silu(rg))
    for h in range(G_HEADS):
        pa, pb, pc, gg = _head_parts(p, 1, h)
        qq, kk, ff = _hgrn_gates(pa, pb, lb_ref[:, h * hd:(h + 1) * hd])
        vb = pc.astype(BF16)
        f_cols = jnp.concatenate([ff, jnp.zeros((hd - SB, hd), F32)], axis=0).T
        qb = qq.astype(BF16)
        o = jnp.zeros((SB, hd), F32)
        for j in range(SB):
            kj = jnp.where(row == j, kk, 0.0).astype(BF16)
            s_new = f_cols[:, j:j + 1] * shg_in[j, h] + _tn(kj, vb)
            shg_out[j, h] = s_new
            o = jnp.where(row == j, _mm(qb, s_new.astype(BF16)), o)
        outs.append(_head_norm(o, hgg_ref[h:h + 1, :], False) * _sigmoid(gg))
    o_ref[pl.ds(r0, SB), :] = jnp.concatenate(outs, axis=1)

    @pl.when(i == n_i - 1)
    def _():
        y_ref[...] = x_ref[...] + _mm(o_ref[...].astype(BF16), wout_ref[...])


def _even_sample(x, g_all, gi, win, wout, retg, hgg, lb, sret, shg):
    n = x.shape[0]
    cos, sin = _rope_tables(jnp.full((1,), PAST_LEN, dtype=jnp.int32))
    state_spec = pl.BlockSpec((SB, 4, 128, 128), lambda i: (i, 0, 0, 0))
    return pl.pallas_call(
        functools.partial(_even_sample_kernel, gi=gi),
        grid=(n // SB,),
        in_specs=[_const_spec((n, D_MODEL)),
                  _const_spec(g_all.shape),
                  _const_spec((D_MODEL, EVEN_IN)),
                  _const_spec((EVEN_OUT, D_MODEL)),
                  _const_spec((1, 128)), _const_spec((1, 128)),
                  _const_spec((4, 128)), _const_spec((4, 128)), _const_spec((1, 512)),
                  state_spec, state_spec],
        out_specs=[pl.BlockSpec((n, D_MODEL), lambda i: (0, 0)), state_spec, state_spec],
        out_shape=[jax.ShapeDtypeStruct((n, D_MODEL), F32),
                   jax.ShapeDtypeStruct(sret.shape, F32),
                   jax.ShapeDtypeStruct(shg.shape, F32)],
        scratch_shapes=[pltpu.VMEM((n, EVEN_IN), F32), pltpu.VMEM((n, EVEN_OUT), F32)],
        compiler_params=_params(("arbitrary",)),
        name="even_sample",
    )(x, g_all, win, wout, cos, sin, retg, hgg, lb, sret, shg)


def _block_diag(w):
    wr = w.reshape(M_INNER // BD, BD, QKV_BLOCK)
    tiled = jnp.tile(wr, (1, 1, BD // QKV_BLOCK))
    rb = lax.broadcasted_iota(jnp.int32, (BD, BD), 0) // QKV_BLOCK
    cb = lax.broadcasted_iota(jnp.int32, (BD, BD), 1) // QKV_BLOCK
    return jnp.where((rb == cb)[None], tiled, 0.0)


def _headwise(xb, w_ref, g0=0):
    return jnp.concatenate([_mm(xb[:, g * BD:(g + 1) * BD], w_ref[g0 + g]) for g in range(xb.shape[1] // BD)], axis=1)


def _gate_weights(w_ig, w_fg):
    w = jnp.concatenate([w_ig, w_fg], axis=1)
    w = jnp.pad(w, ((0, 0), (0, 128 - 2 * M_HEADS)))
    return w.reshape(3, M_INNER, 128)


def _mlstm_out(hs, xc, z, normg_ref, skip_ref, wdown_ref):
    hc = jnp.concatenate([_head_norm(hs[h], normg_ref[:, h * M_DH:(h + 1) * M_DH], True) for h in range(M_HEADS)],
                         axis=1)
    hc = hc + skip_ref[...] * xc
    return _mm((hc * _silu(z)).astype(BF16), wdown_ref[...])


def _odd_prompt_kernel(x_ref, g_ref, win_ref, convw_ref, convb_ref, wq_ref, wk_ref, wv_ref,
                       wgate_t_ref, bgate_t_ref, normg_ref, skip_ref, wdown_ref,
                       y_ref, c_ref, n_ref, m_out_ref, conv_out_ref,
                       carry_ref, m_ref, xc_ref, q_ref, k_ref, v_ref, *, tt, gi):
    t_id = pl.program_id(1)
    k_scale = M_DH ** -0.5

    @pl.when(t_id == 0)
    def _():
        c_ref[...] = jnp.zeros_like(c_ref)
        n_ref[...] = jnp.zeros_like(n_ref)
        m_ref[...] = jnp.zeros_like(m_ref)
        carry_ref[...] = jnp.zeros_like(carry_ref)

    x = x_ref[0]
    hn = _rms(x, g_ref[gi:gi + 1, :]).astype(BF16)

    gates_t = bgate_t_ref[...]
    row8 = lax.broadcasted_iota(jnp.int32, (SUBLANES, M_DH), 0)
    tiles = M_DH // BD
    xm_next = _mm(hn, win_ref[:, :M_DH])
    for h in range(M_HEADS):
        sl = slice(h * M_DH, (h + 1) * M_DH)
        xm = xm_next
        if h + 1 < M_HEADS:
            xm_next = _mm(hn, win_ref[:, (h + 1) * M_DH:(h + 2) * M_DH])
        carry = carry_ref[:, sl]
        conv = convb_ref[:, sl] + convw_ref[M_CONV - 1:M_CONV, sl] * xm
        for j in range(1, M_CONV):
            rolled = pltpu.roll(xm, j, 0)
            head = jnp.where(row8 < j, pltpu.roll(carry, j, 0), rolled[:SUBLANES])
            shifted = jnp.concatenate([head, rolled[SUBLANES:]], axis=0)
            conv = conv + convw_ref[M_CONV - 1 - j:M_CONV - j, sl] * shifted
        carry_ref[:, sl] = xm[tt - SUBLANES:, :]
        conv_out_ref[0, :, sl] = xm[tt - (M_CONV - 1):, :]
        xc = _silu(conv)
        xc_ref[:, sl] = xc
        xcb = xc.astype(BF16)
        qb = _headwise(xcb, wq_ref, h * tiles).astype(BF16)
        kb = _headwise(xcb, wk_ref, h * tiles).astype(BF16)
        vb = _headwise(xm.astype(BF16), wv_ref, h * tiles).astype(BF16)
        q_ref[:, sl] = qb
        k_ref[:, sl] = kb
        v_ref[:, sl] = vb
        gates_t = gates_t + (_nt(wgate_t_ref[0, :, sl], qb) + _nt(wgate_t_ref[1, :, sl], kb)
                             + _nt(wgate_t_ref[2, :, sl], vb))

    lane8 = lax.broadcasted_iota(jnp.int32, (SUBLANES, tt), 1)
    row8t = lax.broadcasted_iota(jnp.int32, (SUBLANES, tt), 0)
    brow_all = _log_sigmoid(gates_t)
    d = 1
    while d < tt:
        brow_all = brow_all + jnp.where(lane8 >= d, pltpu.roll(brow_all, d, 1), 0.0)
        d *= 2
    rows = jnp.where(row8t < M_HEADS, gates_t, brow_all)
    pad = jnp.zeros((128 - SUBLANES, 128), F32)
    cols = jnp.concatenate([jnp.concatenate([rows[:, j * 128:(j + 1) * 128], pad], axis=0).T
                            for j in range(tt // 128)], axis=0)

    ti = lax.broadcasted_iota(jnp.int32, (tt, tt), 0)
    si = lax.broadcasted_iota(jnp.int32, (tt, tt), 1)
    causal = si <= ti
    y = x
    for h in range(M_HEADS):
        sl = slice(h * M_DH, (h + 1) * M_DH)
        qhb = q_ref[:, sl]
        khb = k_ref[:, sl]
        vhb = v_ref[:, sl]
        ig_col = cols[:, h:h + 1]
        b_col = cols[:, M_HEADS + h:M_HEADS + h + 1]
        ig_row = gates_t[h:h + 1, :]
        b_row = brow_all[M_HEADS + h:M_HEADS + h + 1, :]
        m_prev = m_ref[h:h + 1, 0:1]
        c_prev = c_ref[0, h]
        n_prev = n_ref[0, h:h + 1, :]

        dlog = jnp.where(causal, b_col + (ig_row - b_row), -jnp.inf)
        inter_log = b_col + m_prev
        m_row = jnp.maximum(inter_log, jnp.max(dlog, axis=-1, keepdims=True))
        w_inter = jnp.exp(inter_log - m_row)
        qk = _nt(qhb, khb) * (jnp.exp(dlog - m_row) * k_scale)
        num = w_inter * _mm(qhb, c_prev.astype(BF16)) + _mm(qk.astype(BF16), vhb)
        qn = _nt(qhb, jnp.broadcast_to(n_prev, (SUBLANES, M_DH)).astype(BF16))[:, 0:1]
        den = w_inter * qn + jnp.sum(qk, axis=-1, keepdims=True)
        den = jnp.maximum(jnp.abs(den), jnp.exp(-m_row))
        hh = num * (1.0 / den)

        b_end = b_col[tt - 1:tt, :]
        s_log = b_end - b_col + ig_col
        m_new = jnp.maximum(b_end + m_prev, jnp.max(s_log, axis=0, keepdims=True))
        a = jnp.exp(b_end + m_prev - m_new)
        kw = khb.astype(F32) * (jnp.exp(s_log - m_new) * k_scale)
        c_ref[0, h] = a * c_prev + _tn(kw.astype(BF16), vhb)
        n_ref[0, h:h + 1, :] = a * n_prev + jnp.sum(kw, axis=0, keepdims=True)
        m_ref[h:h + 1, :] = jnp.broadcast_to(m_new, (1, 128))

        hc = _head_norm(hh, normg_ref[:, sl], True) + skip_ref[:, sl] * xc_ref[:, sl]
        z = _mm(hn, win_ref[:, M_INNER + h * M_DH:M_INNER + (h + 1) * M_DH])
        y = y + _mm((hc * _silu(z)).astype(BF16), wdown_ref[sl, :])

    y_ref[0] = y
    m_out_ref[0] = m_ref[...]


def _odd_prompt(x, g_all, gi, ml, tt):
    bsz, seq, _ = x.shape
    bt = jnp.broadcast_to(ml['bgate_col'], (SUBLANES, tt))
    return pl.pallas_call(
        functools.partial(_odd_prompt_kernel, tt=tt, gi=gi),
        grid=(bsz, seq // tt),
        in_specs=[pl.BlockSpec((1, tt, D_MODEL), lambda b, t: (b, t, 0)),
                  _const_spec(g_all.shape),
                  _const_spec((D_MODEL, 2 * M_INNER)),
                  _const_spec((M_CONV, M_INNER)),
                  _const_spec((1, M_INNER)),
                  _const_spec((M_INNER // BD, BD, BD)),
                  _const_spec((M_INNER // BD, BD, BD)),
                  _const_spec((M_INNER // BD, BD, BD)),
                  _const_spec((3, SUBLANES, M_INNER)),
                  _const_spec((SUBLANES, tt)),
                  _const_spec((1, M_INNER)),
                  _const_spec((1, M_INNER)),
                  _const_spec((M_INNER, D_MODEL))],
        out_specs=[pl.BlockSpec((1, tt, D_MODEL), lambda b, t: (b, t, 0)),
                   pl.BlockSpec((1, M_HEADS, M_DH, M_DH), lambda b, t: (b, 0, 0, 0)),
                   pl.BlockSpec((1, M_HEADS, M_DH), lambda b, t: (b, 0, 0)),
                   pl.BlockSpec((1, SUBLANES, 128), lambda b, t: (b, 0, 0)),
                   pl.BlockSpec((1, M_CONV - 1, M_INNER), lambda b, t: (b, 0, 0))],
        out_shape=[jax.ShapeDtypeStruct(x.shape, F32),
                   jax.ShapeDtypeStruct((bsz, M_HEADS, M_DH, M_DH), F32),
                   jax.ShapeDtypeStruct((bsz, M_HEADS, M_DH), F32),
                   jax.ShapeDtypeStruct((bsz, SUBLANES, 128), F32),
                   jax.ShapeDtypeStruct((bsz, M_CONV - 1, M_INNER), F32)],
        scratch_shapes=[pltpu.VMEM((SUBLANES, M_INNER), F32), pltpu.VMEM((SUBLANES, 128), F32),
                        pltpu.VMEM((tt, M_INNER), F32), pltpu.VMEM((tt, M_INNER), BF16),
                        pltpu.VMEM((tt, M_INNER), BF16), pltpu.VMEM((tt, M_INNER), BF16)],
        compiler_params=_params(("arbitrary", "arbitrary")),
        name="odd_prompt",
    )(x, g_all, ml['win'], ml['convw'], ml['convb'], ml['wq'], ml['wk'], ml['wv'],
      ml['wgate_t'], bt, ml['normg'], ml['skip'], ml['wdown'])


def _pick_row(ref, b):
    r0 = pl.multiple_of((b // SUBLANES) * SUBLANES, SUBLANES)
    blk = ref[pl.ds(r0, SUBLANES), :]
    row = lax.broadcasted_iota(jnp.int32, blk.shape, 0)
    return jnp.sum(jnp.where(row == b % SUBLANES, blk, 0.0), axis=0, keepdims=True)


def _put_row(ref, b, val):
    r0 = pl.multiple_of((b // SUBLANES) * SUBLANES, SUBLANES)
    blk = ref[pl.ds(r0, SUBLANES), :]
    row = lax.broadcasted_iota(jnp.int32, blk.shape, 0)
    ref[pl.ds(r0, SUBLANES), :] = jnp.where(row == b % SUBLANES, jnp.broadcast_to(val, blk.shape), blk)


def _odd_sample_proj_kernel(x_ref, g_ref, win_ref, convw_ref, convb_ref, cv_ref, wq_ref, wk_ref, wv_ref,
                            wgate_ref, bgate_ref, m_in_ref, n_in_ref,
                            qt_ref, kwt_ref, v_ref, a_ref, den_ref, n_out_ref, m_out_ref, cv_out_ref,
                            xc_ref, z_ref, *, gi):
    hn = _rms(x_ref[...], g_ref[gi:gi + 1, :]).astype(BF16)
    p = _mm(hn, win_ref[...])
    xm = p[:, :M_INNER]
    z_ref[...] = p[:, M_INNER:]
    cv0 = cv_ref[:, :M_INNER]
    cv1 = cv_ref[:, M_INNER:2 * M_INNER]
    cv2 = cv_ref[:, 2 * M_INNER:]
    cv_out_ref[:, :M_INNER] = cv1
    cv_out_ref[:, M_INNER:2 * M_INNER] = cv2
    cv_out_ref[:, 2 * M_INNER:] = xm
    conv = (convb_ref[...] + convw_ref[3:4, :] * xm + convw_ref[2:3, :] * cv2
            + convw_ref[1:2, :] * cv1 + convw_ref[0:1, :] * cv0)
    xc = _silu(conv)
    xc_ref[...] = xc
    xcb = xc.astype(BF16)
    q = _headwise(xcb, wq_ref)
    k = _headwise(xcb, wk_ref)
    v = _headwise(xm.astype(BF16), wv_ref)
    v_ref[...] = v
    gate = (_mm(q.astype(BF16), wgate_ref[0]) + _mm(k.astype(BF16), wgate_ref[1])
            + _mm(v.astype(BF16), wgate_ref[2])) + bgate_ref[...]
    m_in = m_in_ref[...]
    lane = lax.broadcasted_iota(jnp.int32, m_in.shape, 1)
    a_all = jnp.zeros_like(m_in)
    den_all = jnp.zeros_like(m_in)
    m_all = jnp.zeros_like(m_in)
    for h in range(M_HEADS):
        sl = slice(h * M_DH, (h + 1) * M_DH)
        ig = gate[:, h:h + 1]
        lf = _log_sigmoid(gate[:, M_HEADS + h:M_HEADS + h + 1])
        m_prev = m_in[:, h:h + 1]
        m_new = jnp.maximum(lf + m_prev, ig)
        a = jnp.exp(lf + m_prev - m_new)
        kw = (jnp.exp(ig - m_new) * (M_DH ** -0.5)) * k[:, sl]
        n_new = a * n_in_ref[:, sl] + kw
        n_out_ref[:, sl] = n_new
        qh = q[:, sl]
        den = jnp.maximum(jnp.abs(jnp.sum(qh * n_new, axis=-1, keepdims=True)), jnp.exp(-m_new))
        a_all = jnp.where(lane == h, a, a_all)
        den_all = jnp.where(lane == h, den, den_all)
        m_all = jnp.where(lane == h, m_new, m_all)
        for c in range(M_DH // 128):
            qt_ref[h, c * 128:(c + 1) * 128, :] = qh[:, c * 128:(c + 1) * 128].T
            kwt_ref[h, c * 128:(c + 1) * 128, :] = kw[:, c * 128:(c + 1) * 128].T
    a_ref[...] = a_all
    den_ref[...] = den_all
    m_out_ref[...] = m_all


def _odd_sample_proj(x, g_all, gi, ml, n0, m0, conv0):
    n = x.shape[0]
    assert n == 128, "the per-head transposes assume one 128-lane tile of sequences"
    m_pad = jnp.pad(m0, ((0, 0), (0, 128 - M_HEADS)))
    cw = (M_CONV - 1) * M_INNER
    f32 = lambda *shape: jax.ShapeDtypeStruct(shape, F32)
    shapes = [f32(M_HEADS, M_DH, n), f32(M_HEADS, M_DH, n), f32(n, M_INNER), f32(n, 128), f32(n, 128),
              f32(n, M_INNER), f32(n, 128), f32(n, cw), f32(n, M_INNER), f32(n, M_INNER)]
    args = (x, g_all, ml['win'], ml['convw'], ml['convb'], conv0.reshape(n, cw), ml['wq'], ml['wk'], ml['wv'],
            ml['wgate'], ml['bgate'], m_pad, n0.reshape(n, M_INNER))
    outs = pl.pallas_call(
        functools.partial(_odd_sample_proj_kernel, gi=gi),
        grid=(1,),
        in_specs=[_const_spec(a.shape) for a in args],
        out_specs=[pl.BlockSpec(s.shape, lambda i, nd=len(s.shape): (0,) * nd) for s in shapes],
        out_shape=shapes,
        compiler_params=_params(("arbitrary",)),
        name="odd_sample_proj",
    )(*args)
    qt, kwt, v, a, den, n_new, m_new, conv_new, xc, z = outs
    return (qt, kwt, v, a, den, n_new.reshape(n, M_HEADS, M_DH), m_new[:, :M_HEADS],
            conv_new.reshape(n, M_CONV - 1, M_INNER), xc, z)


STREAM_IN_BUFS = 6
STREAM_OUT_BUFS = 6
STREAM_ROWS = 32
FETCH_PRIORITY = 1


def _ffn_stream_kernel(xp_ref, g_ref, gfin_ref, wg_ref, wu_ref, wd_ref, qt_ref, kwt_ref, v_ref, a_ref, c_in,
                       op_ref, c_out, hnum_ref, cin_buf, cout_buf, sem_in, sem_out, *, gi, per_step):
    i = pl.program_id(0)
    n_pieces = per_step * M_HEADS

    def seq(p):
        return i * per_step + p // M_HEADS

    def in_copy(p, ahead=0):
        slot = p % STREAM_IN_BUFS
        s = seq(p) if not ahead else jnp.minimum(seq(p) + per_step, pl.num_programs(0) * per_step - 1)
        return pltpu.make_async_copy(c_in.at[s, p % M_HEADS], cin_buf.at[slot], sem_in.at[slot])

    def out_copy(p):
        slot = p % STREAM_OUT_BUFS
        return pltpu.make_async_copy(cout_buf.at[slot], c_out.at[seq(p), p % M_HEADS], sem_out.at[slot])

    @pl.when(i == 0)
    def _():
        hnum_ref[...] = jnp.zeros_like(hnum_ref)
        for p in range(STREAM_IN_BUFS):
            in_copy(p).start(priority=FETCH_PRIORITY)

    shift = (128 - (i * per_step) % 128) % 128
    rolled = {}

    def columns(ref, h):
        if (id(ref), h) not in rolled:
            rolled[(id(ref), h)] = pltpu.roll(ref[h], shift, 1)
        return rolled[(id(ref), h)]

    rows = {}

    def seq_rows(j):
        if j not in rows:
            s = i * per_step + j
            rows[j] = (_pick_row(a_ref, s), _pick_row(v_ref, s))
        return rows[j]

    h_parts = {}

    def update(p):
        j, h = divmod(p, M_HEADS)
        a_row, v_row = seq_rows(j)
        kw_col = columns(kwt_ref, h)[:, j:j + 1]
        q_col = columns(qt_ref, h)[:, j:j + 1]
        a = a_row[:, h:h + 1]
        v_h = v_row[:, h * M_DH:(h + 1) * M_DH]
        acc = jnp.zeros((STREAM_ROWS, M_DH), F32)
        for r0 in range(0, M_DH, STREAM_ROWS):
            rs = slice(r0, r0 + STREAM_ROWS)
            c_new = a * cin_buf[p % STREAM_IN_BUFS, rs, :] + kw_col[rs] * v_h
            cout_buf[p % STREAM_OUT_BUFS, rs, :] = c_new
            acc = acc + q_col[rs] * c_new
        h_parts[(j, h)] = jnp.sum(acc, axis=0, keepdims=True)
        if h == M_HEADS - 1:
            _put_row(hnum_ref, i * per_step + j, jnp.concatenate([h_parts[(j, hh)] for hh in range(M_HEADS)], axis=1))

    def region(first, matmul):
        ps = (first, first + 1)
        for p in ps:
            in_copy(p).wait()
            if p >= STREAM_OUT_BUFS:
                out_copy(p - STREAM_OUT_BUFS).wait()
            else:
                pl.when(i > 0)(out_copy(p).wait)
        out = matmul()
        for p in ps:
            update(p)
        for p in ps:
            out_copy(p).start()
            if p + STREAM_IN_BUFS < n_pieces:
                in_copy(p + STREAM_IN_BUFS).start(priority=FETCH_PRIORITY)
            else:
                in_copy(p % STREAM_IN_BUFS, ahead=1).start(priority=FETCH_PRIORITY)
        return out

    x = xp_ref[...]
    hb = _rms(x, g_ref[gi:gi + 1, :]).astype(BF16)
    y = None
    first = 0
    for lo, hi in zip(FFN_BOUNDS[:-1], FFN_BOUNDS[1:]):
        half = lo + (hi - lo + MXU_TILE) // (2 * MXU_TILE) * MXU_TILE
        parts = []
        for w_ref in (wg_ref, wu_ref):
            for c0, c1 in ((lo, half), (half, hi)):
                parts.append(region(first, lambda w_ref=w_ref, c0=c0, c1=c1: _mm(hb, w_ref[:, c0:c1])))
                first += 2
        gt = jnp.concatenate(parts[:2], axis=1)
        ut = jnp.concatenate(parts[2:], axis=1)
        d = _mm((_silu(gt) * ut).astype(BF16), wd_ref[lo:hi, :])
        y = d if y is None else y + d
    assert first == n_pieces
    op_ref[...] = _rms(x + 0.5 * y, gfin_ref[...])
    @pl.when(i == pl.num_programs(0) - 1)
    def _():
        for p in range(n_pieces - STREAM_OUT_BUFS, n_pieces):
            out_copy(p).wait()
        for p in range(STREAM_IN_BUFS):
            in_copy(p, ahead=1).wait()


def _ffn_stream(xp, g_all, gfin, wg, wu, wd, gi, tm, qt, kwt, v, a, c0):
    n = xp.shape[0]
    ns = v.shape[0]
    steps = n // tm
    per_step = ns // steps
    assert per_step * steps == ns and 128 % per_step == 0
    any_spec = pl.BlockSpec(memory_space=pl.ANY)
    return pl.pallas_call(
        functools.partial(_ffn_stream_kernel, gi=gi, per_step=per_step),
        grid=(steps,),
        in_specs=[pl.BlockSpec((tm, D_MODEL), lambda i: (i, 0)),
                  _const_spec(g_all.shape),
                  _const_spec((1, D_MODEL)),
                  _const_spec((D_MODEL, D_FF)), _const_spec((D_MODEL, D_FF)), _const_spec((D_FF, D_MODEL)),
                  _const_spec(qt.shape), _const_spec(kwt.shape), _const_spec(v.shape), _const_spec(a.shape),
                  any_spec],
        out_specs=[pl.BlockSpec((tm, D_MODEL), lambda i: (i, 0)),
                   any_spec,
                   pl.BlockSpec((ns, M_INNER), lambda i: (0, 0))],
        out_shape=[jax.ShapeDtypeStruct((n, D_MODEL), F32),
                   jax.ShapeDtypeStruct(c0.shape, F32),
                   jax.ShapeDtypeStruct((ns, M_INNER), F32)],
        scratch_shapes=[pltpu.VMEM((STREAM_IN_BUFS, M_DH, M_DH), F32), pltpu.VMEM((STREAM_OUT_BUFS, M_DH, M_DH), F32),
                        pltpu.SemaphoreType.DMA((STREAM_IN_BUFS,)), pltpu.SemaphoreType.DMA((STREAM_OUT_BUFS,))],
        compiler_params=_params(("arbitrary",)),
        name="ffn_final_stream",
    )(xp, g_all, gfin, wg, wu, wd, qt, kwt, v, a, c0)


def _odd_sample_out_kernel(x_ref, hnum_ref, den_ref, xc_ref, z_ref, normg_ref, skip_ref, wdown_ref,
                           g_ref, gfin_ref, wg_ref, wu_ref, wd_ref, y_ref, *, gi):
    hnum = hnum_ref[...]
    den = den_ref[...]
    hs = [hnum[:, h * M_DH:(h + 1) * M_DH] * (1.0 / den[:, h:h + 1]) for h in range(M_HEADS)]
    xs = x_ref[...] + _mlstm_out(hs, xc_ref[...], z_ref[...], normg_ref, skip_ref, wdown_ref)
    y_ref[...] = _ffn_rows(xs, g_ref[gi:gi + 1, :], wg_ref, wu_ref, wd_ref, gfin_ref[...])


def _odd_sample_out(x, hnum, den, xc, z, ml, g_all, gfin, wg, wu, wd, gi):
    args = (x, hnum, den, xc, z, ml['normg'], ml['skip'], ml['wdown'], g_all, gfin, wg, wu, wd)
    return pl.pallas_call(
        functools.partial(_odd_sample_out_kernel, gi=gi),
        grid=(1,),
        in_specs=[_const_spec(a.shape) for a in args],
        out_specs=pl.BlockSpec(x.shape, lambda i: (0, 0)),
        out_shape=jax.ShapeDtypeStruct(x.shape, F32),
        compiler_params=_params(("arbitrary",)),
        name="odd_sample_out",
    )(*args)


TM_FFN = 512
TT_EVEN = 256
TT_ODD = 512


def kernel(x_prompt, x_sample, state_ret, state_hgrn, state_mlstm_C, state_mlstm_n, state_mlstm_m, state_mlstm_conv,
           norm_g, final_norm_g, ffn_w_gate, ffn_w_up, ffn_w_down, ev_w_in, ev_w_out, ret_norm_g, hg_norm_g,
           hg_lb_logits, ml_w_in, ml_conv_w, ml_conv_b, ml_w_q, ml_w_k, ml_w_v, ml_w_ig, ml_b_ig, ml_w_fg,
           ml_b_fg, ml_norm_g, ml_skip, ml_w_down):
    bp, tp, _ = x_prompt.shape
    ns = x_sample.shape[0]

    g_all = norm_g.reshape(-1, D_MODEL)
    gfin = final_norm_g.reshape(1, D_MODEL)
    lb_all = jnp.cumsum(jax.nn.softmax(hg_lb_logits.astype(F32), axis=0), axis=0)
    lb = lb_all[0].reshape(1, G_HEADS * G_EXP)
    retg = ret_norm_g[0]
    hgg = hg_norm_g[0]
    wgate = _gate_weights(ml_w_ig[0], ml_w_fg[0])
    bgate = jnp.pad(jnp.concatenate([ml_b_ig[0], ml_b_fg[0]]), (0, 128 - 2 * M_HEADS)).reshape(1, 128)
    ml = {
        'convw': ml_conv_w[0],
        'convb': ml_conv_b[0].reshape(1, M_INNER),
        'wq': _block_diag(ml_w_q[0]).astype(BF16),
        'wk': _block_diag(ml_w_k[0]).astype(BF16),
        'wv': _block_diag(ml_w_v[0]).astype(BF16),
        'wgate': wgate.astype(BF16),
        'wgate_t': jnp.swapaxes(wgate[:, :, :SUBLANES], 1, 2).astype(BF16),
        'bgate': bgate,
        'bgate_col': bgate[0, :SUBLANES].reshape(SUBLANES, 1),
        'normg': ml_norm_g[0].reshape(1, M_INNER),
        'skip': ml_skip[0].reshape(1, M_INNER),
    }

    xp = x_prompt.reshape(bp * tp, D_MODEL)
    xs = x_sample.reshape(ns, D_MODEL)
    ffn_w = (ffn_w_gate, ffn_w_up, ffn_w_down)
    next_ffn = lambda layer, idx: [(w, (layer, idx)) for w in ffn_w]

    w00 = [w[0, 0].astype(BF16) for w in ffn_w]
    xp, xs, cast = _ffn(xp, xs, g_all, gfin, *w00, 0, TM_FFN,
                        casts=next_ffn(0, 1) + [(ev_w_in, (0,)), (ev_w_out, (0,))])
    w01, (ev_in, ev_out) = cast[:3], cast[3:]
    xp, ret_p, hg_p = _even_prompt(xp.reshape(bp, tp, D_MODEL), g_all, 1, ev_in, ev_out, retg, hgg, lb, TT_EVEN)
    xs, ret_s, hg_s = _even_sample(xs, g_all, 1, ev_in, ev_out, retg, hgg, lb, state_ret[:, 0], state_hgrn[:, 0])
    xp, xs, cast = _ffn(xp.reshape(bp * tp, D_MODEL), xs, g_all, gfin, *w01, 2, TM_FFN,
                        casts=next_ffn(1, 0) + [(ml_w_in, (0,)), (ml_w_down, (0,))])
    w10, (ml['win'], ml['wdown']) = cast[:3], cast[3:]
    xp, xs, w11 = _ffn(xp, xs, g_all, gfin, *w10, 3, TM_FFN, casts=next_ffn(1, 1))
    xp, c_p, n_p, m_p, conv_p = _odd_prompt(xp.reshape(bp, tp, D_MODEL), g_all, 4, ml, TT_ODD)
    qt, kwt, v_s, a_s, den_s, n_s, m_s, conv_s, xc_s, z_s = _odd_sample_proj(
        xs, g_all, 4, ml, state_mlstm_n[:, 0], state_mlstm_m[:, 0], state_mlstm_conv[:, 0])
    y_p, c_s, hnum = _ffn_stream(xp.reshape(bp * tp, D_MODEL), g_all, gfin, *w11, 5, TM_FFN,
                                 qt, kwt, v_s, a_s, state_mlstm_C[:, 0])
    y_s = _odd_sample_out(xs, hnum, den_s, xc_s, z_s, ml, g_all, gfin, *w11, 5)

    return (y_p.reshape(bp, tp, D_MODEL), y_s.reshape(ns, 1, D_MODEL),
            ret_p[:, None], hg_p[:, None], c_p[:, None], n_p[:, None], m_p[:, None, :M_HEADS, 0], conv_p[:, None],
            ret_s[:, None], hg_s[:, None], c_s[:, None], n_s[:, None], m_s[:, None], conv_s[:, None])
```
